```python
import math
import jax, jax.numpy as jnp
from jax import lax
import numpy as np

D_MODEL = 1024
BATCH = 8
SEQ = 2048
DEPTH = 4
DEC_BATCH = 128
DEC_SEQ = 8
PAST_LEN = 16384
PAGE_SIZE = 128

N_MIXERS = 2
N_SSD_LAYERS = (DEPTH + N_MIXERS - 1) // N_MIXERS
N_S5_LAYERS = DEPTH // N_MIXERS
SSD_EXPAND = 2
D_INNER = SSD_EXPAND * D_MODEL
HEAD_DIM = 64
N_HEADS = D_INNER // HEAD_DIM
N_GROUPS = 8
HEADS_PER_GROUP = N_HEADS // N_GROUPS
D_STATE = 128
CONV_K = 4
CONV_DIM = D_INNER + 2 * N_GROUPS * D_STATE
D_IN_PROJ = 2 * D_INNER + 2 * N_GROUPS * D_STATE + N_HEADS
CHUNK = 128
S5_GROUP = 16
S5_N_GROUPS = D_MODEL // S5_GROUP
S5_STATE = 64
D_FF = -(-8 * D_MODEL // (3 * 256)) * 256
RMS_EPS = 1e-6

kernel_name = 'ssd_s5_hybrid_adaln_step'


def rms(x):
    xf = x.astype(jnp.float32)
    return (xf * lax.rsqrt(jnp.mean(xf * xf, axis=-1, keepdims=True) + RMS_EPS)).astype(x.dtype)


def modulate(x, shift, scale):
    return rms(x) * (1.0 + scale[:, None, :]) + shift[:, None, :]


def ssd_scan(x, dt, A, B, C, s0):
    f32 = jnp.float32
    b, T = x.shape[:2]
    L = CHUNK if T % CHUNK == 0 else T
    nc = T // L
    x, dt, B, C, s0 = (t.astype(f32) for t in (x, dt, B, C, s0))
    xd = x * dt[..., None]
    a = dt * A
    ch = lambda t: t.reshape((b, nc, L) + t.shape[2:])
    xd, a, Bc, Cc = ch(xd), ch(a), ch(B), ch(C)
    a_cs = jnp.cumsum(a, axis=2)
    causal = jnp.tril(jnp.ones((L, L), bool))[:, :, None, None]
    seg = a_cs[:, :, :, None] - a_cs[:, :, None, :]
    decay = jnp.exp(jnp.where(causal, seg, -jnp.inf))
    CB = jnp.einsum('bclgn,bcsgn->bclsg', Cc, Bc)
    y_diag = jnp.einsum('bclsg,bclsgr,bcsgrp->bclgrp', CB, decay, xd)
    decay_to_end = jnp.exp(a_cs[:, :, -1:] - a_cs)
    chunk_states = jnp.einsum('bclgn,bclgr,bclgrp->bcgrpn', Bc, decay_to_end, xd)
    chunk_decay = jnp.exp(a_cs[:, :, -1])

    def step(s, inp):
        dec, st = inp
        return dec[..., None, None] * s + st, s

    s_final, s_in = lax.scan(step, s0, (jnp.moveaxis(chunk_decay, 1, 0), jnp.moveaxis(chunk_states, 1, 0)))
    s_in = jnp.moveaxis(s_in, 0, 1)
    y_off = jnp.einsum('bclgn,bcgrpn,bclgr->bclgrp', Cc, s_in, jnp.exp(a_cs))
    y = (y_diag + y_off).reshape(b, T, N_GROUPS, HEADS_PER_GROUP, HEAD_DIM)
    return y, s_final


def ssd_mixer(h, ssm0, conv0, in_w, conv_w, conv_b, dt_bias, A_log, Dskip, norm_w, out_w):
    b, T, _ = h.shape
    zxbcdt = h @ in_w
    z, xbc, dt = jnp.split(zxbcdt, [D_INNER, D_INNER + CONV_DIM], axis=-1)
    full = jnp.concatenate([conv0.astype(xbc.dtype), xbc], axis=1)
    new_conv = full[:, -(CONV_K - 1):]
    xbc = lax.conv_general_dilated(full, conv_w[:, None, :].astype(full.dtype), (1,), 'VALID',
                                   dimension_numbers=('NWC', 'WIO', 'NWC'),
                                   feature_group_count=CONV_DIM) + conv_b
    xbc = jax.nn.silu(xbc)
    xs, Bm, Cm = jnp.split(xbc, [D_INNER, D_INNER + N_GROUPS * D_STATE], axis=-1)
    xs = xs.reshape(b, T, N_GROUPS, HEADS_PER_GROUP, HEAD_DIM)
    Bm = Bm.reshape(b, T, N_GROUPS, D_STATE)
    Cm = Cm.reshape(b, T, N_GROUPS, D_STATE)
    dt = jax.nn.softplus((dt + dt_bias).astype(jnp.float32)).reshape(b, T, N_GROUPS, HEADS_PER_GROUP)
    A = -jnp.exp(A_log.astype(jnp.float32)).reshape(N_GROUPS, HEADS_PER_GROUP)
    s0 = ssm0.reshape(b, N_GROUPS, HEADS_PER_GROUP, HEAD_DIM, D_STATE)
    y, s_new = ssd_scan(xs, dt, A, Bm, Cm, s0)
    y = y + Dskip.reshape(N_GROUPS, HEADS_PER_GROUP)[..., None] * xs
    y = y.reshape(b, T, D_INNER) * jax.nn.silu(z)
    y = rms(y.reshape(b, T, N_GROUPS, D_INNER // N_GROUPS)).reshape(b, T, D_INNER) * norm_w
    out = (y @ out_w).astype(h.dtype)
    return out, s_new.reshape(b, N_HEADS, HEAD_DIM, D_STATE).astype(ssm0.dtype), new_conv.astype(conv0.dtype)


def s5_mixer(h, s0_re, s0_im, A_re, A_im, log_dt, B_re, B_im, C_re, C_im, Dskip, glu_w):
    f32 = jnp.float32
    b, T, _ = h.shape
    u = h.astype(f32).reshape(b, T, S5_N_GROUPS, S5_GROUP)
    dt = jnp.exp(log_dt.astype(f32))[:, None]
    lre, lim = A_re.astype(f32), A_im.astype(f32)
    mag = jnp.exp(lre * dt)
    ab_re, ab_im = mag * jnp.cos(lim * dt), mag * jnp.sin(lim * dt)
    den = lre * lre + lim * lim
    nr, ni = ab_re - 1.0, ab_im
    q_re = (nr * lre + ni * lim) / den
    q_im = (ni * lre - nr * lim) / den
    Br, Bi = B_re.astype(f32), B_im.astype(f32)
    bb_re = q_re[..., None] * Br - q_im[..., None] * Bi
    bb_im = q_re[..., None] * Bi + q_im[..., None] * Br
    bu_re = jnp.einsum('btgi,gni->btgn', u, bb_re)
    bu_im = jnp.einsum('btgi,gni->btgn', u, bb_im)
    s0r, s0i = s0_re.astype(f32), s0_im.astype(f32)
    bu_re = bu_re.at[:, 0].add(ab_re * s0r - ab_im * s0i)
    bu_im = bu_im.at[:, 0].add(ab_re * s0i + ab_im * s0r)
    a_re = jnp.broadcast_to(ab_re, bu_re.shape)
    a_im = jnp.broadcast_to(ab_im, bu_im.shape)

    def combine(e1, e2):
        a1r, a1i, b1r, b1i = e1
        a2r, a2i, b2r, b2i = e2
        return (a2r * a1r - a2i * a1i, a2r * a1i + a2i * a1r,
                a2r * b1r - a2i * b1i + b2r, a2r * b1i + a2i * b1r + b2i)

    _, _, xr, xi = lax.associative_scan(combine, (a_re, a_im, bu_re, bu_im), axis=1)
    y = jnp.einsum('btgn,gin->btgi', xr, C_re.astype(f32)) - jnp.einsum('btgn,gin->btgi', xi, C_im.astype(f32))
    y = y.reshape(b, T, D_MODEL) + Dskip.astype(f32) * h.astype(f32)
    y = jax.nn.gelu(y)
    ga, gb = jnp.split(y @ glu_w.astype(f32), 2, axis=-1)
    out = (ga * jax.nn.sigmoid(gb)).astype(h.dtype)
    return out, xr[:, -1].astype(s0_re.dtype), xi[:, -1].astype(s0_im.dtype)


def swiglu(h, w_in, w_out):
    g, u = jnp.split(h @ w_in, 2, axis=-1)
    return (jax.nn.silu(g) * u) @ w_out


def trunk(x, c, ssm0, conv0, re0, im0, p):
    silu_c = jax.nn.silu(c)
    ssm_new, conv_new, re_new, im_new = [], [], [], []
    for i in range(DEPTH):
        mod = silu_c @ p['ada_w'][i] + p['ada_b'][i]
        sh1, sc1, g1, sh2, sc2, g2 = jnp.split(mod, 6, axis=-1)
        h = modulate(x, sh1, sc1)
        j = i // N_MIXERS
        if i % N_MIXERS == 0:
            out, s, cv = ssd_mixer(h, ssm0[j], conv0[j], p['ssd_in_w'][j], p['ssd_conv_w'][j], p['ssd_conv_b'][j],
                                   p['ssd_dt_bias'][j], p['ssd_A_log'][j], p['ssd_D'][j], p['ssd_norm_w'][j],
                                   p['ssd_out_w'][j])
            ssm_new.append(s)
            conv_new.append(cv)
        else:
            out, sr, si = s5_mixer(h, re0[j], im0[j], p['s5_A_re'][j], p['s5_A_im'][j], p['s5_log_dt'][j],
                                   p['s5_B_re'][j], p['s5_B_im'][j], p['s5_C_re'][j], p['s5_C_im'][j],
                                   p['s5_D'][j], p['s5_glu_w'][j])
            re_new.append(sr)
            im_new.append(si)
        x = x + g1[:, None, :] * out
        h = modulate(x, sh2, sc2)
        x = x + g2[:, None, :] * swiglu(h, p['ffn_w_in'][i], p['ffn_w_out'][i])
    y = rms(x) * p['final_norm_w']
    return y, jnp.stack(ssm_new), jnp.stack(conv_new), jnp.stack(re_new), jnp.stack(im_new)


def setup_inputs(seed: int = 0) -> dict:
    key = jax.random.key(seed)
    ks = iter(jax.random.split(key, 48))
    f32 = jnp.float32
    nrm = lambda shape, s: jax.random.normal(next(ks), shape, f32) * s
    uni = lambda shape, lo, hi: jax.random.uniform(next(ks), shape, f32, lo, hi)
    dt0 = jnp.exp(uni((N_SSD_LAYERS, N_HEADS), math.log(1e-3), math.log(1e-1)))
    dt_bias = dt0 + jnp.log(-jnp.expm1(-dt0))
    a_im0 = jnp.pi * jnp.arange(S5_STATE, dtype=f32)
    return {
        'x_prompt': nrm((BATCH, SEQ, D_MODEL), 1.0),
        'x_sample': nrm((DEC_BATCH, DEC_SEQ, D_MODEL), 1.0),
        'state_ssm': nrm((N_SSD_LAYERS, DEC_BATCH, N_HEADS, HEAD_DIM, D_STATE), 0.3),
        'state_conv': nrm((N_SSD_LAYERS, DEC_BATCH, CONV_K - 1, CONV_DIM), 1.0),
        'state_s5_re': nrm((N_S5_LAYERS, DEC_BATCH, S5_N_GROUPS, S5_STATE), 0.3),
        'state_s5_im': nrm((N_S5_LAYERS, DEC_BATCH, S5_N_GROUPS, S5_STATE), 0.3),
        'c_prompt': nrm((BATCH, D_MODEL), 1.0),
        'c_sample': nrm((DEC_BATCH, D_MODEL), 1.0),
        'ada_w': nrm((DEPTH, D_MODEL, 6 * D_MODEL), 0.5 * D_MODEL ** -0.5),
        'ada_b': nrm((DEPTH, 6 * D_MODEL), 0.1),
        'ssd_in_w': nrm((N_SSD_LAYERS, D_MODEL, D_IN_PROJ), D_MODEL ** -0.5),
        'ssd_conv_w': nrm((N_SSD_LAYERS, CONV_K, CONV_DIM), CONV_K ** -0.5),
        'ssd_conv_b': nrm((N_SSD_LAYERS, CONV_DIM), 0.02),
        'ssd_dt_bias': dt_bias,
        'ssd_A_log': jnp.log(uni((N_SSD_LAYERS, N_HEADS), 1.0, 16.0)),
        'ssd_D': 1.0 + nrm((N_SSD_LAYERS, N_HEADS), 0.1),
        'ssd_norm_w': 1.0 + nrm((N_SSD_LAYERS, D_INNER), 0.05),
        'ssd_out_w': nrm((N_SSD_LAYERS, D_INNER, D_MODEL), D_INNER ** -0.5),
        's5_A_re': -0.5 + nrm((N_S5_LAYERS, S5_N_GROUPS, S5_STATE), 0.01),
        's5_A_im': a_im0 + nrm((N_S5_LAYERS, S5_N_GROUPS, S5_STATE), 0.01),
        's5_log_dt': uni((N_S5_LAYERS, S5_N_GROUPS), math.log(1e-3), math.log(1e-1)),
        's5_B_re': nrm((N_S5_LAYERS, S5_N_GROUPS, S5_STATE, S5_GROUP), (2 * S5_GROUP) ** -0.5),
        's5_B_im': nrm((N_S5_LAYERS, S5_N_GROUPS, S5_STATE, S5_GROUP), (2 * S5_GROUP) ** -0.5),
        's5_C_re': nrm((N_S5_LAYERS, S5_N_GROUPS, S5_GROUP, S5_STATE), S5_STATE ** -0.5),
        's5_C_im': nrm((N_S5_LAYERS, S5_N_GROUPS, S5_GROUP, S5_STATE), S5_STATE ** -0.5),
        's5_D': nrm((N_S5_LAYERS, D_MODEL), 0.5),
        's5_glu_w': nrm((N_S5_LAYERS, D_MODEL, 2 * D_MODEL), D_MODEL ** -0.5),
        'ffn_w_in': nrm((DEPTH, D_MODEL, 2 * D_FF), D_MODEL ** -0.5),
        'ffn_w_out': nrm((DEPTH, D_FF, D_MODEL), D_FF ** -0.5),
        'final_norm_w': 1.0 + nrm((D_MODEL,), 0.05),
    }


def reference(x_prompt, x_sample, state_ssm, state_conv, state_s5_re, state_s5_im, c_prompt, c_sample,
              ada_w, ada_b, ssd_in_w, ssd_conv_w, ssd_conv_b, ssd_dt_bias, ssd_A_log, ssd_D, ssd_norm_w,
              ssd_out_w, s5_A_re, s5_A_im, s5_log_dt, s5_B_re, s5_B_im, s5_C_re, s5_C_im, s5_D, s5_glu_w,
              ffn_w_in, ffn_w_out, final_norm_w):
    p = dict(ada_w=ada_w, ada_b=ada_b, ssd_in_w=ssd_in_w, ssd_conv_w=ssd_conv_w, ssd_conv_b=ssd_conv_b,
             ssd_dt_bias=ssd_dt_bias, ssd_A_log=ssd_A_log, ssd_D=ssd_D, ssd_norm_w=ssd_norm_w,
             ssd_out_w=ssd_out_w, s5_A_re=s5_A_re, s5_A_im=s5_A_im, s5_log_dt=s5_log_dt, s5_B_re=s5_B_re,
             s5_B_im=s5_B_im, s5_C_re=s5_C_re, s5_C_im=s5_C_im, s5_D=s5_D, s5_glu_w=s5_glu_w,
             ffn_w_in=ffn_w_in, ffn_w_out=ffn_w_out, final_norm_w=final_norm_w)
    b = x_prompt.shape[0]
    dtp = state_ssm.dtype
    ssm0_p = jnp.zeros((N_SSD_LAYERS, b, N_HEADS, HEAD_DIM, D_STATE), dtp)
    conv0_p = jnp.zeros((N_SSD_LAYERS, b, CONV_K - 1, CONV_DIM), state_conv.dtype)
    re0_p = jnp.zeros((N_S5_LAYERS, b, S5_N_GROUPS, S5_STATE), state_s5_re.dtype)
    im0_p = jnp.zeros((N_S5_LAYERS, b, S5_N_GROUPS, S5_STATE), state_s5_im.dtype)
    y_prompt, ssm_p, conv_p, re_p, im_p = trunk(x_prompt, c_prompt, ssm0_p, conv0_p, re0_p, im0_p, p)
    y_sample, ssm_s, conv_s, re_s, im_s = trunk(x_sample, c_sample, state_ssm, state_conv,
                                                state_s5_re, state_s5_im, p)
    return (y_prompt, y_sample, ssm_p, conv_p, re_p, im_p, ssm_s, conv_s, re_s, im_s)
```

```python
import functools
import math

import jax
import jax.numpy as jnp
from jax import lax
from jax.experimental import pallas as pl
from jax.experimental.pallas import tpu as pltpu

F32 = jnp.float32
BF16 = jnp.bfloat16

D_MODEL = 1024
DEPTH = 4
D_INNER = 2048
HEAD_DIM = 64
N_HEADS = 32
N_GROUPS = 8
HEADS_PER_GROUP = 4
D_STATE = 128
CONV_K = 4
BC_DIM = 2 * N_GROUPS * D_STATE
CONV_DIM = D_INNER + BC_DIM
ZX_DIM = D_INNER + CONV_DIM
SSD_CHUNK = 128
S5_GROUPS = 64
S5_GROUP = 16
S5_STATE = 64
S5_LANES = S5_GROUPS * S5_STATE
S5_BLOCKS = D_MODEL // 128
D_FF = 2816
RMS_EPS = 1e-6
LANES = 128
SUBLANES = 8
CONV_PAD = 8
VMEM_LIMIT = 52 * 1024 * 1024


def _cparams(sem):
    return pltpu.CompilerParams(dimension_semantics=sem, vmem_limit_bytes=VMEM_LIMIT)


def _sigmoid(x):
    return 1.0 / (1.0 + jnp.exp(-x))


def _silu(x):
    return x * _sigmoid(x)


def _rows_by_batch(x, nb):
    tm, d = x.shape
    return x.reshape(tm // nb, nb, d)


def _modulate_rows(x, sh, sc, nb):
    x3 = _rows_by_batch(x, nb)
    r = lax.rsqrt(jnp.mean(x3 * x3, axis=-1, keepdims=True) + RMS_EPS)
    h = (x3 * r) * (1.0 + sc[None]) + sh[None]
    return h.reshape(x.shape)


def _gate_rows(v, g, nb):
    return (_rows_by_batch(v, nb) * g[None]).reshape(v.shape)


def _dot(a, b):
    return jnp.dot(a, b, preferred_element_type=F32)


def _dot_nt(a, b):
    return lax.dot_general(a, b, (((1,), (1,)), ((), ())), preferred_element_type=F32)


def _dot_tn(a, b):
    return lax.dot_general(a, b, (((0,), (0,)), ((), ())), preferred_element_type=F32)


def _ada_kernel(c_ref, w_ref, b_ref, o_ref):
    s = _silu(c_ref[...]).astype(BF16)
    o_ref[0] = _dot(s, w_ref[0].astype(BF16)) + b_ref[0]


def _ada(c_all, ada_w, ada_b):
    nrow = c_all.shape[0]
    tn = 1536
    return pl.pallas_call(
        _ada_kernel,
        grid=(DEPTH, 6 * D_MODEL // tn),
        in_specs=[
            pl.BlockSpec((nrow, D_MODEL), lambda l, j: (0, 0)),
            pl.BlockSpec((1, D_MODEL, tn), lambda l, j: (l, 0, j)),
            pl.BlockSpec((1, 1, tn), lambda l, j: (l, 0, j)),
        ],
        out_specs=pl.BlockSpec((1, nrow, tn), lambda l, j: (l, 0, j)),
        out_shape=jax.ShapeDtypeStruct((DEPTH, nrow, 6 * D_MODEL), F32),
        compiler_params=_cparams(("arbitrary", "arbitrary")),
        name="ada",
    )(c_all, ada_w, ada_b.reshape(DEPTH, 1, 6 * D_MODEL))


def _inproj_kernel(x_ref, sh_ref, sc_ref, w_ref, wdt_ref, zx_ref, dt_ref, h_scr, *, nb):
    j = pl.program_id(1)

    @pl.when(j == 0)
    def _():
        h = _modulate_rows(x_ref[...], sh_ref[...], sc_ref[...], nb).astype(BF16)
        h_scr[...] = h
        dt_ref[...] = _dot(h, wdt_ref[...])

    zx_ref[...] = _dot(h_scr[...], w_ref[...])


def _inproj(x2d, sh, sc, w, wdt, nb, tm):
    rows = x2d.shape[0]
    tn = 2048
    return pl.pallas_call(
        functools.partial(_inproj_kernel, nb=nb),
        grid=(rows // tm, ZX_DIM // tn),
        in_specs=[
            pl.BlockSpec((tm, D_MODEL), lambda i, j: (i, 0)),
            pl.BlockSpec((nb, D_MODEL), lambda i, j: (0, 0)),
            pl.BlockSpec((nb, D_MODEL), lambda i, j: (0, 0)),
            pl.BlockSpec((D_MODEL, tn), lambda i, j: (0, j)),
            pl.BlockSpec((D_MODEL, LANES), lambda i, j: (0, 0)),
        ],
        out_specs=[
            pl.BlockSpec((tm, tn), lambda i, j: (i, j)),
            pl.BlockSpec((tm, LANES), lambda i, j: (i, 0)),
        ],
        out_shape=[
            jax.ShapeDtypeStruct((rows, ZX_DIM), F32),
            jax.ShapeDtypeStruct((rows, LANES), F32),
        ],
        scratch_shapes=[pltpu.VMEM((tm, D_MODEL), BF16)],
        compiler_params=_cparams(("arbitrary", "arbitrary")),
        name="ssd_inproj",
    )(x2d, sh, sc, w, wdt)


def _cumsum_rows(a, L):
    rows = lax.broadcasted_iota(jnp.int32, a.shape, 0)
    k = 1
    while k < L:
        a = a + jnp.where(rows >= k, pltpu.roll(a, k, 0), 0.0)
        k *= 2
    return a


def _transpose_rows(a, L):
    if L < LANES:
        a = jnp.concatenate([a, jnp.zeros((LANES - L, LANES), F32)], axis=0)
    return a.T[:, :L]


def _ssd_kernel(z_ref, xr_ref, bcr_ref, dt_ref, conv0_ref, ssm0_ref,
                cw_ref, cb_ref, dtb_ref, alog_ref, dexp_ref, nw_ref,
                y_ref, ssm_out_ref, conv_out_ref,
                xpad, bcpad, state, *, L):
    c = pl.program_id(1)
    hist = CONV_K - 1
    lo = CONV_PAD - hist

    @pl.when(c == 0)
    def _():
        state[...] = ssm0_ref[0]
        xpad[lo:CONV_PAD, :] = conv0_ref[0, :, :D_INNER]
        bcpad[lo:CONV_PAD, :] = conv0_ref[0, :, D_INNER:]

    xpad[CONV_PAD:CONV_PAD + L, :] = xr_ref[...]
    bcpad[CONV_PAD:CONV_PAD + L, :] = bcr_ref[...]

    def conv(pad, off):
        acc = cb_ref[:, off:off + D_INNER]
        for k in range(CONV_K):
            acc = acc + cw_ref[k:k + 1, off:off + D_INNER] * pad[lo + k:lo + k + L, :]
        return _silu(acc)

    xs = conv(xpad, 0)
    bc = conv(bcpad, D_INNER)

    tail_x = xpad[lo + L:CONV_PAD + L, :]
    tail_bc = bcpad[lo + L:CONV_PAD + L, :]
    xpad[lo:CONV_PAD, :] = tail_x
    bcpad[lo:CONV_PAD, :] = tail_bc
    conv_out_ref[0, :, :D_INNER] = tail_x
    conv_out_ref[0, :, D_INNER:] = tail_bc

    dtr = dt_ref[...] + dtb_ref[...]
    dt = jnp.maximum(dtr, 0.0) + jnp.log1p(jnp.exp(-jnp.abs(dtr)))
    a = dt * (-jnp.exp(alog_ref[...]))
    acs = _cumsum_rows(a, L)
    acs_t = _transpose_rows(acs, L)
    dt_t = _transpose_rows(dt, L)
    a_last = acs[L - 1:L, :]
    e_acs = jnp.exp(acs)
    w_end = jnp.exp(a_last - acs) * dt
    e_last = jnp.exp(a_last)

    causal = (lax.broadcasted_iota(jnp.int32, (L, L), 0)
              >= lax.broadcasted_iota(jnp.int32, (L, L), 1))
    z = z_ref[...]

    for g in range(N_GROUPS):
        bg = bc[:, g * D_STATE:(g + 1) * D_STATE].astype(BF16)
        cg = bc[:, (N_GROUPS + g) * D_STATE:(N_GROUPS + g + 1) * D_STATE].astype(BF16)
        cb = _dot_nt(cg, bg)
        r0 = g * HEADS_PER_GROUP * HEAD_DIM
        sg = state[r0:r0 + HEADS_PER_GROUP * HEAD_DIM, :]
        y_off = _dot_nt(cg, sg.astype(BF16))
        ys, xws = [], []
        for r in range(HEADS_PER_GROUP):
            h = g * HEADS_PER_GROUP + r
            col = acs[:, h:h + 1]
            seg = col - acs_t[h:h + 1, :]
            decay = jnp.exp(jnp.where(causal, seg, -jnp.inf))
            m = (cb * decay * dt_t[h:h + 1, :]).astype(BF16)
            xh = xs[:, h * HEAD_DIM:(h + 1) * HEAD_DIM]
            yh = (_dot(m, xh.astype(BF16))
                  + e_acs[:, h:h + 1] * y_off[:, r * HEAD_DIM:(r + 1) * HEAD_DIM]
                  + dexp_ref[:, h * HEAD_DIM:(h + 1) * HEAD_DIM] * xh)
            ys.append(yh)
            xws.append(xh * w_end[:, h:h + 1])
        upd = _dot_tn(jnp.concatenate(xws, axis=1).astype(BF16), bg)
        for r in range(HEADS_PER_GROUP):
            h = g * HEADS_PER_GROUP + r
            rows = slice(r0 + r * HEAD_DIM, r0 + (r + 1) * HEAD_DIM)
            state[rows, :] = (e_last[:, h:h + 1] * sg[r * HEAD_DIM:(r + 1) * HEAD_DIM, :]
                              + upd[r * HEAD_DIM:(r + 1) * HEAD_DIM, :])
        gw = HEADS_PER_GROUP * HEAD_DIM
        yg = jnp.concatenate(ys, axis=1) * _silu(z[:, r0:r0 + gw])
        yg = yg * lax.rsqrt(jnp.mean(yg * yg, axis=-1, keepdims=True) + RMS_EPS)
        y_ref[:, r0:r0 + gw] = (yg * nw_ref[:, r0:r0 + gw]).astype(y_ref.dtype)

    @pl.when(c == pl.num_programs(1) - 1)
    def _():
        ssm_out_ref[0] = state[...]


def _ssd_core(zx, dtraw, conv0, ssm0, cw, cb, dtb, alog, dexp, nw, T, B, L):
    nc = T // L
    nblk = ZX_DIM // D_INNER
    full = lambda shape: pl.BlockSpec(shape, lambda b, c: (0,) * len(shape))
    return pl.pallas_call(
        functools.partial(_ssd_kernel, L=L),
        grid=(B, nc),
        in_specs=[
            pl.BlockSpec((L, D_INNER), lambda b, c: (c, nblk * b)),
            pl.BlockSpec((L, D_INNER), lambda b, c: (c, nblk * b + 1)),
            pl.BlockSpec((L, D_INNER), lambda b, c: (c, nblk * b + 2)),
            pl.BlockSpec((L, LANES), lambda b, c: (c, b)),
            pl.BlockSpec((1, CONV_K - 1, CONV_DIM), lambda b, c: (b, 0, 0)),
            pl.BlockSpec((1, N_HEADS * HEAD_DIM, D_STATE), lambda b, c: (b, 0, 0)),
            full((CONV_K, CONV_DIM)),
            full((1, CONV_DIM)),
            full((1, LANES)),
            full((1, LANES)),
            full((1, D_INNER)),
            full((1, D_INNER)),
        ],
        out_specs=[
            pl.BlockSpec((L, D_INNER), lambda b, c: (c, b)),
            pl.BlockSpec((1, N_HEADS * HEAD_DIM, D_STATE), lambda b, c: (b, 0, 0)),
            pl.BlockSpec((1, CONV_K - 1, CONV_DIM), lambda b, c: (b, 0, 0)),
        ],
        out_shape=[
            jax.ShapeDtypeStruct((T, B * D_INNER), BF16),
            jax.ShapeDtypeStruct((B, N_HEADS * HEAD_DIM, D_STATE), F32),
            jax.ShapeDtypeStruct((B, CONV_K - 1, CONV_DIM), F32),
        ],
        scratch_shapes=[
            pltpu.VMEM((CONV_PAD + L, D_INNER), F32),
            pltpu.VMEM((CONV_PAD + L, D_INNER), F32),
            pltpu.VMEM((N_HEADS * HEAD_DIM, D_STATE), F32),
        ],
        compiler_params=_cparams(("arbitrary", "arbitrary")),
        name="ssd_core",
    )(zx, zx, zx, dtraw, conv0, ssm0, cw, cb, dtb, alog, dexp, nw)


def _ffn_kernel(*refs, nb, has_proj, final):
    if has_proj:
        (x_ref, y_ref, ow_ref, g1_ref, sh_ref, sc_ref, g2_ref, wg_ref, wu_ref, wo_ref, fw_ref,
         o_ref, x1_scr, h_scr, acc_scr) = refs
    else:
        (x_ref, sh_ref, sc_ref, g2_ref, wg_ref, wu_ref, wo_ref, fw_ref,
         o_ref, x1_scr, h_scr, acc_scr) = refs
    j = pl.program_id(1)

    @pl.when(j == 0)
    def _():
        x = x_ref[...]
        if has_proj:
            x = x + _gate_rows(_dot(y_ref[...], ow_ref[...]), g1_ref[...], nb)
        x1_scr[...] = x
        h_scr[...] = _modulate_rows(x, sh_ref[...], sc_ref[...], nb).astype(BF16)
        acc_scr[...] = jnp.zeros_like(acc_scr)

    h = h_scr[...]
    act = (_silu(_dot(h, wg_ref[...])) * _dot(h, wu_ref[...])).astype(BF16)
    acc_scr[...] += _dot(act, wo_ref[...])

    @pl.when(j == pl.num_programs(1) - 1)
    def _():
        x2 = x1_scr[...] + _gate_rows(acc_scr[...], g2_ref[...], nb)
        if final:
            x2 = x2 * lax.rsqrt(jnp.mean(x2 * x2, axis=-1, keepdims=True) + RMS_EPS) * fw_ref[...]
        o_ref[...] = x2


def _ffn(x2d, y2d, out_w, g1, sh, sc, g2, w_in, w_out, fw, nb, tm, final):
    rows = x2d.shape[0]
    tf = D_FF // 2
    nj = D_FF // tf
    has_proj = y2d is not None
    row_blk = lambda w: pl.BlockSpec((tm, w), lambda i, j: (i, 0))
    per_b = pl.BlockSpec((nb, D_MODEL), lambda i, j: (0, 0))
    in_specs = [row_blk(D_MODEL)]
    args = [x2d]
    if has_proj:
        in_specs += [row_blk(D_INNER), pl.BlockSpec((D_INNER, D_MODEL), lambda i, j: (0, 0)), per_b]
        args += [y2d, out_w, g1]
    in_specs += [
        per_b, per_b, per_b,
        pl.BlockSpec((D_MODEL, tf), lambda i, j: (0, j)),
        pl.BlockSpec((D_MODEL, tf), lambda i, j: (0, nj + j)),
        pl.BlockSpec((tf, D_MODEL), lambda i, j: (j, 0)),
        pl.BlockSpec((1, D_MODEL), lambda i, j: (0, 0)),
    ]
    args += [sh, sc, g2, w_in, w_in, w_out, fw]
    return pl.pallas_call(
        functools.partial(_ffn_kernel, nb=nb, has_proj=has_proj, final=final),
        grid=(rows // tm, nj),
        in_specs=in_specs,
        out_specs=pl.BlockSpec((tm, D_MODEL), lambda i, j: (i, 0)),
        out_shape=jax.ShapeDtypeStruct((rows, D_MODEL), F32),
        scratch_shapes=[
            pltpu.VMEM((tm, D_MODEL), F32),
            pltpu.VMEM((tm, D_MODEL), BF16),
            pltpu.VMEM((tm, D_MODEL), F32),
        ],
        compiler_params=_cparams(("arbitrary", "arbitrary")),
        name="proj_ffn" if has_proj else "ffn",
    )(*args)


def _s5_disc_kernel(are_ref, aim_ref, ldt_ref, bre_ref, bim_ref, abre_ref, abim_ref, bbre_ref, bbim_ref):
    lre, lim = are_ref[...], aim_ref[...]
    dt = jnp.exp(ldt_ref[...])
    mag = jnp.exp(lre * dt)
    ab_re, ab_im = mag * jnp.cos(lim * dt), mag * jnp.sin(lim * dt)
    den = lre * lre + lim * lim
    nr, ni = ab_re - 1.0, ab_im
    q_re = (nr * lre + ni * lim) / den
    q_im = (ni * lre - nr * lim) / den
    abre_ref[...] = ab_re
    abim_ref[...] = ab_im
    br, bi = bre_ref[...], bim_ref[...]
    bbre_ref[...] = q_re[:, None, :] * br - q_im[:, None, :] * bi
    bbim_ref[...] = q_re[:, None, :] * bi + q_im[:, None, :] * br


def _s5_disc(a_re, a_im, log_dt, b_re, b_im):
    gn = jax.ShapeDtypeStruct((S5_GROUPS, S5_STATE), F32)
    gin = jax.ShapeDtypeStruct((S5_GROUPS, S5_GROUP, S5_STATE), F32)
    return pl.pallas_call(
        _s5_disc_kernel,
        out_shape=[gn, gn, gin, gin],
        name="s5_disc",
    )(a_re, a_im, log_dt.reshape(S5_GROUPS, 1), b_re, b_im)


S5_SCAN_LANES = 512


def _s5_kernel(x_ref, sh_ref, sc_ref, g1_ref, bb_ref, cc_ref, are_ref, aim_ref, s0re_ref, s0im_ref,
               dskip_ref, glu_ref, o_ref, sre_ref, sim_ref, bur, bui, st_re, st_im, *, tb, bb):
    t_idx = pl.program_id(1)
    rows = tb * bb

    @pl.when(t_idx == 0)
    def _():
        st_re[...] = s0re_ref[...]
        st_im[...] = s0im_ref[...]

    x = x_ref[...].reshape(rows, D_MODEL)
    h = _modulate_rows(x, sh_ref[...], sc_ref[...], bb)
    hb = h.astype(BF16)
    half = S5_LANES // S5_BLOCKS
    for k in range(S5_BLOCKS):
        bu = _dot(hb[:, k * LANES:(k + 1) * LANES], bb_ref[k])
        bur[:, k * half:(k + 1) * half] = bu[:, :half]
        bui[:, k * half:(k + 1) * half] = bu[:, half:]

    for rg in range(bb // SUBLANES):
        for lb in range(S5_LANES // S5_SCAN_LANES):
            ls = slice(lb * S5_SCAN_LANES, (lb + 1) * S5_SCAN_LANES)
            rs = slice(rg * SUBLANES, (rg + 1) * SUBLANES)
            ar, ai = are_ref[:, ls], aim_ref[:, ls]

            def step(t, carry):
                xr, xi = carry
                r0 = pl.multiple_of(t * bb + rg * SUBLANES, SUBLANES)
                nr = ar * xr - ai * xi + bur[pl.ds(r0, SUBLANES), ls]
                ni = ar * xi + ai * xr + bui[pl.ds(r0, SUBLANES), ls]
                bur[pl.ds(r0, SUBLANES), ls] = nr
                bui[pl.ds(r0, SUBLANES), ls] = ni
                return nr, ni

            xr, xi = lax.fori_loop(0, tb, step, (st_re[rs, ls], st_im[rs, ls]))
            st_re[rs, ls] = xr
            st_im[rs, ls] = xi

    ys = []
    for k in range(S5_BLOCKS):
        ks = slice(k * half, (k + 1) * half)
        ys.append(_dot(bur[:, ks].astype(BF16), cc_ref[k, :half, :])
                  - _dot(bui[:, ks].astype(BF16), cc_ref[k, half:, :]))
    y = jnp.concatenate(ys, axis=1) + dskip_ref[...] * h
    y = 0.5 * y * (1.0 + jnp.tanh(math.sqrt(2.0 / math.pi) * (y + 0.044715 * (y * y * y))))
    gl = _dot(y.astype(BF16), glu_ref[...])
    out = gl[:, :D_MODEL] * _sigmoid(gl[:, D_MODEL:])
    o_ref[...] = (x + _gate_rows(out, g1_ref[...], bb)).reshape(tb, bb, D_MODEL)

    @pl.when(t_idx == pl.num_programs(1) - 1)
    def _():
        sre_ref[...] = st_re[...]
        sim_ref[...] = st_im[...]


def _s5_layer(x3, sh, sc, g1, bbd, ccd, a_re, a_im, s0re, s0im, dskip, glu_w, tb, bb):
    T, B, _ = x3.shape
    per_b = pl.BlockSpec((bb, D_MODEL), lambda i, t: (i, 0))
    st = pl.BlockSpec((bb, S5_LANES), lambda i, t: (i, 0))
    full = lambda shape: pl.BlockSpec(shape, lambda i, t: (0,) * len(shape))
    return pl.pallas_call(
        functools.partial(_s5_kernel, tb=tb, bb=bb),
        grid=(B // bb, T // tb),
        in_specs=[
            pl.BlockSpec((tb, bb, D_MODEL), lambda i, t: (t, i, 0)),
            per_b, per_b, per_b,
            full((S5_BLOCKS, LANES, 2 * S5_LANES // S5_BLOCKS)),
            full((S5_BLOCKS, 2 * S5_LANES // S5_BLOCKS, LANES)),
            full((SUBLANES, S5_LANES)),
            full((SUBLANES, S5_LANES)),
            st, st,
            full((1, D_MODEL)),
            full((D_MODEL, 2 * D_MODEL)),
        ],
        out_specs=[pl.BlockSpec((tb, bb, D_MODEL), lambda i, t: (t, i, 0)), st, st],
        out_shape=[
            jax.ShapeDtypeStruct((T, B, D_MODEL), F32),
            jax.ShapeDtypeStruct((B, S5_LANES), F32),
            jax.ShapeDtypeStruct((B, S5_LANES), F32),
        ],
        scratch_shapes=[
            pltpu.VMEM((tb * bb, S5_LANES), F32),
            pltpu.VMEM((tb * bb, S5_LANES), F32),
            pltpu.VMEM((bb, S5_LANES), F32),
            pltpu.VMEM((bb, S5_LANES), F32),
        ],
        compiler_params=_cparams(("arbitrary", "arbitrary")),
        name="s5_layer",
    )(x3, sh, sc, g1, bbd, ccd, a_re, a_im, s0re, s0im, dskip, glu_w)


def _block_diag(w):
    _, r, c = w.shape
    w = w.reshape(S5_BLOCKS, 8, r, c)
    eye = jnp.eye(8, dtype=w.dtype)
    return (w[:, :, :, None, :] * eye[None, :, None, :, None]).reshape(S5_BLOCKS, 8 * r, 8 * c)


def _prep_params(p):
    w = {}
    w["ssd_in_w"] = [p["ssd_in_w"][j, :, :ZX_DIM].astype(BF16) for j in range(2)]
    w["ssd_dt_w"] = [jnp.pad(p["ssd_in_w"][j, :, ZX_DIM:], ((0, 0), (0, LANES - N_HEADS))).astype(BF16)
                     for j in range(2)]
    pad_h = lambda v: jnp.pad(v, (0, LANES - N_HEADS)).reshape(1, LANES)
    w["dt_bias"] = [pad_h(p["ssd_dt_bias"][j]) for j in range(2)]
    w["a_log"] = [pad_h(p["ssd_A_log"][j]) for j in range(2)]
    w["d_exp"] = [jnp.repeat(p["ssd_D"][j], HEAD_DIM).reshape(1, D_INNER) for j in range(2)]
    w["norm_w"] = [p["ssd_norm_w"][j].reshape(1, D_INNER) for j in range(2)]
    w["conv_w"] = [p["ssd_conv_w"][j] for j in range(2)]
    w["conv_b"] = [p["ssd_conv_b"][j].reshape(1, CONV_DIM) for j in range(2)]
    w["ssd_out_w"] = [p["ssd_out_w"][j].astype(BF16) for j in range(2)]
    w["ffn_w_in"] = [p["ffn_w_in"][i].astype(BF16) for i in range(DEPTH)]
    w["ffn_w_out"] = [p["ffn_w_out"][i].astype(BF16) for i in range(DEPTH)]
    w["final_w"] = p["final_norm_w"].reshape(1, D_MODEL)
    w["s5"] = []
    for j in range(2):
        ab_re, ab_im, bb_re, bb_im = _s5_disc(
            p["s5_A_re"][j], p["s5_A_im"][j], p["s5_log_dt"][j],
            jnp.swapaxes(p["s5_B_re"][j], 1, 2), jnp.swapaxes(p["s5_B_im"][j], 1, 2))
        bbd = jnp.concatenate([_block_diag(bb_re), _block_diag(bb_im)], axis=2).astype(BF16)
        c_re = jnp.swapaxes(p["s5_C_re"][j], 1, 2)
        c_im = jnp.swapaxes(p["s5_C_im"][j], 1, 2)
        ccd = jnp.concatenate([_block_diag(c_re), _block_diag(c_im)], axis=1).astype(BF16)
        bro = lambda v: jnp.broadcast_to(v.reshape(1, S5_LANES), (SUBLANES, S5_LANES))
        w["s5"].append(dict(bbd=bbd, ccd=ccd, a_re=bro(ab_re), a_im=bro(ab_im),
                            dskip=p["s5_D"][j].reshape(1, D_MODEL),
                            glu_w=p["s5_glu_w"][j].astype(BF16)))
    return w


def _trunk(x3, mods, ssm0, conv0, re0, im0, w, tm, s5_tb, s5_bb):
    T, B, _ = x3.shape
    L = SSD_CHUNK if T % SSD_CHUNK == 0 else T
    ssm_new, conv_new, re_new, im_new = [], [], [], []
    for i in range(DEPTH):
        sh1, sc1, g1, sh2, sc2, g2 = [mods[i][:, k * D_MODEL:(k + 1) * D_MODEL] for k in range(6)]
        j = i // 2
        final = i == DEPTH - 1
        x2d = x3.reshape(T * B, D_MODEL)
        if i % 2 == 0:
            zx, dtraw = _inproj(x2d, sh1, sc1, w["ssd_in_w"][j], w["ssd_dt_w"][j], B, tm)
            y, s_new, cv_new = _ssd_core(
                zx.reshape(T, B * ZX_DIM), dtraw.reshape(T, B * LANES), conv0[j], ssm0[j],
                w["conv_w"][j], w["conv_b"][j], w["dt_bias"][j], w["a_log"][j], w["d_exp"][j],
                w["norm_w"][j], T, B, L)
            ssm_new.append(s_new)
            conv_new.append(cv_new)
            x2d = _ffn(x2d, y.reshape(T * B, D_INNER), w["ssd_out_w"][j], g1, sh2, sc2, g2,
                       w["ffn_w_in"][i], w["ffn_w_out"][i], w["final_w"], B, tm, final)
        else:
            s5 = w["s5"][j]
            x3, sr, si = _s5_layer(x3, sh1, sc1, g1, s5["bbd"], s5["ccd"], s5["a_re"], s5["a_im"],
                                   re0[j], im0[j], s5["dskip"], s5["glu_w"], s5_tb, s5_bb)
            re_new.append(sr)
            im_new.append(si)
            x2d = _ffn(x3.reshape(T * B, D_MODEL), None, None, None, sh2, sc2, g2,
                       w["ffn_w_in"][i], w["ffn_w_out"][i], w["final_w"], B, tm, final)
        x3 = x2d.reshape(T, B, D_MODEL)
    return x3, jnp.stack(ssm_new), jnp.stack(conv_new), jnp.stack(re_new), jnp.stack(im_new)


def kernel(x_prompt, x_sample, state_ssm, state_conv, state_s5_re, state_s5_im, c_prompt, c_sample,
           ada_w, ada_b, ssd_in_w, ssd_conv_w, ssd_conv_b, ssd_dt_bias, ssd_A_log, ssd_D, ssd_norm_w,
           ssd_out_w, s5_A_re, s5_A_im, s5_log_dt, s5_B_re, s5_B_im, s5_C_re, s5_C_im, s5_D, s5_glu_w,
           ffn_w_in, ffn_w_out, final_norm_w):
    p = dict(ssd_in_w=ssd_in_w, ssd_conv_w=ssd_conv_w, ssd_conv_b=ssd_conv_b, ssd_dt_bias=ssd_dt_bias,
             ssd_A_log=ssd_A_log, ssd_D=ssd_D, ssd_norm_w=ssd_norm_w, ssd_out_w=ssd_out_w,
             s5_A_re=s5_A_re, s5_A_im=s5_A_im, s5_log_dt=s5_log_dt, s5_B_re=s5_B_re, s5_B_im=s5_B_im,
             s5_C_re=s5_C_re, s5_C_im=s5_C_im, s5_D=s5_D, s5_glu_w=s5_glu_w,
             ffn_w_in=ffn_w_in, ffn_w_out=ffn_w_out, final_norm_w=final_norm_w)
    w = _prep_params(p)
    bp, tp, _ = x_prompt.shape
    bs, ts, _ = x_sample.shape

    mods = _ada(jnp.concatenate([c_prompt, c_sample], axis=0), ada_w, ada_b)
    mods_p = [mods[l, :bp] for l in range(DEPTH)]
    mods_s = [mods[l, bp:] for l in range(DEPTH)]

    n_ssd, n_s5 = state_ssm.shape[0], state_s5_re.shape[0]
    rows_hp = N_HEADS * HEAD_DIM
    zeros_p = (jnp.zeros((n_ssd, bp, rows_hp, D_STATE), state_ssm.dtype),
               jnp.zeros((n_ssd, bp, CONV_K - 1, CONV_DIM), state_conv.dtype),
               jnp.zeros((n_s5, bp, S5_LANES), state_s5_re.dtype),
               jnp.zeros((n_s5, bp, S5_LANES), state_s5_im.dtype))
    yp, ssm_p, conv_p, re_p, im_p = _trunk(
        jnp.swapaxes(x_prompt, 0, 1), mods_p, *zeros_p, w,
        tm=512, s5_tb=512 // bp, s5_bb=bp)
    ys, ssm_s, conv_s, re_s, im_s = _trunk(
        jnp.swapaxes(x_sample, 0, 1), mods_s,
        state_ssm.reshape(n_ssd, bs, rows_hp, D_STATE), state_conv,
        state_s5_re.reshape(n_s5, bs, S5_LANES), state_s5_im.reshape(n_s5, bs, S5_LANES), w,
        tm=512, s5_tb=ts, s5_bb=512 // ts)

    ssm_shape = lambda b: (n_ssd, b, N_HEADS, HEAD_DIM, D_STATE)
    s5_shape = lambda b: (n_s5, b, S5_GROUPS, S5_STATE)
    return (jnp.swapaxes(yp, 0, 1), jnp.swapaxes(ys, 0, 1),
            ssm_p.reshape(ssm_shape(bp)), conv_p, re_p.reshape(s5_shape(bp)), im_p.reshape(s5_shape(bp)),
            ssm_s.reshape(ssm_shape(bs)), conv_s, re_s.reshape(s5_shape(bs)), im_s.reshape(s5_shape(bs)))
```

```python
import functools
import math

import jax
import jax.numpy as jnp
from jax import lax
from jax.experimental import pallas as pl
from jax.experimental.pallas import tpu as pltpu

F32 = jnp.float32
BF16 = jnp.bfloat16

D_MODEL = 1024
DEPTH = 4
D_INNER = 2048
HEAD_DIM = 64
N_HEADS = 32
N_GROUPS = 8
HEADS_PER_GROUP = 4
D_STATE = 128
CONV_K = 4
BC_DIM = 2 * N_GROUPS * D_STATE
CONV_DIM = D_INNER + BC_DIM
ZX_DIM = D_INNER + CONV_DIM
SSD_CHUNK = 128
S5_GROUPS = 64
S5_GROUP = 16
S5_STATE = 64
S5_LANES = S5_GROUPS * S5_STATE
S5_BLOCKS = D_MODEL // 128
D_FF = 2816
RMS_EPS = 1e-6
LANES = 128
SUBLANES = 8
CONV_PAD = 8
VMEM_LIMIT = 52 * 1024 * 1024


def _cparams(sem):
    return pltpu.CompilerParams(dimension_semantics=sem, vmem_limit_bytes=VMEM_LIMIT)


def _sigmoid(x):
    return 1.0 / (1.0 + jnp.exp(-x))


def _silu(x):
    return x * _sigmoid(x)


def _rows_by_batch(x, nb):
    tm, d = x.shape
    return x.reshape(tm // nb, nb, d)


def _modulate_rows(x, sh, sc, nb):
    if nb == 1:
        r = lax.rsqrt(jnp.mean(x * x, axis=-1, keepdims=True) + RMS_EPS)
        return (x * r) * (1.0 + sc) + sh
    x3 = _rows_by_batch(x, nb)
    r = lax.rsqrt(jnp.mean(x3 * x3, axis=-1, keepdims=True) + RMS_EPS)
    h = (x3 * r) * (1.0 + sc[None]) + sh[None]
    return h.reshape(x.shape)


def _gate_rows(v, g, nb):
    if nb == 1:
        return v * g
    return (_rows_by_batch(v, nb) * g[None]).reshape(v.shape)


def _dot(a, b):
    return jnp.dot(a, b, preferred_element_type=F32)


def _dot_nt(a, b):
    return lax.dot_general(a, b, (((1,), (1,)), ((), ())), preferred_element_type=F32)


def _dot_tn(a, b):
    return lax.dot_general(a, b, (((0,), (0,)), ((), ())), preferred_element_type=F32)


def _ada_kernel(c_ref, w_ref, b_ref, o_ref):
    s = _silu(c_ref[...]).astype(BF16)
    o_ref[0] = _dot(s, w_ref[0].astype(BF16)) + b_ref[0]


def _ada(c_all, ada_w, ada_b):
    nrow = c_all.shape[0]
    tn = 1536
    return pl.pallas_call(
        _ada_kernel,
        grid=(DEPTH, 6 * D_MODEL // tn),
        in_specs=[
            pl.BlockSpec((nrow, D_MODEL), lambda l, j: (0, 0)),
            pl.BlockSpec((1, D_MODEL, tn), lambda l, j: (l, 0, j)),
            pl.BlockSpec((1, 1, tn), lambda l, j: (l, 0, j)),
        ],
        out_specs=pl.BlockSpec((1, nrow, tn), lambda l, j: (l, 0, j)),
        out_shape=jax.ShapeDtypeStruct((DEPTH, nrow, 6 * D_MODEL), F32),
        compiler_params=_cparams(("arbitrary", "arbitrary")),
        name="ada",
    )(c_all, ada_w, ada_b.reshape(DEPTH, 1, 6 * D_MODEL))


def _inproj_kernel(x_ref, sh_ref, sc_ref, w_ref, wdt_ref, zx_ref, dt_ref, h_scr, *, nb):
    j = pl.program_id(2)

    @pl.when(j == 0)
    def _():
        h = _modulate_rows(x_ref[...], sh_ref[...], sc_ref[...], nb).astype(BF16)
        h_scr[...] = h
        dt_ref[...] = _dot(h, wdt_ref[...])

    zx_ref[...] = _dot(h_scr[...], w_ref[...])


def _inproj(x3, sh, sc, w, wdt, tm):
    ng, rows, _ = x3.shape
    nb = sh.shape[1]
    tn = 2048
    per_g = pl.BlockSpec((None, nb, D_MODEL), lambda g, i, j: (g, 0, 0))
    return pl.pallas_call(
        functools.partial(_inproj_kernel, nb=nb),
        grid=(ng, rows // tm, ZX_DIM // tn),
        in_specs=[
            pl.BlockSpec((None, tm, D_MODEL), lambda g, i, j: (g, i, 0)),
            per_g, per_g,
            pl.BlockSpec((D_MODEL, tn), lambda g, i, j: (0, j)),
            pl.BlockSpec((D_MODEL, LANES), lambda g, i, j: (0, 0)),
        ],
        out_specs=[
            pl.BlockSpec((None, tm, tn), lambda g, i, j: (g, i, j)),
            pl.BlockSpec((None, tm, LANES), lambda g, i, j: (g, i, 0)),
        ],
        out_shape=[
            jax.ShapeDtypeStruct((ng, rows, ZX_DIM), F32),
            jax.ShapeDtypeStruct((ng, rows, LANES), F32),
        ],
        scratch_shapes=[pltpu.VMEM((tm, D_MODEL), BF16)],
        compiler_params=_cparams(("arbitrary", "arbitrary", "arbitrary")),
        name="ssd_inproj",
    )(x3, sh, sc, w, wdt)


def _cumsum_rows(a, L):
    rows = lax.broadcasted_iota(jnp.int32, a.shape, 0)
    k = 1
    while k < L:
        a = a + jnp.where(rows >= k, pltpu.roll(a, k, 0), 0.0)
        k *= 2
    return a


def _transpose_rows(a, L):
    if L < LANES:
        a = jnp.concatenate([a, jnp.zeros((LANES - L, LANES), F32)], axis=0)
    return a.T[:, :L]


def _ssd_kernel(z_ref, xr_ref, bcr_ref, dt_ref, conv0_ref, ssm0_ref,
                cw_ref, cb_ref, dtb_ref, alog_ref, dexp_ref, nw_ref,
                y_ref, ssm_out_ref, conv_out_ref,
                xpad, bcpad, state, *, L):
    c = pl.program_id(1)
    hist = CONV_K - 1
    lo = CONV_PAD - hist

    @pl.when(c == 0)
    def _():
        state[...] = ssm0_ref[0]
        xpad[lo:CONV_PAD, :] = conv0_ref[0, :, :D_INNER]
        bcpad[lo:CONV_PAD, :] = conv0_ref[0, :, D_INNER:]

    xpad[CONV_PAD:CONV_PAD + L, :] = xr_ref[...]
    bcpad[CONV_PAD:CONV_PAD + L, :] = bcr_ref[...]

    def conv(pad, off):
        acc = cb_ref[:, off:off + D_INNER]
        for k in range(CONV_K):
            acc = acc + cw_ref[k:k + 1, off:off + D_INNER] * pad[lo + k:lo + k + L, :]
        return _silu(acc)

    xs = conv(xpad, 0)
    bc = conv(bcpad, D_INNER)

    tail_x = xpad[lo + L:CONV_PAD + L, :]
    tail_bc = bcpad[lo + L:CONV_PAD + L, :]
    xpad[lo:CONV_PAD, :] = tail_x
    bcpad[lo:CONV_PAD, :] = tail_bc
    conv_out_ref[0, :, :D_INNER] = tail_x
    conv_out_ref[0, :, D_INNER:] = tail_bc

    dtr = dt_ref[...] + dtb_ref[...]
    dt = jnp.maximum(dtr, 0.0) + jnp.log1p(jnp.exp(-jnp.abs(dtr)))
    a = dt * (-jnp.exp(alog_ref[...]))
    acs = _cumsum_rows(a, L)
    acs_t = _transpose_rows(acs, L)
    dt_t = _transpose_rows(dt, L)
    a_last = acs[L - 1:L, :]
    e_acs = jnp.exp(acs)
    w_end = jnp.exp(a_last - acs) * dt
    e_last = jnp.exp(a_last)

    causal = (lax.broadcasted_iota(jnp.int32, (L, L), 0)
              >= lax.broadcasted_iota(jnp.int32, (L, L), 1))
    z = z_ref[...]

    for g in range(N_GROUPS):
        bg = bc[:, g * D_STATE:(g + 1) * D_STATE].astype(BF16)
        cg = bc[:, (N_GROUPS + g) * D_STATE:(N_GROUPS + g + 1) * D_STATE].astype(BF16)
        cb = _dot_nt(cg, bg)
        r0 = g * HEADS_PER_GROUP * HEAD_DIM
        sg = state[r0:r0 + HEADS_PER_GROUP * HEAD_DIM, :]
        y_off = _dot_nt(cg, sg.astype(BF16))
        ys, xws = [], []
        for r in range(HEADS_PER_GROUP):
            h = g * HEADS_PER_GROUP + r
            col = acs[:, h:h + 1]
            seg = col - acs_t[h:h + 1, :]
            decay = jnp.exp(jnp.where(causal, seg, -jnp.inf))
            m = (cb * decay * dt_t[h:h + 1, :]).astype(BF16)
            xh = xs[:, h * HEAD_DIM:(h + 1) * HEAD_DIM]
            yh = (_dot(m, xh.astype(BF16))
                  + e_acs[:, h:h + 1] * y_off[:, r * HEAD_DIM:(r + 1) * HEAD_DIM]
                  + dexp_ref[:, h * HEAD_DIM:(h + 1) * HEAD_DIM] * xh)
            ys.append(yh)
            xws.append(xh * w_end[:, h:h + 1])
        upd = _dot_tn(jnp.concatenate(xws, axis=1).astype(BF16), bg)
        for r in range(HEADS_PER_GROUP):
            h = g * HEADS_PER_GROUP + r
            rows = slice(r0 + r * HEAD_DIM, r0 + (r + 1) * HEAD_DIM)
            state[rows, :] = (e_last[:, h:h + 1] * sg[r * HEAD_DIM:(r + 1) * HEAD_DIM, :]
                              + upd[r * HEAD_DIM:(r + 1) * HEAD_DIM, :])
        gw = HEADS_PER_GROUP * HEAD_DIM
        yg = jnp.concatenate(ys, axis=1) * _silu(z[:, r0:r0 + gw])
        yg = yg * lax.rsqrt(jnp.mean(yg * yg, axis=-1, keepdims=True) + RMS_EPS)
        y_ref[:, r0:r0 + gw] = (yg * nw_ref[:, r0:r0 + gw]).astype(y_ref.dtype)

    @pl.when(c == pl.num_programs(1) - 1)
    def _():
        ssm_out_ref[0] = state[...]


N_XTILES = D_INNER // LANES
GROUP_W = HEADS_PER_GROUP * HEAD_DIM


def _split_bf16(v):
    hi = v.astype(BF16).astype(F32)
    return hi, (v - hi).astype(BF16).astype(F32)


def _ssd_chunk_kernel(z_ref, xr_ref, bcr_ref, dt_ref, conv0_ref, ssm0_ref,
                      cw_ref, cb_ref, dtb_ref, alog_ref, dexp_ref, nw_ref, expand_ref,
                      y_ref, ssm_out_ref, conv_out_ref,
                      pad, state_t, *, L):
    c = pl.program_id(1)
    lo = CONV_PAD - (CONV_K - 1)

    @pl.when(c == 0)
    def _():
        state_t[...] = ssm0_ref[0].T
        for lt in range(2 * N_XTILES):
            pad[lt, lo:CONV_PAD, :] = conv0_ref[0, :, lt * LANES:(lt + 1) * LANES]

    for lt in range(N_XTILES):
        pad[lt, CONV_PAD:CONV_PAD + L, :] = xr_ref[:, lt * LANES:(lt + 1) * LANES]
        pad[N_XTILES + lt, CONV_PAD:CONV_PAD + L, :] = bcr_ref[:, lt * LANES:(lt + 1) * LANES]

    def conv_tile(lt):
        cols = slice(lt * LANES, (lt + 1) * LANES)
        acc = cb_ref[:, cols]
        for k in range(CONV_K):
            acc = acc + cw_ref[k:k + 1, cols] * pad[lt, lo + k:lo + k + L, :]
        return _silu(acc)

    tiles = [conv_tile(lt) for lt in range(2 * N_XTILES)]

    for lt in range(2 * N_XTILES):
        tail = pad[lt, lo + L:CONV_PAD + L, :]
        pad[lt, lo:CONV_PAD, :] = tail
        conv_out_ref[0, :, lt * LANES:(lt + 1) * LANES] = tail

    dtr = dt_ref[...] + dtb_ref[...]
    dt = jnp.maximum(dtr, 0.0) + jnp.log1p(jnp.exp(-jnp.abs(dtr)))
    a = dt * (-jnp.exp(alog_ref[...]))
    acs = _cumsum_rows(a, L)
    acs_t = acs.T
    dt_t = dt.T
    a_last = acs[L - 1:L, :]
    ea_hi, ea_lo = _split_bf16(jnp.exp(acs))
    we_hi, we_lo = _split_bf16(jnp.exp(a_last - acs) * dt)
    parts = jnp.concatenate([jnp.concatenate([ea_hi, ea_lo], axis=1),
                             jnp.concatenate([we_hi, we_lo], axis=1)], axis=0).astype(BF16)
    wide = _dot(parts, expand_ref[...])
    ea_x = wide[:L]
    we_x = wide[L:]

    causal = (lax.broadcasted_iota(jnp.int32, (L, L), 0)
              >= lax.broadcasted_iota(jnp.int32, (L, L), 1))
    lane_head = lax.broadcasted_iota(jnp.int32, (L, GROUP_W), 1) // HEAD_DIM

    for g in range(N_GROUPS):
        gl = slice(g * GROUP_W, (g + 1) * GROUP_W)
        bg = tiles[N_XTILES + g].astype(BF16)
        cg = tiles[N_XTILES + N_GROUPS + g].astype(BF16)
        cb = _dot_nt(cg, bg)
        sg = state_t[:, gl]
        y_off = _dot(cg, sg.astype(BF16))
        ms = []
        for r in range(HEADS_PER_GROUP):
            h = g * HEADS_PER_GROUP + r
            seg = acs[:, h:h + 1] - acs_t[h:h + 1, :]
            decay = jnp.exp(jnp.where(causal, seg, -jnp.inf))
            ms.append((cb * decay * dt_t[h:h + 1, :]).astype(BF16))
        xg = jnp.concatenate([tiles[2 * g], tiles[2 * g + 1]], axis=1)
        xgb = xg.astype(BF16)
        xblk = jnp.concatenate([jnp.where(lane_head == r, xgb, jnp.zeros_like(xgb))
                                for r in range(HEADS_PER_GROUP)], axis=0)
        y_diag = _dot(jnp.concatenate(ms, axis=1), xblk)
        yg = y_diag + ea_x[:, gl] * y_off + dexp_ref[:, gl] * xg
        upd = _dot_tn(bg, (xg * we_x[:, gl]).astype(BF16))
        state_t[:, gl] = ea_x[L - 1:L, gl] * sg + upd
        yg = yg * _silu(z_ref[:, gl])
        yg = yg * lax.rsqrt(jnp.mean(yg * yg, axis=-1, keepdims=True) + RMS_EPS)
        y_ref[:, gl] = (yg * nw_ref[:, gl]).astype(y_ref.dtype)

    @pl.when(c == pl.num_programs(1) - 1)
    def _():
        ssm_out_ref[0] = state_t[...].T


def _ssd_core(zx, dtraw, conv0, ssm0, cw, cb, dtb, alog, dexp, nw, T, B, L, batch_major):
    nc = T // L
    nblk = ZX_DIM // D_INNER
    full = lambda shape: pl.BlockSpec(shape, lambda b, c: (0,) * len(shape))
    if batch_major:
        col = lambda w, k: pl.BlockSpec((None, L, w), lambda b, c: (b, c, k))
        dt_spec = col(LANES, 0)
        y_spec = col(D_INNER, 0)
        y_shape = (B, T, D_INNER)
    else:
        col = lambda w, k: pl.BlockSpec((None, L, w), lambda b, c: (0, c, nblk * b + k))
        dt_spec = pl.BlockSpec((None, L, LANES), lambda b, c: (0, c, b))
        y_spec = pl.BlockSpec((None, L, D_INNER), lambda b, c: (0, c, b))
        y_shape = (1, T, B * D_INNER)
    in_specs = [
        col(D_INNER, 0),
        col(D_INNER, 1),
        col(D_INNER, 2),
        dt_spec,
        pl.BlockSpec((1, CONV_K - 1, CONV_DIM), lambda b, c: (b, 0, 0)),
        pl.BlockSpec((1, N_HEADS * HEAD_DIM, D_STATE), lambda b, c: (b, 0, 0)),
        full((CONV_K, CONV_DIM)),
        full((1, CONV_DIM)),
        full((1, LANES)),
        full((1, LANES)),
        full((1, D_INNER)),
        full((1, D_INNER)),
    ]
    args = [zx, zx, zx, dtraw, conv0, ssm0, cw, cb, dtb, alog, dexp, nw]
    if L == SSD_CHUNK:
        body = _ssd_chunk_kernel
        sel = (jnp.arange(D_INNER)[None, :] // HEAD_DIM == jnp.arange(LANES)[:, None]).astype(BF16)
        in_specs.append(full((2 * LANES, D_INNER)))
        args.append(jnp.concatenate([sel, sel], axis=0))
        scratch = [
            pltpu.VMEM((2 * N_XTILES, CONV_PAD + L, LANES), F32),
            pltpu.VMEM((D_STATE, N_HEADS * HEAD_DIM), F32),
        ]
    else:
        body = _ssd_kernel
        scratch = [
            pltpu.VMEM((CONV_PAD + L, D_INNER), F32),
            pltpu.VMEM((CONV_PAD + L, D_INNER), F32),
            pltpu.VMEM((N_HEADS * HEAD_DIM, D_STATE), F32),
        ]
    return pl.pallas_call(
        functools.partial(body, L=L),
        grid=(B, nc),
        in_specs=in_specs,
        out_specs=[
            y_spec,
            pl.BlockSpec((1, N_HEADS * HEAD_DIM, D_STATE), lambda b, c: (b, 0, 0)),
            pl.BlockSpec((1, CONV_K - 1, CONV_DIM), lambda b, c: (b, 0, 0)),
        ],
        out_shape=[
            jax.ShapeDtypeStruct(y_shape, BF16),
            jax.ShapeDtypeStruct((B, N_HEADS * HEAD_DIM, D_STATE), F32),
            jax.ShapeDtypeStruct((B, CONV_K - 1, CONV_DIM), F32),
        ],
        scratch_shapes=scratch,
        compiler_params=_cparams(("arbitrary", "arbitrary")),
        name="ssd_core",
    )(*args)


def _ffn_kernel(*refs, nb, has_proj, final):
    if has_proj:
        (x_ref, y_ref, ow_ref, g1_ref, sh_ref, sc_ref, g2_ref, wg_ref, wu_ref, wo_ref, fw_ref,
         o_ref, x1_scr, h_scr, acc_scr) = refs
    else:
        (x_ref, sh_ref, sc_ref, g2_ref, wg_ref, wu_ref, wo_ref, fw_ref,
         o_ref, x1_scr, h_scr, acc_scr) = refs
    j = pl.program_id(2)

    @pl.when(j == 0)
    def _():
        x = x_ref[...]
        if has_proj:
            x = x + _gate_rows(_dot(y_ref[...], ow_ref[...]), g1_ref[...], nb)
        x1_scr[...] = x
        h_scr[...] = _modulate_rows(x, sh_ref[...], sc_ref[...], nb).astype(BF16)
        acc_scr[...] = jnp.zeros_like(acc_scr)

    h = h_scr[...]
    act = (_silu(_dot(h, wg_ref[...])) * _dot(h, wu_ref[...])).astype(BF16)
    acc_scr[...] += _dot(act, wo_ref[...])

    @pl.when(j == pl.num_programs(2) - 1)
    def _():
        x2 = x1_scr[...] + _gate_rows(acc_scr[...], g2_ref[...], nb)
        if final:
            x2 = x2 * lax.rsqrt(jnp.mean(x2 * x2, axis=-1, keepdims=True) + RMS_EPS) * fw_ref[...]
        o_ref[...] = x2


def _ffn(x3, y3, out_w, g1, sh, sc, g2, w_in, w_out, fw, tm, final):
    ng, rows, _ = x3.shape
    nb = sh.shape[1]
    tf = D_FF // 2
    nj = D_FF // tf
    has_proj = y3 is not None
    row_blk = lambda w: pl.BlockSpec((None, tm, w), lambda g, i, j: (g, i, 0))
    per_g = pl.BlockSpec((None, nb, D_MODEL), lambda g, i, j: (g, 0, 0))
    in_specs = [row_blk(D_MODEL)]
    args = [x3]
    if has_proj:
        in_specs += [row_blk(D_INNER), pl.BlockSpec((D_INNER, D_MODEL), lambda g, i, j: (0, 0)), per_g]
        args += [y3, out_w, g1]
    in_specs += [
        per_g, per_g, per_g,
        pl.BlockSpec((D_MODEL, tf), lambda g, i, j: (0, j)),
        pl.BlockSpec((D_MODEL, tf), lambda g, i, j: (0, nj + j)),
        pl.BlockSpec((tf, D_MODEL), lambda g, i, j: (j, 0)),
        pl.BlockSpec((1, D_MODEL), lambda g, i, j: (0, 0)),
    ]
    args += [sh, sc, g2, w_in, w_in, w_out, fw]
    return pl.pallas_call(
        functools.partial(_ffn_kernel, nb=nb, has_proj=has_proj, final=final),
        grid=(ng, rows // tm, nj),
        in_specs=in_specs,
        out_specs=row_blk(D_MODEL),
        out_shape=jax.ShapeDtypeStruct((ng, rows, D_MODEL), F32),
        scratch_shapes=[
            pltpu.VMEM((tm, D_MODEL), F32),
            pltpu.VMEM((tm, D_MODEL), BF16),
            pltpu.VMEM((tm, D_MODEL), F32),
        ],
        compiler_params=_cparams(("arbitrary", "arbitrary", "arbitrary")),
        name="proj_ffn" if has_proj else "ffn",
    )(*args)


def _s5_disc_kernel(are_ref, aim_ref, ldt_ref, bre_ref, bim_ref, abre_ref, abim_ref, bbre_ref, bbim_ref):
    lre, lim = are_ref[...], aim_ref[...]
    dt = jnp.exp(ldt_ref[...])
    mag = jnp.exp(lre * dt)
    ab_re, ab_im = mag * jnp.cos(lim * dt), mag * jnp.sin(lim * dt)
    den = lre * lre + lim * lim
    nr, ni = ab_re - 1.0, ab_im
    q_re = (nr * lre + ni * lim) / den
    q_im = (ni * lre - nr * lim) / den
    abre_ref[...] = ab_re
    abim_ref[...] = ab_im
    br, bi = bre_ref[...], bim_ref[...]
    bbre_ref[...] = q_re[:, None, :] * br - q_im[:, None, :] * bi
    bbim_ref[...] = q_re[:, None, :] * bi + q_im[:, None, :] * br


def _s5_disc(a_re, a_im, log_dt, b_re, b_im):
    gn = jax.ShapeDtypeStruct((S5_GROUPS, S5_STATE), F32)
    gin = jax.ShapeDtypeStruct((S5_GROUPS, S5_GROUP, S5_STATE), F32)
    return pl.pallas_call(
        _s5_disc_kernel,
        out_shape=[gn, gn, gin, gin],
        name="s5_disc",
    )(a_re, a_im, log_dt.reshape(S5_GROUPS, 1), b_re, b_im)


S5_SCAN_LANES = 512


def _s5_kernel(x_ref, sh_ref, sc_ref, g1_ref, bb_ref, cc_ref, are_ref, aim_ref, s0re_ref, s0im_ref,
               dskip_ref, glu_ref, o_ref, sre_ref, sim_ref, bur, bui, st_re, st_im, *, tb, bb):
    t_idx = pl.program_id(1)
    rows = tb * bb

    @pl.when(t_idx == 0)
    def _():
        st_re[...] = s0re_ref[...]
        st_im[...] = s0im_ref[...]

    x = x_ref[...].reshape(rows, D_MODEL)
    h = _modulate_rows(x, sh_ref[...], sc_ref[...], bb)
    hb = h.astype(BF16)
    half = S5_LANES // S5_BLOCKS
    for k in range(S5_BLOCKS):
        bu = _dot(hb[:, k * LANES:(k + 1) * LANES], bb_ref[k])
        bur[:, k * half:(k + 1) * half] = bu[:, :half]
        bui[:, k * half:(k + 1) * half] = bu[:, half:]

    for rg in range(bb // SUBLANES):
        for lb in range(S5_LANES // S5_SCAN_LANES):
            ls = slice(lb * S5_SCAN_LANES, (lb + 1) * S5_SCAN_LANES)
            rs = slice(rg * SUBLANES, (rg + 1) * SUBLANES)
            ar, ai = are_ref[:, ls], aim_ref[:, ls]

            def step(t, carry):
                xr, xi = carry
                r0 = pl.multiple_of(t * bb + rg * SUBLANES, SUBLANES)
                nr = ar * xr - ai * xi + bur[pl.ds(r0, SUBLANES), ls]
                ni = ar * xi + ai * xr + bui[pl.ds(r0, SUBLANES), ls]
                bur[pl.ds(r0, SUBLANES), ls] = nr
                bui[pl.ds(r0, SUBLANES), ls] = ni
                return nr, ni

            xr, xi = lax.fori_loop(0, tb, step, (st_re[rs, ls], st_im[rs, ls]))
            st_re[rs, ls] = xr
            st_im[rs, ls] = xi

    ys = []
    for k in range(S5_BLOCKS):
        ks = slice(k * half, (k + 1) * half)
        ys.append(_dot(bur[:, ks].astype(BF16), cc_ref[k, :half, :])
                  - _dot(bui[:, ks].astype(BF16), cc_ref[k, half:, :]))
    y = jnp.concatenate(ys, axis=1) + dskip_ref[...] * h
    y = 0.5 * y * (1.0 + jnp.tanh(math.sqrt(2.0 / math.pi) * (y + 0.044715 * (y * y * y))))
    gl = _dot(y.astype(BF16), glu_ref[...])
    out = gl[:, :D_MODEL] * _sigmoid(gl[:, D_MODEL:])
    o_ref[...] = (x + _gate_rows(out, g1_ref[...], bb)).reshape(tb, bb, D_MODEL)

    @pl.when(t_idx == pl.num_programs(1) - 1)
    def _():
        sre_ref[...] = st_re[...]
        sim_ref[...] = st_im[...]


def _s5_layer(x3, sh, sc, g1, bbd, ccd, a_re, a_im, s0re, s0im, dskip, glu_w, tb, bb):
    T, B, _ = x3.shape
    per_b = pl.BlockSpec((bb, D_MODEL), lambda i, t: (i, 0))
    st = pl.BlockSpec((bb, S5_LANES), lambda i, t: (i, 0))
    full = lambda shape: pl.BlockSpec(shape, lambda i, t: (0,) * len(shape))
    return pl.pallas_call(
        functools.partial(_s5_kernel, tb=tb, bb=bb),
        grid=(B // bb, T // tb),
        in_specs=[
            pl.BlockSpec((tb, bb, D_MODEL), lambda i, t: (t, i, 0)),
            per_b, per_b, per_b,
            full((S5_BLOCKS, LANES, 2 * S5_LANES // S5_BLOCKS)),
            full((S5_BLOCKS, 2 * S5_LANES // S5_BLOCKS, LANES)),
            full((SUBLANES, S5_LANES)),
            full((SUBLANES, S5_LANES)),
            st, st,
            full((1, D_MODEL)),
            full((D_MODEL, 2 * D_MODEL)),
        ],
        out_specs=[pl.BlockSpec((tb, bb, D_MODEL), lambda i, t: (t, i, 0)), st, st],
        out_shape=[
            jax.ShapeDtypeStruct((T, B, D_MODEL), F32),
            jax.ShapeDtypeStruct((B, S5_LANES), F32),
            jax.ShapeDtypeStruct((B, S5_LANES), F32),
        ],
        scratch_shapes=[
            pltpu.VMEM((tb * bb, S5_LANES), F32),
            pltpu.VMEM((tb * bb, S5_LANES), F32),
            pltpu.VMEM((bb, S5_LANES), F32),
            pltpu.VMEM((bb, S5_LANES), F32),
        ],
        compiler_params=_cparams(("arbitrary", "arbitrary")),
        name="s5_layer",
    )(x3, sh, sc, g1, bbd, ccd, a_re, a_im, s0re, s0im, dskip, glu_w)


def _block_diag(w):
    _, r, c = w.shape
    w = w.reshape(S5_BLOCKS, 8, r, c)
    eye = jnp.eye(8, dtype=w.dtype)
    return (w[:, :, :, None, :] * eye[None, :, None, :, None]).reshape(S5_BLOCKS, 8 * r, 8 * c)


def _prep_params(p):
    w = {}
    w["ssd_in_w"] = [p["ssd_in_w"][j, :, :ZX_DIM].astype(BF16) for j in range(2)]
    w["ssd_dt_w"] = [jnp.pad(p["ssd_in_w"][j, :, ZX_DIM:], ((0, 0), (0, LANES - N_HEADS))).astype(BF16)
                     for j in range(2)]
    pad_h = lambda v: jnp.pad(v, (0, LANES - N_HEADS)).reshape(1, LANES)
    w["dt_bias"] = [pad_h(p["ssd_dt_bias"][j]) for j in range(2)]
    w["a_log"] = [pad_h(p["ssd_A_log"][j]) for j in range(2)]
    w["d_exp"] = [jnp.repeat(p["ssd_D"][j], HEAD_DIM).reshape(1, D_INNER) for j in range(2)]
    w["norm_w"] = [p["ssd_norm_w"][j].reshape(1, D_INNER) for j in range(2)]
    w["conv_w"] = [p["ssd_conv_w"][j] for j in range(2)]
    w["conv_b"] = [p["ssd_conv_b"][j].reshape(1, CONV_DIM) for j in range(2)]
    w["ssd_out_w"] = [p["ssd_out_w"][j].astype(BF16) for j in range(2)]
    w["ffn_w_in"] = [p["ffn_w_in"][i].astype(BF16) for i in range(DEPTH)]
    w["ffn_w_out"] = [p["ffn_w_out"][i].astype(BF16) for i in range(DEPTH)]
    w["final_w"] = p["final_norm_w"].reshape(1, D_MODEL)
    w["s5"] = []
    for j in range(2):
        ab_re, ab_im, bb_re, bb_im = _s5_disc(
            p["s5_A_re"][j], p["s5_A_im"][j], p["s5_log_dt"][j],
            jnp.swapaxes(p["s5_B_re"][j], 1, 2), jnp.swapaxes(p["s5_B_im"][j], 1, 2))
        bbd = jnp.concatenate([_block_diag(bb_re), _block_diag(bb_im)], axis=2).astype(BF16)
        c_re = jnp.swapaxes(p["s5_C_re"][j], 1, 2)
        c_im = jnp.swapaxes(p["s5_C_im"][j], 1, 2)
        ccd = jnp.concatenate([_block_diag(c_re), _block_diag(c_im)], axis=1).astype(BF16)
        bro = lambda v: jnp.broadcast_to(v.reshape(1, S5_LANES), (SUBLANES, S5_LANES))
        w["s5"].append(dict(bbd=bbd, ccd=ccd, a_re=bro(ab_re), a_im=bro(ab_im),
                            dskip=p["s5_D"][j].reshape(1, D_MODEL),
                            glu_w=p["s5_glu_w"][j].astype(BF16)))
    return w


def _trunk(x, mods, ssm0, conv0, re0, im0, w, tm, s5_tb, s5_bb, ssd_batch_major):
    B, T, _ = x.shape
    L = SSD_CHUNK if T % SSD_CHUNK == 0 else T
    ssm_new, conv_new, re_new, im_new = [], [], [], []
    time_major = False
    for i in range(DEPTH):
        j = i // 2
        final = i == DEPTH - 1
        is_ssd = i % 2 == 0
        want_tm = not (is_ssd and ssd_batch_major)
        if want_tm != time_major:
            x = jnp.swapaxes(x, 0, 1)
            time_major = want_tm
        parts = [mods[i][:, k * D_MODEL:(k + 1) * D_MODEL] for k in range(6)]
        if time_major:
            sh1, sc1, g1, sh2, sc2, g2 = [v[None] for v in parts]
            x3 = x.reshape(1, T * B, D_MODEL)
        else:
            sh1, sc1, g1, sh2, sc2, g2 = [v[:, None] for v in parts]
            x3 = x
        if is_ssd:
            zx, dtraw = _inproj(x3, sh1, sc1, w["ssd_in_w"][j], w["ssd_dt_w"][j], tm)
            if time_major:
                zx, dtraw = zx.reshape(1, T, B * ZX_DIM), dtraw.reshape(1, T, B * LANES)
            y, s_new, cv_new = _ssd_core(
                zx, dtraw, conv0[j], ssm0[j],
                w["conv_w"][j], w["conv_b"][j], w["dt_bias"][j], w["a_log"][j], w["d_exp"][j],
                w["norm_w"][j], T, B, L, not time_major)
            ssm_new.append(s_new)
            conv_new.append(cv_new)
            if time_major:
                y = y.reshape(1, T * B, D_INNER)
            x3 = _ffn(x3, y, w["ssd_out_w"][j], g1, sh2, sc2, g2,
                      w["ffn_w_in"][i], w["ffn_w_out"][i], w["final_w"], tm, final)
        else:
            s5 = w["s5"][j]
            xs5, sr, si = _s5_layer(x, sh1[0], sc1[0], g1[0], s5["bbd"], s5["ccd"], s5["a_re"], s5["a_im"],
                                    re0[j], im0[j], s5["dskip"], s5["glu_w"], s5_tb, s5_bb)
            re_new.append(sr)
            im_new.append(si)
            x3 = _ffn(xs5.reshape(1, T * B, D_MODEL), None, None, None, sh2, sc2, g2,
                      w["ffn_w_in"][i], w["ffn_w_out"][i], w["final_w"], tm, final)
        x = x3.reshape(T, B, D_MODEL) if time_major else x3
    if time_major:
        x = jnp.swapaxes(x, 0, 1)
    return x, jnp.stack(ssm_new), jnp.stack(conv_new), jnp.stack(re_new), jnp.stack(im_new)


def kernel(x_prompt, x_sample, state_ssm, state_conv, state_s5_re, state_s5_im, c_prompt, c_sample,
           ada_w, ada_b, ssd_in_w, ssd_conv_w, ssd_conv_b, ssd_dt_bias, ssd_A_log, ssd_D, ssd_norm_w,
           ssd_out_w, s5_A_re, s5_A_im, s5_log_dt, s5_B_re, s5_B_im, s5_C_re, s5_C_im, s5_D, s5_glu_w,
           ffn_w_in, ffn_w_out, final_norm_w):
    p = dict(ssd_in_w=ssd_in_w, ssd_conv_w=ssd_conv_w, ssd_conv_b=ssd_conv_b, ssd_dt_bias=ssd_dt_bias,
             ssd_A_log=ssd_A_log, ssd_D=ssd_D, ssd_norm_w=ssd_norm_w, ssd_out_w=ssd_out_w,
             s5_A_re=s5_A_re, s5_A_im=s5_A_im, s5_log_dt=s5_log_dt, s5_B_re=s5_B_re, s5_B_im=s5_B_im,
             s5_C_re=s5_C_re, s5_C_im=s5_C_im, s5_D=s5_D, s5_glu_w=s5_glu_w,
             ffn_w_in=ffn_w_in, ffn_w_out=ffn_w_out, final_norm_w=final_norm_w)
    w = _prep_params(p)
    bp, tp, _ = x_prompt.shape
    bs, ts, _ = x_sample.shape

    mods = _ada(jnp.concatenate([c_prompt, c_sample], axis=0), ada_w, ada_b)
    mods_p = [mods[l, :bp] for l in range(DEPTH)]
    mods_s = [mods[l, bp:] for l in range(DEPTH)]

    n_ssd, n_s5 = state_ssm.shape[0], state_s5_re.shape[0]
    rows_hp = N_HEADS * HEAD_DIM
    zeros_p = (jnp.zeros((n_ssd, bp, rows_hp, D_STATE), state_ssm.dtype),
               jnp.zeros((n_ssd, bp, CONV_K - 1, CONV_DIM), state_conv.dtype),
               jnp.zeros((n_s5, bp, S5_LANES), state_s5_re.dtype),
               jnp.zeros((n_s5, bp, S5_LANES), state_s5_im.dtype))
    yp, ssm_p, conv_p, re_p, im_p = _trunk(
        x_prompt, mods_p, *zeros_p, w,
        tm=512, s5_tb=512 // bp, s5_bb=bp, ssd_batch_major=True)
    ys, ssm_s, conv_s, re_s, im_s = _trunk(
        x_sample, mods_s,
        state_ssm.reshape(n_ssd, bs, rows_hp, D_STATE), state_conv,
        state_s5_re.reshape(n_s5, bs, S5_LANES), state_s5_im.reshape(n_s5, bs, S5_LANES), w,
        tm=512, s5_tb=ts, s5_bb=512 // ts, ssd_batch_major=False)

    ssm_shape = lambda b: (n_ssd, b, N_HEADS, HEAD_DIM, D_STATE)
    s5_shape = lambda b: (n_s5, b, S5_GROUPS, S5_STATE)
    return (yp, ys,
            ssm_p.reshape(ssm_shape(bp)), conv_p, re_p.reshape(s5_shape(bp)), im_p.reshape(s5_shape(bp)),
            ssm_s.reshape(ssm_shape(bs)), conv_s, re_s.reshape(s5_shape(bs)), im_s.reshape(s5_shape(bs)))
```

```python
import functools
import math

import jax
import jax.numpy as jnp
from jax import lax
from jax.experimental import pallas as pl
from jax.experimental.pallas import tpu as pltpu

F32 = jnp.float32
BF16 = jnp.bfloat16

D_MODEL = 1024
DEPTH = 4
D_INNER = 2048
HEAD_DIM = 64
N_HEADS = 32
N_GROUPS = 8
HEADS_PER_GROUP = 4
D_STATE = 128
CONV_K = 4
BC_DIM = 2 * N_GROUPS * D_STATE
CONV_DIM = D_INNER + BC_DIM
ZX_DIM = D_INNER + CONV_DIM
SSD_CHUNK = 128
S5_GROUPS = 64
S5_GROUP = 16
S5_STATE = 64
S5_LANES = S5_GROUPS * S5_STATE
S5_BLOCKS = D_MODEL // 128
D_FF = 2816
RMS_EPS = 1e-6
LANES = 128
SUBLANES = 8
CONV_PAD = 8
N_XTILES = D_INNER // LANES
GROUP_W = HEADS_PER_GROUP * HEAD_DIM
SSD_STEP_BATCHES = 4
VMEM_LIMIT = 52 * 1024 * 1024


def _cparams(sem):
    return pltpu.CompilerParams(dimension_semantics=sem, vmem_limit_bytes=VMEM_LIMIT)


def _sigmoid(x):
    return 1.0 / (1.0 + jnp.exp(-x))


def _silu(x):
    return x * _sigmoid(x)


def _rows_by_batch(x, nb):
    tm, d = x.shape
    return x.reshape(tm // nb, nb, d)


def _modulate_rows(x, sh, sc, nb):
    if nb == 1:
        r = lax.rsqrt(jnp.mean(x * x, axis=-1, keepdims=True) + RMS_EPS)
        return (x * r) * (1.0 + sc) + sh
    x3 = _rows_by_batch(x, nb)
    r = lax.rsqrt(jnp.mean(x3 * x3, axis=-1, keepdims=True) + RMS_EPS)
    h = (x3 * r) * (1.0 + sc[None]) + sh[None]
    return h.reshape(x.shape)


def _gate_rows(v, g, nb):
    if nb == 1:
        return v * g
    return (_rows_by_batch(v, nb) * g[None]).reshape(v.shape)


def _dot(a, b):
    return jnp.dot(a, b, preferred_element_type=F32)


def _dot_nt(a, b):
    return lax.dot_general(a, b, (((1,), (1,)), ((), ())), preferred_element_type=F32)


def _dot_tn(a, b):
    return lax.dot_general(a, b, (((0,), (0,)), ((), ())), preferred_element_type=F32)


def _ada_kernel(c_ref, w_ref, b_ref, o_ref):
    s = _silu(c_ref[...]).astype(BF16)
    o_ref[0] = _dot(s, w_ref[0].astype(BF16)) + b_ref[0]


def _ada(c_all, ada_w, ada_b):
    nrow = c_all.shape[0]
    tn = 1536
    return pl.pallas_call(
        _ada_kernel,
        grid=(DEPTH, 6 * D_MODEL // tn),
        in_specs=[
            pl.BlockSpec((nrow, D_MODEL), lambda l, j: (0, 0)),
            pl.BlockSpec((1, D_MODEL, tn), lambda l, j: (l, 0, j)),
            pl.BlockSpec((1, 1, tn), lambda l, j: (l, 0, j)),
        ],
        out_specs=pl.BlockSpec((1, nrow, tn), lambda l, j: (l, 0, j)),
        out_shape=jax.ShapeDtypeStruct((DEPTH, nrow, 6 * D_MODEL), F32),
        compiler_params=_cparams(("arbitrary", "arbitrary")),
        name="ada",
    )(c_all, ada_w, ada_b.reshape(DEPTH, 1, 6 * D_MODEL))


def _inproj_kernel(x_ref, sh_ref, sc_ref, w_ref, wdt_ref, zx_ref, dt_ref, h_scr, *, nb):
    j = pl.program_id(2)

    @pl.when(j == 0)
    def _():
        h = _modulate_rows(x_ref[...], sh_ref[...], sc_ref[...], nb).astype(BF16)
        h_scr[...] = h
        dt_ref[...] = _dot(h, wdt_ref[...])

    zx_ref[...] = _dot(h_scr[...], w_ref[...])


def _inproj(x3, sh, sc, w, wdt, tm):
    ng, rows, _ = x3.shape
    nb = sh.shape[1]
    tn = 2048
    per_g = pl.BlockSpec((None, nb, D_MODEL), lambda g, i, j: (g, 0, 0))
    return pl.pallas_call(
        functools.partial(_inproj_kernel, nb=nb),
        grid=(ng, rows // tm, ZX_DIM // tn),
        in_specs=[
            pl.BlockSpec((None, tm, D_MODEL), lambda g, i, j: (g, i, 0)),
            per_g, per_g,
            pl.BlockSpec((D_MODEL, tn), lambda g, i, j: (0, j)),
            pl.BlockSpec((D_MODEL, LANES), lambda g, i, j: (0, 0)),
        ],
        out_specs=[
            pl.BlockSpec((None, tm, tn), lambda g, i, j: (g, i, j)),
            pl.BlockSpec((None, tm, LANES), lambda g, i, j: (g, i, 0)),
        ],
        out_shape=[
            jax.ShapeDtypeStruct((ng, rows, ZX_DIM), F32),
            jax.ShapeDtypeStruct((ng, rows, LANES), F32),
        ],
        scratch_shapes=[pltpu.VMEM((tm, D_MODEL), BF16)],
        compiler_params=_cparams(("arbitrary", "arbitrary", "arbitrary")),
        name="ssd_inproj",
    )(x3, sh, sc, w, wdt)


def _cumsum_rows(a, L):
    rows = lax.broadcasted_iota(jnp.int32, a.shape, 0)
    k = 1
    while k < L:
        a = a + jnp.where(rows >= k, pltpu.roll(a, k, 0), 0.0)
        k *= 2
    return a


def _transpose_rows(a, L):
    if L < LANES:
        a = jnp.concatenate([a, jnp.zeros((LANES - L, LANES), F32)], axis=0)
    return a.T[:, :L]


def _ssd_step_kernel(zx_ref, dt_ref, conv0_ref, ssm0_ref,
                     cw_ref, cb_ref, dtb_ref, alog_ref, dexp_ref, nw_ref, gsum_ref, expand_ref,
                     *rest, L, nbatch):
    y_ref, ssm_out_ref, conv_out_ref, pad = rest[-4:]
    lo = CONV_PAD - (CONV_K - 1)
    row_t = lax.broadcasted_iota(jnp.int32, (L, LANES), 0)
    b_lo, c_lo = D_INNER, D_INNER + N_GROUPS * D_STATE

    xbcs, dts, acss, prods = [], [], [], []
    for i in range(nbatch):
        z0 = i * ZX_DIM
        pad[i, lo:CONV_PAD, :] = conv0_ref[i]
        pad[i, CONV_PAD:CONV_PAD + L, :] = zx_ref[:, z0 + D_INNER:z0 + ZX_DIM]
        acc = cb_ref[...]
        for k in range(CONV_K):
            acc = acc + cw_ref[k:k + 1, :] * pad[i, lo + k:lo + k + L, :]
        xbc = _silu(acc)
        conv_out_ref[i] = pad[i, lo + L:CONV_PAD + L, :]
        dtr = dt_ref[:, i * LANES:(i + 1) * LANES] + dtb_ref[...]
        dt = jnp.maximum(dtr, 0.0) + jnp.log1p(jnp.exp(-jnp.abs(dtr)))
        xbcs.append(xbc)
        dts.append(dt)
        acss.append(_cumsum_rows(dt * (-jnp.exp(alog_ref[...])), L))
        cm = xbc[:, c_lo:]
        prods += [cm * xbc[s:s + 1, b_lo:c_lo] for s in range(L)]

    p_hi, p_lo = _split_bf16(jnp.concatenate(prods, axis=0))
    cbh = _dot(jnp.concatenate([p_hi, p_lo], axis=1).astype(BF16), gsum_ref[...])

    per = L * L + 2 * L
    rows = []
    for i in range(nbatch):
        dt, acs = dts[i], acss[i]
        for s in range(L):
            decay = jnp.exp(jnp.where(row_t >= s, acs - acs[s:s + 1, :], -jnp.inf))
            r0 = (i * L + s) * L
            rows.append(cbh[r0:r0 + L, :] * decay * dt[s:s + 1, :])
        rows += [jnp.exp(acs), jnp.exp(acs[L - 1:L, :] - acs) * dt]
    w_hi, w_lo = _split_bf16(jnp.concatenate(rows, axis=0))
    wide = _dot(jnp.concatenate([w_hi, w_lo], axis=1).astype(BF16), expand_ref[...])

    for i in range(nbatch):
        z0, w0 = i * ZX_DIM, i * per
        xbc = xbcs[i]
        xs = xbc[:, :D_INNER]
        y = dexp_ref[...] * xs
        for s in range(L):
            y = y + wide[w0 + s * L:w0 + (s + 1) * L, :] * xs[s:s + 1, :]
        ea_x = wide[w0 + L * L:w0 + L * L + L, :]
        xw = xs * wide[w0 + L * L + L:w0 + per, :]
        e_last = jnp.exp(acss[i][L - 1:L, :])
        for g in range(N_GROUPS):
            gl = slice(g * GROUP_W, (g + 1) * GROUP_W)
            bg = xbc[:, b_lo + g * D_STATE:b_lo + (g + 1) * D_STATE].astype(BF16)
            cg = xbc[:, c_lo + g * D_STATE:c_lo + (g + 1) * D_STATE].astype(BF16)
            sg = ssm0_ref[i, gl, :]
            yg = y[:, gl] + ea_x[:, gl] * _dot_nt(cg, sg.astype(BF16))
            upd = _dot_tn(xw[:, gl].astype(BF16), bg)
            for r in range(HEADS_PER_GROUP):
                h = g * HEADS_PER_GROUP + r
                rr = slice(r * HEAD_DIM, (r + 1) * HEAD_DIM)
                ssm_out_ref[i, g * GROUP_W + r * HEAD_DIM:g * GROUP_W + (r + 1) * HEAD_DIM, :] = (
                    e_last[:, h:h + 1] * sg[rr, :] + upd[rr, :])
            yg = yg * _silu(zx_ref[:, z0 + g * GROUP_W:z0 + (g + 1) * GROUP_W])
            yg = yg * lax.rsqrt(jnp.mean(yg * yg, axis=-1, keepdims=True) + RMS_EPS)
            y0 = i * D_INNER + g * GROUP_W
            y_ref[:, y0:y0 + GROUP_W] = (yg * nw_ref[:, gl]).astype(y_ref.dtype)


def _split_bf16(v):
    hi = v.astype(BF16).astype(F32)
    return hi, (v - hi).astype(BF16).astype(F32)


def _ssd_chunk_kernel(z_ref, xr_ref, bcr_ref, dt_ref, conv0_ref, ssm0_ref,
                      cw_ref, cb_ref, dtb_ref, alog_ref, dexp_ref, nw_ref, expand_ref, *rest, L):
    y_ref, ssm_out_ref, conv_out_ref, pad, state_t = rest[-5:]
    c = pl.program_id(1)
    lo = CONV_PAD - (CONV_K - 1)

    @pl.when(c == 0)
    def _():
        state_t[...] = ssm0_ref[...].T
        for lt in range(2 * N_XTILES):
            pad[lt, lo:CONV_PAD, :] = conv0_ref[0, :, lt * LANES:(lt + 1) * LANES]

    for lt in range(N_XTILES):
        pad[lt, CONV_PAD:CONV_PAD + L, :] = xr_ref[:, lt * LANES:(lt + 1) * LANES]
        pad[N_XTILES + lt, CONV_PAD:CONV_PAD + L, :] = bcr_ref[:, lt * LANES:(lt + 1) * LANES]

    def conv_tile(lt):
        cols = slice(lt * LANES, (lt + 1) * LANES)
        acc = cb_ref[:, cols]
        for k in range(CONV_K):
            acc = acc + cw_ref[k:k + 1, cols] * pad[lt, lo + k:lo + k + L, :]
        return _silu(acc)

    tiles = [conv_tile(lt) for lt in range(2 * N_XTILES)]

    for lt in range(2 * N_XTILES):
        tail = pad[lt, lo + L:CONV_PAD + L, :]
        pad[lt, lo:CONV_PAD, :] = tail
        conv_out_ref[0, :, lt * LANES:(lt + 1) * LANES] = tail

    dtr = dt_ref[...] + dtb_ref[...]
    dt = jnp.maximum(dtr, 0.0) + jnp.log1p(jnp.exp(-jnp.abs(dtr)))
    a = dt * (-jnp.exp(alog_ref[...]))
    acs = _cumsum_rows(a, L)
    acs_t = acs.T
    dt_t = dt.T
    a_last = acs[L - 1:L, :]
    ea_hi, ea_lo = _split_bf16(jnp.exp(acs))
    we_hi, we_lo = _split_bf16(jnp.exp(a_last - acs) * dt)
    parts = jnp.concatenate([jnp.concatenate([ea_hi, ea_lo], axis=1),
                             jnp.concatenate([we_hi, we_lo], axis=1)], axis=0).astype(BF16)
    wide = _dot(parts, expand_ref[...])
    ea_x = wide[:L]
    we_x = wide[L:]

    causal = (lax.broadcasted_iota(jnp.int32, (L, L), 0)
              >= lax.broadcasted_iota(jnp.int32, (L, L), 1))
    lane_head = lax.broadcasted_iota(jnp.int32, (L, GROUP_W), 1) // HEAD_DIM

    for g in range(N_GROUPS):
        gl = slice(g * GROUP_W, (g + 1) * GROUP_W)
        bg = tiles[N_XTILES + g].astype(BF16)
        cg = tiles[N_XTILES + N_GROUPS + g].astype(BF16)
        cb = _dot_nt(cg, bg)
        sg = state_t[:, gl]
        y_off = _dot(cg, sg.astype(BF16))
        ms = []
        for r in range(HEADS_PER_GROUP):
            h = g * HEADS_PER_GROUP + r
            seg = acs[:, h:h + 1] - acs_t[h:h + 1, :]
            decay = jnp.exp(jnp.where(causal, seg, -jnp.inf))
            ms.append((cb * decay * dt_t[h:h + 1, :]).astype(BF16))
        xg = jnp.concatenate([tiles[2 * g], tiles[2 * g + 1]], axis=1)
        xgb = xg.astype(BF16)
        xblk = jnp.concatenate([jnp.where(lane_head == r, xgb, jnp.zeros_like(xgb))
                                for r in range(HEADS_PER_GROUP)], axis=0)
        y_diag = _dot(jnp.concatenate(ms, axis=1), xblk)
        yg = y_diag + ea_x[:, gl] * y_off + dexp_ref[:, gl] * xg
        upd = _dot_tn(bg, (xg * we_x[:, gl]).astype(BF16))
        state_t[:, gl] = ea_x[L - 1:L, gl] * sg + upd
        yg = yg * _silu(z_ref[:, gl])
        yg = yg * lax.rsqrt(jnp.mean(yg * yg, axis=-1, keepdims=True) + RMS_EPS)
        y_ref[:, gl] = (yg * nw_ref[:, gl]).astype(y_ref.dtype)

    @pl.when(c == pl.num_programs(1) - 1)
    def _():
        ssm_out_ref[...] = state_t[...].T


def _ssm_state_spec(grid_rank, j, nb):
    shape = (None, nb, N_HEADS * HEAD_DIM, D_STATE)
    if grid_rank == 2:
        return pl.BlockSpec(shape, lambda b, c: (j, b, 0, 0))
    return pl.BlockSpec(shape, lambda b: (j, b, 0, 0))


def _ssd_call(body, grid, in_specs, args, y_spec, y_shape, conv_spec, ssm_all, ssm_prev, j, nb, scratch, sem):
    nbatch = ssm_all.shape[1]
    in_specs = list(in_specs) + [pl.BlockSpec(memory_space=pl.ANY)]
    args = list(args) + [ssm_prev]
    aliases = {len(args) - 1: 1}
    return pl.pallas_call(
        body,
        grid=grid,
        in_specs=in_specs,
        out_specs=[y_spec, _ssm_state_spec(len(grid), j, nb), conv_spec],
        out_shape=[
            jax.ShapeDtypeStruct(y_shape, BF16),
            jax.ShapeDtypeStruct(ssm_all.shape, F32),
            jax.ShapeDtypeStruct((nbatch, CONV_K - 1, CONV_DIM), F32),
        ],
        scratch_shapes=scratch,
        input_output_aliases=aliases,
        compiler_params=_cparams(sem),
        name="ssd_core",
    )(*args)


def _ssd_chunked(zx, dtraw, conv0, ssm_all, ssm_prev, j, cw, cb, dtb, alog, dexp, nw):
    B, T, _ = zx.shape
    L = SSD_CHUNK
    full = lambda shape: pl.BlockSpec(shape, lambda b, c: (0,) * len(shape))
    col = lambda w, k: pl.BlockSpec((None, L, w), lambda b, c: (b, c, k))
    conv_spec = pl.BlockSpec((1, CONV_K - 1, CONV_DIM), lambda b, c: (b, 0, 0))
    sel = (jnp.arange(D_INNER)[None, :] // HEAD_DIM == jnp.arange(LANES)[:, None]).astype(BF16)
    in_specs = [
        col(D_INNER, 0), col(D_INNER, 1), col(D_INNER, 2), col(LANES, 0),
        conv_spec, _ssm_state_spec(2, j, None),
        full((CONV_K, CONV_DIM)), full((1, CONV_DIM)), full((1, LANES)), full((1, LANES)),
        full((1, D_INNER)), full((1, D_INNER)), full((2 * LANES, D_INNER)),
    ]
    args = [zx, zx, zx, dtraw, conv0, ssm_all, cw, cb, dtb, alog, dexp, nw, jnp.concatenate([sel, sel], axis=0)]
    scratch = [
        pltpu.VMEM((2 * N_XTILES, CONV_PAD + L, LANES), F32),
        pltpu.VMEM((D_STATE, N_HEADS * HEAD_DIM), F32),
    ]
    return _ssd_call(functools.partial(_ssd_chunk_kernel, L=L), (B, T // L), in_specs, args,
                     col(D_INNER, 0), (B, T, D_INNER), conv_spec, ssm_all, ssm_prev, j, None, scratch,
                     ("arbitrary", "arbitrary"))


def _ssd_step(zx, dtraw, conv0, ssm_all, ssm_prev, j, cw, cb, dtb, alog, dexp, nw, B):
    T = zx.shape[0]
    nb = SSD_STEP_BATCHES
    full = lambda shape: pl.BlockSpec(shape, lambda b: (0,) * len(shape))
    col = lambda w: pl.BlockSpec((T, nb * w), lambda b: (0, b))
    conv_spec = pl.BlockSpec((nb, CONV_K - 1, CONV_DIM), lambda b: (b, 0, 0))
    in_specs = [
        col(ZX_DIM), col(LANES), conv_spec, _ssm_state_spec(1, j, nb),
        full((CONV_K, CONV_DIM)), full((1, CONV_DIM)), full((1, LANES)), full((1, LANES)),
        full((1, D_INNER)), full((1, D_INNER)),
        full((2 * N_GROUPS * D_STATE, LANES)), full((2 * LANES, D_INNER)),
    ]
    head_ids = jnp.arange(LANES)
    gsum = ((jnp.arange(N_GROUPS * D_STATE)[:, None] // D_STATE == head_ids[None, :] // HEADS_PER_GROUP)
            & (head_ids[None, :] < N_HEADS)).astype(BF16)
    sel = (jnp.arange(D_INNER)[None, :] // HEAD_DIM == head_ids[:, None]).astype(BF16)
    args = [zx, dtraw, conv0, ssm_all, cw, cb, dtb, alog, dexp, nw,
            jnp.concatenate([gsum, gsum], axis=0), jnp.concatenate([sel, sel], axis=0)]
    scratch = [pltpu.VMEM((nb, CONV_PAD + T, CONV_DIM), F32)]
    return _ssd_call(functools.partial(_ssd_step_kernel, L=T, nbatch=nb), (B // nb,), in_specs, args,
                     col(D_INNER), (T, B * D_INNER), conv_spec, ssm_all, ssm_prev, j, nb, scratch,
                     ("arbitrary",))


def _ffn_kernel(*refs, nb, has_proj, final):
    if has_proj:
        (x_ref, y_ref, ow_ref, g1_ref, sh_ref, sc_ref, g2_ref, wg_ref, wu_ref, wo_ref, fw_ref,
         o_ref, x1_scr, h_scr, acc_scr) = refs
    else:
        (x_ref, sh_ref, sc_ref, g2_ref, wg_ref, wu_ref, wo_ref, fw_ref,
         o_ref, x1_scr, h_scr, acc_scr) = refs
    j = pl.program_id(2)

    @pl.when(j == 0)
    def _():
        x = x_ref[...]
        if has_proj:
            x = x + _gate_rows(_dot(y_ref[...], ow_ref[...]), g1_ref[...], nb)
        x1_scr[...] = x
        h_scr[...] = _modulate_rows(x, sh_ref[...], sc_ref[...], nb).astype(BF16)
        acc_scr[...] = jnp.zeros_like(acc_scr)

    h = h_scr[...]
    act = (_silu(_dot(h, wg_ref[...])) * _dot(h, wu_ref[...])).astype(BF16)
    acc_scr[...] += _dot(act, wo_ref[...])

    @pl.when(j == pl.num_programs(2) - 1)
    def _():
        x2 = x1_scr[...] + _gate_rows(acc_scr[...], g2_ref[...], nb)
        if final:
            x2 = x2 * lax.rsqrt(jnp.mean(x2 * x2, axis=-1, keepdims=True) + RMS_EPS) * fw_ref[...]
        o_ref[...] = x2


def _ffn(x3, y3, out_w, g1, sh, sc, g2, w_in, w_out, fw, tm, final):
    ng, rows, _ = x3.shape
    nb = sh.shape[1]
    tf = D_FF // 2
    nj = D_FF // tf
    has_proj = y3 is not None
    row_blk = lambda w: pl.BlockSpec((None, tm, w), lambda g, i, j: (g, i, 0))
    per_g = pl.BlockSpec((None, nb, D_MODEL), lambda g, i, j: (g, 0, 0))
    in_specs = [row_blk(D_MODEL)]
    args = [x3]
    if has_proj:
        in_specs += [row_blk(D_INNER), pl.BlockSpec((D_INNER, D_MODEL), lambda g, i, j: (0, 0)), per_g]
        args += [y3, out_w, g1]
    in_specs += [
        per_g, per_g, per_g,
        pl.BlockSpec((D_MODEL, tf), lambda g, i, j: (0, j)),
        pl.BlockSpec((D_MODEL, tf), lambda g, i, j: (0, nj + j)),
        pl.BlockSpec((tf, D_MODEL), lambda g, i, j: (j, 0)),
        pl.BlockSpec((1, D_MODEL), lambda g, i, j: (0, 0)),
    ]
    args += [sh, sc, g2, w_in, w_in, w_out, fw]
    return pl.pallas_call(
        functools.partial(_ffn_kernel, nb=nb, has_proj=has_proj, final=final),
        grid=(ng, rows // tm, nj),
        in_specs=in_specs,
        out_specs=row_blk(D_MODEL),
        out_shape=jax.ShapeDtypeStruct((ng, rows, D_MODEL), F32),
        scratch_shapes=[
            pltpu.VMEM((tm, D_MODEL), F32),
            pltpu.VMEM((tm, D_MODEL), BF16),
            pltpu.VMEM((tm, D_MODEL), F32),
        ],
        compiler_params=_cparams(("arbitrary", "arbitrary", "arbitrary")),
        name="proj_ffn" if has_proj else "ffn",
    )(*args)


def _s5_disc_kernel(are_ref, aim_ref, ldt_ref, bre_ref, bim_ref, abre_ref, abim_ref, bbre_ref, bbim_ref):
    lre, lim = are_ref[...], aim_ref[...]
    dt = jnp.exp(ldt_ref[...])
    mag = jnp.exp(lre * dt)
    ab_re, ab_im = mag * jnp.cos(lim * dt), mag * jnp.sin(lim * dt)
    den = lre * lre + lim * lim
    nr, ni = ab_re - 1.0, ab_im
    q_re = (nr * lre + ni * lim) / den
    q_im = (ni * lre - nr * lim) / den
    abre_ref[...] = ab_re
    abim_ref[...] = ab_im
    br, bi = bre_ref[...], bim_ref[...]
    bbre_ref[...] = q_re[:, None, :] * br - q_im[:, None, :] * bi
    bbim_ref[...] = q_re[:, None, :] * bi + q_im[:, None, :] * br


def _s5_disc(a_re, a_im, log_dt, b_re, b_im):
    gn = jax.ShapeDtypeStruct((S5_GROUPS, S5_STATE), F32)
    gin = jax.ShapeDtypeStruct((S5_GROUPS, S5_GROUP, S5_STATE), F32)
    return pl.pallas_call(
        _s5_disc_kernel,
        out_shape=[gn, gn, gin, gin],
        name="s5_disc",
    )(a_re, a_im, log_dt.reshape(S5_GROUPS, 1), b_re, b_im)


def _s5_kernel(x_ref, sh_ref, sc_ref, g1_ref, bb_ref, cc_ref, are_ref, aim_ref, s0re_ref, s0im_ref,
               dskip_ref, glu_ref, o_ref, sre_ref, sim_ref, xs_re, xs_im, st_re, st_im, *, tb, bb):
    t_idx = pl.program_id(1)
    rows = tb * bb

    @pl.when(t_idx == 0)
    def _():
        st_re[...] = s0re_ref[...]
        st_im[...] = s0im_ref[...]

    x = x_ref[...].reshape(rows, D_MODEL)
    h = _modulate_rows(x, sh_ref[...], sc_ref[...], bb)
    hb = h.astype(BF16)
    half = S5_LANES // S5_BLOCKS
    ys = []
    for k in range(S5_BLOCKS):
        ks = slice(k * half, (k + 1) * half)
        bu = _dot(hb[:, k * LANES:(k + 1) * LANES], bb_ref[k])
        ar, ai = are_ref[:, ks], aim_ref[:, ks]
        for rg in range(bb // SUBLANES):
            rs = slice(rg * SUBLANES, (rg + 1) * SUBLANES)
            xr, xi = st_re[rs, ks], st_im[rs, ks]
            for t in range(tb):
                sl = slice(t * bb + rg * SUBLANES, t * bb + (rg + 1) * SUBLANES)
                xr, xi = (ar * xr - ai * xi + bu[sl, :half], ar * xi + ai * xr + bu[sl, half:])
                xs_re[sl, ks] = xr
                xs_im[sl, ks] = xi
            st_re[rs, ks] = xr
            st_im[rs, ks] = xi
        ys.append(_dot(xs_re[:, ks].astype(BF16), cc_ref[k, :half, :])
                  - _dot(xs_im[:, ks].astype(BF16), cc_ref[k, half:, :]))
    y = jnp.concatenate(ys, axis=1) + dskip_ref[...] * h
    y = 0.5 * y * (1.0 + jnp.tanh(math.sqrt(2.0 / math.pi) * (y + 0.044715 * (y * y * y))))
    gl = _dot(y.astype(BF16), glu_ref[...])
    out = gl[:, :D_MODEL] * _sigmoid(gl[:, D_MODEL:])
    o_ref[...] = (x + _gate_rows(out, g1_ref[...], bb)).reshape(tb, bb, D_MODEL)

    @pl.when(t_idx == pl.num_programs(1) - 1)
    def _():
        sre_ref[...] = st_re[...]
        sim_ref[...] = st_im[...]


def _s5_layer(x3, sh, sc, g1, bbd, ccd, a_re, a_im, s0re, s0im, dskip, glu_w, tb, bb):
    T, B, _ = x3.shape
    per_b = pl.BlockSpec((bb, D_MODEL), lambda i, t: (i, 0))
    st = pl.BlockSpec((bb, S5_LANES), lambda i, t: (i, 0))
    full = lambda shape: pl.BlockSpec(shape, lambda i, t: (0,) * len(shape))
    return pl.pallas_call(
        functools.partial(_s5_kernel, tb=tb, bb=bb),
        grid=(B // bb, T // tb),
        in_specs=[
            pl.BlockSpec((tb, bb, D_MODEL), lambda i, t: (t, i, 0)),
            per_b, per_b, per_b,
            full((S5_BLOCKS, LANES, 2 * S5_LANES // S5_BLOCKS)),
            full((S5_BLOCKS, 2 * S5_LANES // S5_BLOCKS, LANES)),
            full((SUBLANES, S5_LANES)),
            full((SUBLANES, S5_LANES)),
            st, st,
            full((1, D_MODEL)),
            full((D_MODEL, 2 * D_MODEL)),
        ],
        out_specs=[pl.BlockSpec((tb, bb, D_MODEL), lambda i, t: (t, i, 0)), st, st],
        out_shape=[
            jax.ShapeDtypeStruct((T, B, D_MODEL), F32),
            jax.ShapeDtypeStruct((B, S5_LANES), F32),
            jax.ShapeDtypeStruct((B, S5_LANES), F32),
        ],
        scratch_shapes=[
            pltpu.VMEM((tb * bb, S5_LANES), F32),
            pltpu.VMEM((tb * bb, S5_LANES), F32),
            pltpu.VMEM((bb, S5_LANES), F32),
            pltpu.VMEM((bb, S5_LANES), F32),
        ],
        compiler_params=_cparams(("arbitrary", "arbitrary")),
        name="s5_layer",
    )(x3, sh, sc, g1, bbd, ccd, a_re, a_im, s0re, s0im, dskip, glu_w)


def _block_diag(w):
    _, r, c = w.shape
    w = w.reshape(S5_BLOCKS, 8, r, c)
    eye = jnp.eye(8, dtype=w.dtype)
    return (w[:, :, :, None, :] * eye[None, :, None, :, None]).reshape(S5_BLOCKS, 8 * r, 8 * c)


def _prep_params(p):
    w = {}
    w["ssd_in_w"] = [p["ssd_in_w"][j, :, :ZX_DIM].astype(BF16) for j in range(2)]
    w["ssd_dt_w"] = [jnp.pad(p["ssd_in_w"][j, :, ZX_DIM:], ((0, 0), (0, LANES - N_HEADS))).astype(BF16)
                     for j in range(2)]
    pad_h = lambda v: jnp.pad(v, (0, LANES - N_HEADS)).reshape(1, LANES)
    w["dt_bias"] = [pad_h(p["ssd_dt_bias"][j]) for j in range(2)]
    w["a_log"] = [pad_h(p["ssd_A_log"][j]) for j in range(2)]
    w["d_exp"] = [jnp.repeat(p["ssd_D"][j], HEAD_DIM).reshape(1, D_INNER) for j in range(2)]
    w["norm_w"] = [p["ssd_norm_w"][j].reshape(1, D_INNER) for j in range(2)]
    w["conv_w"] = [p["ssd_conv_w"][j] for j in range(2)]
    w["conv_b"] = [p["ssd_conv_b"][j].reshape(1, CONV_DIM) for j in range(2)]
    w["ssd_out_w"] = [p["ssd_out_w"][j].astype(BF16) for j in range(2)]
    w["ffn_w_in"] = [p["ffn_w_in"][i].astype(BF16) for i in range(DEPTH)]
    w["ffn_w_out"] = [p["ffn_w_out"][i].astype(BF16) for i in range(DEPTH)]
    w["final_w"] = p["final_norm_w"].reshape(1, D_MODEL)
    w["s5"] = []
    for j in range(2):
        ab_re, ab_im, bb_re, bb_im = _s5_disc(
            p["s5_A_re"][j], p["s5_A_im"][j], p["s5_log_dt"][j],
            jnp.swapaxes(p["s5_B_re"][j], 1, 2), jnp.swapaxes(p["s5_B_im"][j], 1, 2))
        bbd = jnp.concatenate([_block_diag(bb_re), _block_diag(bb_im)], axis=2).astype(BF16)
        c_re = jnp.swapaxes(p["s5_C_re"][j], 1, 2)
        c_im = jnp.swapaxes(p["s5_C_im"][j], 1, 2)
        ccd = jnp.concatenate([_block_diag(c_re), _block_diag(c_im)], axis=1).astype(BF16)
        bro = lambda v: jnp.broadcast_to(v.reshape(1, S5_LANES), (SUBLANES, S5_LANES))
        w["s5"].append(dict(bbd=bbd, ccd=ccd, a_re=bro(ab_re), a_im=bro(ab_im),
                            dskip=p["s5_D"][j].reshape(1, D_MODEL),
                            glu_w=p["s5_glu_w"][j].astype(BF16)))
    return w


def _trunk(x, mods, ssm0, conv0, re0, im0, w, tm, s5_tb, s5_bb, ssd_batch_major):
    B, T, _ = x.shape
    assert ssd_batch_major == (T % SSD_CHUNK == 0)
    ssm_new = jnp.zeros(ssm0.shape, ssm0.dtype)
    conv_new, re_new, im_new = [], [], []
    if not ssd_batch_major:
        x = jnp.swapaxes(x, 0, 1)
    for i in range(DEPTH):
        j = i // 2
        final = i == DEPTH - 1
        parts = [mods[i][:, k * D_MODEL:(k + 1) * D_MODEL] for k in range(6)]
        by_rows = [v[None] for v in parts]
        by_batch = [v[:, None] for v in parts]
        ffn_w = (w["ffn_w_in"][i], w["ffn_w_out"][i], w["final_w"], tm, final)
        if i % 2 == 0:
            sh1, sc1, g1, sh2, sc2, g2 = by_batch if ssd_batch_major else by_rows
            x3 = x if ssd_batch_major else x.reshape(1, T * B, D_MODEL)
            zx, dtraw = _inproj(x3, sh1, sc1, w["ssd_in_w"][j], w["ssd_dt_w"][j], tm)
            ssd_w = (w["conv_w"][j], w["conv_b"][j], w["dt_bias"][j], w["a_log"][j], w["d_exp"][j],
                     w["norm_w"][j])
            if ssd_batch_major:
                y, ssm_new, cv_new = _ssd_chunked(zx, dtraw, conv0[j], ssm0, ssm_new, j, *ssd_w)
                x = jnp.swapaxes(_ffn(x3, y, w["ssd_out_w"][j], g1, sh2, sc2, g2, *ffn_w), 0, 1)
            else:
                y, ssm_new, cv_new = _ssd_step(zx.reshape(T, B * ZX_DIM), dtraw.reshape(T, B * LANES),
                                               conv0[j], ssm0, ssm_new, j, *ssd_w, B)
                x = _ffn(x3, y.reshape(1, T * B, D_INNER), w["ssd_out_w"][j], g1, sh2, sc2, g2,
                         *ffn_w).reshape(T, B, D_MODEL)
            conv_new.append(cv_new)
        else:
            s5 = w["s5"][j]
            sh1, sc1, g1 = [v[0] for v in by_rows[:3]]
            xs5, sr, si = _s5_layer(x, sh1, sc1, g1, s5["bbd"], s5["ccd"], s5["a_re"], s5["a_im"],
                                    re0[j], im0[j], s5["dskip"], s5["glu_w"], s5_tb, s5_bb)
            re_new.append(sr)
            im_new.append(si)
            x = _ffn(xs5.reshape(1, T * B, D_MODEL), None, None, None, *by_rows[3:],
                     *ffn_w).reshape(T, B, D_MODEL)
            if ssd_batch_major:
                x = jnp.swapaxes(x, 0, 1)
    if not ssd_batch_major:
        x = jnp.swapaxes(x, 0, 1)
    return x, ssm_new, jnp.stack(conv_new), jnp.stack(re_new), jnp.stack(im_new)


def kernel(x_prompt, x_sample, state_ssm, state_conv, state_s5_re, state_s5_im, c_prompt, c_sample,
           ada_w, ada_b, ssd_in_w, ssd_conv_w, ssd_conv_b, ssd_dt_bias, ssd_A_log, ssd_D, ssd_norm_w,
           ssd_out_w, s5_A_re, s5_A_im, s5_log_dt, s5_B_re, s5_B_im, s5_C_re, s5_C_im, s5_D, s5_glu_w,
           ffn_w_in, ffn_w_out, final_norm_w):
    p = dict(ssd_in_w=ssd_in_w, ssd_conv_w=ssd_conv_w, ssd_conv_b=ssd_conv_b, ssd_dt_bias=ssd_dt_bias,
             ssd_A_log=ssd_A_log, ssd_D=ssd_D, ssd_norm_w=ssd_norm_w, ssd_out_w=ssd_out_w,
             s5_A_re=s5_A_re, s5_A_im=s5_A_im, s5_log_dt=s5_log_dt, s5_B_re=s5_B_re, s5_B_im=s5_B_im,
             s5_C_re=s5_C_re, s5_C_im=s5_C_im, s5_D=s5_D, s5_glu_w=s5_glu_w,
             ffn_w_in=ffn_w_in, ffn_w_out=ffn_w_out, final_norm_w=final_norm_w)
    w = _prep_params(p)
    bp, tp, _ = x_prompt.shape
    bs, ts, _ = x_sample.shape

    mods = _ada(jnp.concatenate([c_prompt, c_sample], axis=0), ada_w, ada_b)
    mods_p = [mods[l, :bp] for l in range(DEPTH)]
    mods_s = [mods[l, bp:] for l in range(DEPTH)]

    n_ssd, n_s5 = state_ssm.shape[0], state_s5_re.shape[0]
    rows_hp = N_HEADS * HEAD_DIM
    zeros_p = (jnp.zeros((n_ssd, bp, rows_hp, D_STATE), state_ssm.dtype),
               jnp.zeros((n_ssd, bp, CONV_K - 1, CONV_DIM), state_conv.dtype),
               jnp.zeros((n_s5, bp, S5_LANES), state_s5_re.dtype),
               jnp.zeros((n_s5, bp, S5_LANES), state_s5_im.dtype))
    yp, ssm_p, conv_p, re_p, im_p = _trunk(
        x_prompt, mods_p, *zeros_p, w,
        tm=512, s5_tb=512 // bp, s5_bb=bp, ssd_batch_major=True)
    ys, ssm_s, conv_s, re_s, im_s = _trunk(
        x_sample, mods_s,
        state_ssm.reshape(n_ssd, bs, rows_hp, D_STATE), state_conv,
        state_s5_re.reshape(n_s5, bs, S5_LANES), state_s5_im.reshape(n_s5, bs, S5_LANES), w,
        tm=512, s5_tb=ts, s5_bb=512 // ts, ssd_batch_major=False)

    ssm_shape = lambda b: (n_ssd, b, N_HEADS, HEAD_DIM, D_STATE)
    s5_shape = lambda b: (n_s5, b, S5_GROUPS, S5_STATE)
    return (yp, ys,
            ssm_p.reshape(ssm_shape(bp)), conv_p, re_p.reshape(s5_shape(bp)), im_p.reshape(s5_shape(bp)),
            ssm_s.reshape(ssm_shape(bs)), conv_s, re_s.reshape(s5_shape(bs)), im_s.reshape(s5_shape(bs)))
```

```python
import functools
import math

import jax
import jax.numpy as jnp
from jax import lax
from jax.experimental import pallas as pl
from jax.experimental.pallas import tpu as pltpu

F32 = jnp.float32
BF16 = jnp.bfloat16

D_MODEL = 1024
DEPTH = 4
D_INNER = 2048
HEAD_DIM = 64
N_HEADS = 32
N_GROUPS = 8
HEADS_PER_GROUP = 4
D_STATE = 128
CONV_K = 4
BC_DIM = 2 * N_GROUPS * D_STATE
CONV_DIM = D_INNER + BC_DIM
ZX_DIM = D_INNER + CONV_DIM
SSD_CHUNK = 128
S5_GROUPS = 64
S5_GROUP = 16
S5_STATE = 64
S5_LANES = S5_GROUPS * S5_STATE
S5_BLOCKS = D_MODEL // 128
D_FF = 2816
RMS_EPS = 1e-6
LANES = 128
SUBLANES = 8
CONV_PAD = 8
N_XTILES = D_INNER // LANES
GROUP_W = HEADS_PER_GROUP * HEAD_DIM
SSD_STEP_BATCHES = 4
FFN_SPLIT = 11
S5_GLU_SPLIT = 4
VMEM_LIMIT = 52 * 1024 * 1024


def _cparams(sem):
    return pltpu.CompilerParams(dimension_semantics=sem, vmem_limit_bytes=VMEM_LIMIT)


def _sigmoid(x):
    return 1.0 / (1.0 + jnp.exp(-x))


def _silu(x):
    return x * _sigmoid(x)


def _rows_by_batch(x, nb):
    tm, d = x.shape
    return x.reshape(tm // nb, nb, d)


def _modulate_rows(x, sh, sc, nb):
    if nb == 1:
        r = lax.rsqrt(jnp.mean(x * x, axis=-1, keepdims=True) + RMS_EPS)
        return (x * r) * (1.0 + sc) + sh
    x3 = _rows_by_batch(x, nb)
    r = lax.rsqrt(jnp.mean(x3 * x3, axis=-1, keepdims=True) + RMS_EPS)
    h = (x3 * r) * (1.0 + sc[None]) + sh[None]
    return h.reshape(x.shape)


def _gate_rows(v, g, nb):
    if nb == 1:
        return v * g
    return (_rows_by_batch(v, nb) * g[None]).reshape(v.shape)


def _dot(a, b):
    return jnp.dot(a, b, preferred_element_type=F32)


def _dot_nt(a, b):
    return lax.dot_general(a, b, (((1,), (1,)), ((), ())), preferred_element_type=F32)


def _dot_tn(a, b):
    return lax.dot_general(a, b, (((0,), (0,)), ((), ())), preferred_element_type=F32)


def _ada_kernel(c_ref, w_ref, b_ref, o_ref):
    s = _silu(c_ref[...]).astype(BF16)
    o_ref[0] = _dot(s, w_ref[0].astype(BF16)) + b_ref[0]


def _ada(c_all, ada_w, ada_b):
    nrow = c_all.shape[0]
    tn = 1536
    return pl.pallas_call(
        _ada_kernel,
        grid=(DEPTH, 6 * D_MODEL // tn),
        in_specs=[
            pl.BlockSpec((nrow, D_MODEL), lambda l, j: (0, 0)),
            pl.BlockSpec((1, D_MODEL, tn), lambda l, j: (l, 0, j)),
            pl.BlockSpec((1, 1, tn), lambda l, j: (l, 0, j)),
        ],
        out_specs=pl.BlockSpec((1, nrow, tn), lambda l, j: (l, 0, j)),
        out_shape=jax.ShapeDtypeStruct((DEPTH, nrow, 6 * D_MODEL), F32),
        compiler_params=_cparams(("arbitrary", "arbitrary")),
        name="ada",
    )(c_all, ada_w, ada_b.reshape(DEPTH, 1, 6 * D_MODEL))


def _inproj_kernel(x_ref, sh_ref, sc_ref, w_ref, wdt_ref, zx_ref, dt_ref, *, nb):
    h = _modulate_rows(x_ref[...], sh_ref[...], sc_ref[...], nb).astype(BF16)
    zx_ref[...] = _dot(h, w_ref[...])
    dt_ref[...] = _dot(h, wdt_ref[...])


def _inproj(x3, sh, sc, w, wdt, tm):
    ng, rows, _ = x3.shape
    nb = sh.shape[1]
    tn = 2048
    per_g = pl.BlockSpec((None, nb, D_MODEL), lambda j, g, i: (g, 0, 0))
    return pl.pallas_call(
        functools.partial(_inproj_kernel, nb=nb),
        grid=(ZX_DIM // tn, ng, rows // tm),
        in_specs=[
            pl.BlockSpec((None, tm, D_MODEL), lambda j, g, i: (g, i, 0)),
            per_g, per_g,
            pl.BlockSpec((D_MODEL, tn), lambda j, g, i: (0, j)),
            pl.BlockSpec((D_MODEL, LANES), lambda j, g, i: (0, 0)),
        ],
        out_specs=[
            pl.BlockSpec((None, tm, tn), lambda j, g, i: (g, i, j)),
            pl.BlockSpec((None, None, tm, LANES), lambda j, g, i: (j, g, i, 0)),
        ],
        out_shape=[
            jax.ShapeDtypeStruct((ng, rows, ZX_DIM), F32),
            jax.ShapeDtypeStruct((ZX_DIM // tn, ng, rows, LANES), F32),
        ],
        compiler_params=_cparams(("arbitrary", "arbitrary", "arbitrary")),
        name="ssd_inproj",
    )(x3, sh, sc, w, wdt)


def _cumsum_rows(a, L):
    rows = lax.broadcasted_iota(jnp.int32, a.shape, 0)
    k = 1
    while k < L:
        a = a + jnp.where(rows >= k, pltpu.roll(a, k, 0), 0.0)
        k *= 2
    return a


def _transpose_rows(a, L):
    if L < LANES:
        a = jnp.concatenate([a, jnp.zeros((LANES - L, LANES), F32)], axis=0)
    return a.T[:, :L]


def _ssd_step_kernel(zx_ref, dt_ref, conv0_ref, ssm0_ref,
                     cw_ref, cb_ref, dtb_ref, alog_ref, dexp_ref, nw_ref, gsum_ref, expand_ref,
                     *rest, L, nbatch):
    y_ref, ssm_out_ref, conv_out_ref, pad = rest[-4:]
    lo = CONV_PAD - (CONV_K - 1)
    row_t = lax.broadcasted_iota(jnp.int32, (L, LANES), 0)
    b_lo, c_lo = D_INNER, D_INNER + N_GROUPS * D_STATE

    xbcs, dts, acss, prods = [], [], [], []
    for i in range(nbatch):
        z0 = i * ZX_DIM
        pad[i, lo:CONV_PAD, :] = conv0_ref[i]
        pad[i, CONV_PAD:CONV_PAD + L, :] = zx_ref[:, z0 + D_INNER:z0 + ZX_DIM]
        acc = cb_ref[...]
        for k in range(CONV_K):
            acc = acc + cw_ref[k:k + 1, :] * pad[i, lo + k:lo + k + L, :]
        xbc = _silu(acc)
        conv_out_ref[i] = pad[i, lo + L:CONV_PAD + L, :]
        dtr = dt_ref[:, i * LANES:(i + 1) * LANES] + dtb_ref[...]
        dt = jnp.maximum(dtr, 0.0) + jnp.log1p(jnp.exp(-jnp.abs(dtr)))
        xbcs.append(xbc)
        dts.append(dt)
        acss.append(_cumsum_rows(dt * (-jnp.exp(alog_ref[...])), L))
        cm = xbc[:, c_lo:]
        prods += [cm * xbc[s:s + 1, b_lo:c_lo] for s in range(L)]

    p_hi, p_lo = _split_bf16(jnp.concatenate(prods, axis=0))
    cbh = _dot(jnp.concatenate([p_hi, p_lo], axis=1).astype(BF16), gsum_ref[...])

    per = L * L + 2 * L
    rows = []
    for i in range(nbatch):
        dt, acs = dts[i], acss[i]
        for s in range(L):
            decay = jnp.exp(jnp.where(row_t >= s, acs - acs[s:s + 1, :], -jnp.inf))
            r0 = (i * L + s) * L
            rows.append(cbh[r0:r0 + L, :] * decay * dt[s:s + 1, :])
        rows += [jnp.exp(acs), jnp.exp(acs[L - 1:L, :] - acs) * dt]
    w_hi, w_lo = _split_bf16(jnp.concatenate(rows, axis=0))
    wide = _dot(jnp.concatenate([w_hi, w_lo], axis=1).astype(BF16), expand_ref[...])

    for i in range(nbatch):
        z0, w0 = i * ZX_DIM, i * per
        xbc = xbcs[i]
        xs = xbc[:, :D_INNER]
        y = dexp_ref[...] * xs
        for s in range(L):
            y = y + wide[w0 + s * L:w0 + (s + 1) * L, :] * xs[s:s + 1, :]
        ea_x = wide[w0 + L * L:w0 + L * L + L, :]
        xw = xs * wide[w0 + L * L + L:w0 + per, :]
        e_last = jnp.exp(acss[i][L - 1:L, :])
        for g in range(N_GROUPS):
            gl = slice(g * GROUP_W, (g + 1) * GROUP_W)
            bg = xbc[:, b_lo + g * D_STATE:b_lo + (g + 1) * D_STATE].astype(BF16)
            cg = xbc[:, c_lo + g * D_STATE:c_lo + (g + 1) * D_STATE].astype(BF16)
            sg = ssm0_ref[i, gl, :]
            yg = y[:, gl] + ea_x[:, gl] * _dot_nt(cg, sg.astype(BF16))
            upd = _dot_tn(xw[:, gl].astype(BF16), bg)
            for r in range(HEADS_PER_GROUP):
                h = g * HEADS_PER_GROUP + r
                rr = slice(r * HEAD_DIM, (r + 1) * HEAD_DIM)
                ssm_out_ref[i, g * GROUP_W + r * HEAD_DIM:g * GROUP_W + (r + 1) * HEAD_DIM, :] = (
                    e_last[:, h:h + 1] * sg[rr, :] + upd[rr, :])
            yg = yg * _silu(zx_ref[:, z0 + g * GROUP_W:z0 + (g + 1) * GROUP_W])
            yg = yg * lax.rsqrt(jnp.mean(yg * yg, axis=-1, keepdims=True) + RMS_EPS)
            y0 = i * D_INNER + g * GROUP_W
            y_ref[:, y0:y0 + GROUP_W] = (yg * nw_ref[:, gl]).astype(y_ref.dtype)


def _split_bf16(v):
    hi = v.astype(BF16).astype(F32)
    return hi, (v - hi).astype(BF16).astype(F32)


def _ssd_chunk_kernel(z_ref, xr_ref, bcr_ref, dt_ref, conv0_ref, ssm0_ref,
                      cw_ref, cb_ref, dtb_ref, alog_ref, dexp_ref, nw_ref, expand_ref, *rest, L):
    y_ref, ssm_out_ref, conv_out_ref, pad, state_t = rest[-5:]
    c = pl.program_id(1)
    lo = CONV_PAD - (CONV_K - 1)

    @pl.when(c == 0)
    def _():
        state_t[...] = ssm0_ref[...].T
        for lt in range(2 * N_XTILES):
            pad[lt, lo:CONV_PAD, :] = conv0_ref[0, :, lt * LANES:(lt + 1) * LANES]

    for lt in range(N_XTILES):
        pad[lt, CONV_PAD:CONV_PAD + L, :] = xr_ref[:, lt * LANES:(lt + 1) * LANES]
        pad[N_XTILES + lt, CONV_PAD:CONV_PAD + L, :] = bcr_ref[:, lt * LANES:(lt + 1) * LANES]

    def conv_tile(lt):
        cols = slice(lt * LANES, (lt + 1) * LANES)
        acc = cb_ref[:, cols]
        for k in range(CONV_K):
            acc = acc + cw_ref[k:k + 1, cols] * pad[lt, lo + k:lo + k + L, :]
        return _silu(acc)

    tiles = [conv_tile(lt) for lt in range(2 * N_XTILES)]

    for lt in range(2 * N_XTILES):
        tail = pad[lt, lo + L:CONV_PAD + L, :]
        pad[lt, lo:CONV_PAD, :] = tail
        conv_out_ref[0, :, lt * LANES:(lt + 1) * LANES] = tail

    dtr = dt_ref[...] + dtb_ref[...]
    dt = jnp.maximum(dtr, 0.0) + jnp.log1p(jnp.exp(-jnp.abs(dtr)))
    a = dt * (-jnp.exp(alog_ref[...]))
    acs = _cumsum_rows(a, L)
    acs_t = acs.T
    dt_t = dt.T
    a_last = acs[L - 1:L, :]
    ea_hi, ea_lo = _split_bf16(jnp.exp(acs))
    we_hi, we_lo = _split_bf16(jnp.exp(a_last - acs) * dt)
    parts = jnp.concatenate([jnp.concatenate([ea_hi, ea_lo], axis=1),
                             jnp.concatenate([we_hi, we_lo], axis=1)], axis=0).astype(BF16)
    wide = _dot(parts, expand_ref[...])
    ea_x = wide[:L]
    we_x = wide[L:]

    causal = (lax.broadcasted_iota(jnp.int32, (L, L), 0)
              >= lax.broadcasted_iota(jnp.int32, (L, L), 1))
    lane_head = lax.broadcasted_iota(jnp.int32, (L, GROUP_W), 1) // HEAD_DIM

    for g in range(N_GROUPS):
        gl = slice(g * GROUP_W, (g + 1) * GROUP_W)
        bg = tiles[N_XTILES + g].astype(BF16)
        cg = tiles[N_XTILES + N_GROUPS + g].astype(BF16)
        cb = _dot_nt(cg, bg)
        sg = state_t[:, gl]
        y_off = _dot(cg, sg.astype(BF16))
        ms = []
        for r in range(HEADS_PER_GROUP):
            h = g * HEADS_PER_GROUP + r
            seg = acs[:, h:h + 1] - acs_t[h:h + 1, :]
            decay = jnp.exp(jnp.where(causal, seg, -jnp.inf))
            ms.append((cb * decay * dt_t[h:h + 1, :]).astype(BF16))
        xg = jnp.concatenate([tiles[2 * g], tiles[2 * g + 1]], axis=1)
        xgb = xg.astype(BF16)
        xblk = jnp.concatenate([jnp.where(lane_head == r, xgb, jnp.zeros_like(xgb))
                                for r in range(HEADS_PER_GROUP)], axis=0)
        y_diag = _dot(jnp.concatenate(ms, axis=1), xblk)
        yg = y_diag + ea_x[:, gl] * y_off + dexp_ref[:, gl] * xg
        upd = _dot_tn(bg, (xg * we_x[:, gl]).astype(BF16))
        state_t[:, gl] = ea_x[L - 1:L, gl] * sg + upd
        yg = yg * _silu(z_ref[:, gl])
        yg = yg * lax.rsqrt(jnp.mean(yg * yg, axis=-1, keepdims=True) + RMS_EPS)
        y_ref[:, gl] = (yg * nw_ref[:, gl]).astype(y_ref.dtype)

    @pl.when(c == pl.num_programs(1) - 1)
    def _():
        ssm_out_ref[...] = state_t[...].T


def _ssm_state_spec(grid_rank, j, nb):
    shape = (None, nb, N_HEADS * HEAD_DIM, D_STATE)
    if grid_rank == 2:
        return pl.BlockSpec(shape, lambda b, c: (j, b, 0, 0))
    return pl.BlockSpec(shape, lambda b: (j, b, 0, 0))


def _ssd_call(body, grid, in_specs, args, y_spec, y_shape, conv_spec, ssm_all, ssm_prev, j, nb, scratch, sem):
    nbatch = ssm_all.shape[1]
    in_specs = list(in_specs) + [pl.BlockSpec(memory_space=pl.ANY)]
    args = list(args) + [ssm_prev]
    aliases = {len(args) - 1: 1}
    return pl.pallas_call(
        body,
        grid=grid,
        in_specs=in_specs,
        out_specs=[y_spec, _ssm_state_spec(len(grid), j, nb), conv_spec],
        out_shape=[
            jax.ShapeDtypeStruct(y_shape, BF16),
            jax.ShapeDtypeStruct(ssm_all.shape, F32),
            jax.ShapeDtypeStruct((nbatch, CONV_K - 1, CONV_DIM), F32),
        ],
        scratch_shapes=scratch,
        input_output_aliases=aliases,
        compiler_params=_cparams(sem),
        name="ssd_core",
    )(*args)


def _ssd_chunked(zx, dtraw, conv0, ssm_all, ssm_prev, j, cw, cb, dtb, alog, dexp, nw):
    B, T, _ = zx.shape
    L = SSD_CHUNK
    full = lambda shape: pl.BlockSpec(shape, lambda b, c: (0,) * len(shape))
    col = lambda w, k: pl.BlockSpec((None, L, w), lambda b, c: (b, c, k))
    conv_spec = pl.BlockSpec((1, CONV_K - 1, CONV_DIM), lambda b, c: (b, 0, 0))
    sel = (jnp.arange(D_INNER)[None, :] // HEAD_DIM == jnp.arange(LANES)[:, None]).astype(BF16)
    in_specs = [
        col(D_INNER, 0), col(D_INNER, 1), col(D_INNER, 2),
        pl.BlockSpec((None, None, L, LANES), lambda b, c: (0, b, c, 0)),
        conv_spec, _ssm_state_spec(2, j, None),
        full((CONV_K, CONV_DIM)), full((1, CONV_DIM)), full((1, LANES)), full((1, LANES)),
        full((1, D_INNER)), full((1, D_INNER)), full((2 * LANES, D_INNER)),
    ]
    args = [zx, zx, zx, dtraw, conv0, ssm_all, cw, cb, dtb, alog, dexp, nw, jnp.concatenate([sel, sel], axis=0)]
    scratch = [
        pltpu.VMEM((2 * N_XTILES, CONV_PAD + L, LANES), F32),
        pltpu.VMEM((D_STATE, N_HEADS * HEAD_DIM), F32),
    ]
    return _ssd_call(functools.partial(_ssd_chunk_kernel, L=L), (B, T // L), in_specs, args,
                     col(D_INNER, 0), (B, T, D_INNER), conv_spec, ssm_all, ssm_prev, j, None, scratch,
                     ("arbitrary", "arbitrary"))


def _ssd_step(zx, dtraw, conv0, ssm_all, ssm_prev, j, cw, cb, dtb, alog, dexp, nw, B):
    T = zx.shape[0]
    nb = SSD_STEP_BATCHES
    full = lambda shape: pl.BlockSpec(shape, lambda b: (0,) * len(shape))
    col = lambda w: pl.BlockSpec((T, nb * w), lambda b: (0, b))
    conv_spec = pl.BlockSpec((nb, CONV_K - 1, CONV_DIM), lambda b: (b, 0, 0))
    in_specs = [
        col(ZX_DIM), col(LANES), conv_spec, _ssm_state_spec(1, j, nb),
        full((CONV_K, CONV_DIM)), full((1, CONV_DIM)), full((1, LANES)), full((1, LANES)),
        full((1, D_INNER)), full((1, D_INNER)),
        full((2 * N_GROUPS * D_STATE, LANES)), full((2 * LANES, D_INNER)),
    ]
    head_ids = jnp.arange(LANES)
    gsum = ((jnp.arange(N_GROUPS * D_STATE)[:, None] // D_STATE == head_ids[None, :] // HEADS_PER_GROUP)
            & (head_ids[None, :] < N_HEADS)).astype(BF16)
    sel = (jnp.arange(D_INNER)[None, :] // HEAD_DIM == head_ids[:, None]).astype(BF16)
    args = [zx, dtraw, conv0, ssm_all, cw, cb, dtb, alog, dexp, nw,
            jnp.concatenate([gsum, gsum], axis=0), jnp.concatenate([sel, sel], axis=0)]
    scratch = [pltpu.VMEM((nb, CONV_PAD + T, CONV_DIM), F32)]
    return _ssd_call(functools.partial(_ssd_step_kernel, L=T, nbatch=nb), (B // nb,), in_specs, args,
                     col(D_INNER), (T, B * D_INNER), conv_spec, ssm_all, ssm_prev, j, nb, scratch,
                     ("arbitrary",))


def _ffn_kernel(*refs, nb, has_proj, final):
    if has_proj:
        x_ref, y_ref, ow_ref, g1_ref, sh_ref, sc_ref, g2_ref, win_ref, wo_ref, fw_ref, o_ref = refs
    else:
        x_ref, sh_ref, sc_ref, g2_ref, win_ref, wo_ref, fw_ref, o_ref = refs
    x = x_ref[...]
    if has_proj:
        x = x + _gate_rows(_dot(y_ref[...], ow_ref[...]), g1_ref[...], nb)
    h = _modulate_rows(x, sh_ref[...], sc_ref[...], nb).astype(BF16)
    tf = D_FF // FFN_SPLIT
    acc = None
    for c in range(FFN_SPLIT):
        gate = _dot(h, win_ref[:, c * tf:(c + 1) * tf])
        up = _dot(h, win_ref[:, D_FF + c * tf:D_FF + (c + 1) * tf])
        part = _dot((_silu(gate) * up).astype(BF16), wo_ref[c * tf:(c + 1) * tf, :])
        acc = part if acc is None else acc + part
    x2 = x + _gate_rows(acc, g2_ref[...], nb)
    if final:
        x2 = x2 * lax.rsqrt(jnp.mean(x2 * x2, axis=-1, keepdims=True) + RMS_EPS) * fw_ref[...]
    o_ref[...] = x2


def _ffn(x3, y3, out_w, g1, sh, sc, g2, w_in, w_out, fw, tm, final):
    ng, rows, _ = x3.shape
    nb = sh.shape[1]
    has_proj = y3 is not None
    row_blk = lambda w: pl.BlockSpec((None, tm, w), lambda g, i: (g, i, 0))
    per_g = pl.BlockSpec((None, nb, D_MODEL), lambda g, i: (g, 0, 0))
    resident = lambda shape: pl.BlockSpec(shape, lambda g, i: (0,) * len(shape), pipeline_mode=pl.Buffered(1))
    in_specs = [row_blk(D_MODEL)]
    args = [x3]
    if has_proj:
        in_specs += [row_blk(D_INNER), resident((D_INNER, D_MODEL)), per_g]
        args += [y3, out_w, g1]
    in_specs += [per_g, per_g, per_g,
                 resident((D_MODEL, 2 * D_FF)), resident((D_FF, D_MODEL)), resident((1, D_MODEL))]
    args += [sh, sc, g2, w_in, w_out, fw]
    return pl.pallas_call(
        functools.partial(_ffn_kernel, nb=nb, has_proj=has_proj, final=final),
        grid=(ng, rows // tm),
        in_specs=in_specs,
        out_specs=row_blk(D_MODEL),
        out_shape=jax.ShapeDtypeStruct((ng, rows, D_MODEL), F32),
        compiler_params=_cparams(("arbitrary", "arbitrary")),
        name="proj_ffn" if has_proj else "ffn",
    )(*args)


def _s5_disc_kernel(are_ref, aim_ref, ldt_ref, bre_ref, bim_ref, abre_ref, abim_ref, bbre_ref, bbim_ref):
    lre, lim = are_ref[...], aim_ref[...]
    dt = jnp.exp(ldt_ref[...])
    mag = jnp.exp(lre * dt)
    ab_re, ab_im = mag * jnp.cos(lim * dt), mag * jnp.sin(lim * dt)
    den = lre * lre + lim * lim
    nr, ni = ab_re - 1.0, ab_im
    q_re = (nr * lre + ni * lim) / den
    q_im = (ni * lre - nr * lim) / den
    abre_ref[...] = ab_re
    abim_ref[...] = ab_im
    br, bi = bre_ref[...], bim_ref[...]
    bbre_ref[...] = q_re[:, None, :] * br - q_im[:, None, :] * bi
    bbim_ref[...] = q_re[:, None, :] * bi + q_im[:, None, :] * br


def _s5_disc(a_re, a_im, log_dt, b_re, b_im):
    gn = jax.ShapeDtypeStruct((S5_GROUPS, S5_STATE), F32)
    gin = jax.ShapeDtypeStruct((S5_GROUPS, S5_GROUP, S5_STATE), F32)
    return pl.pallas_call(
        _s5_disc_kernel,
        out_shape=[gn, gn, gin, gin],
        name="s5_disc",
    )(a_re, a_im, log_dt.reshape(S5_GROUPS, 1), b_re, b_im)


def _s5_kernel(x_ref, sh_ref, sc_ref, g1_ref, bb_ref, cc_ref, are_ref, aim_ref, s0re_ref, s0im_ref,
               dskip_ref, glu_ref, o_ref, sre_ref, sim_ref, xs_re, xs_im, st_re, st_im, *, tb, bb):
    t_idx = pl.program_id(1)
    rows = tb * bb

    @pl.when(t_idx == 0)
    def _():
        st_re[...] = s0re_ref[...]
        st_im[...] = s0im_ref[...]

    x = x_ref[...].reshape(rows, D_MODEL)
    h = _modulate_rows(x, sh_ref[...], sc_ref[...], bb)
    hb = h.astype(BF16)
    half = S5_LANES // S5_BLOCKS
    ys = []
    for k in range(S5_BLOCKS):
        ks = slice(k * half, (k + 1) * half)
        bu = _dot(hb[:, k * LANES:(k + 1) * LANES], bb_ref[k])
        ar, ai = are_ref[:, ks], aim_ref[:, ks]
        for rg in range(bb // SUBLANES):
            rs = slice(rg * SUBLANES, (rg + 1) * SUBLANES)
            xr, xi = st_re[rs, ks], st_im[rs, ks]
            for t in range(tb):
                sl = slice(t * bb + rg * SUBLANES, t * bb + (rg + 1) * SUBLANES)
                xr, xi = (ar * xr - ai * xi + bu[sl, :half], ar * xi + ai * xr + bu[sl, half:])
                xs_re[sl, ks] = xr
                xs_im[sl, ks] = xi
            st_re[rs, ks] = xr
            st_im[rs, ks] = xi
        ys.append(_dot(xs_re[:, ks].astype(BF16), cc_ref[k, :half, :])
                  - _dot(xs_im[:, ks].astype(BF16), cc_ref[k, half:, :]))
    y = jnp.concatenate(ys, axis=1) + dskip_ref[...] * h
    y = 0.5 * y * (1.0 + jnp.tanh(math.sqrt(2.0 / math.pi) * (y + 0.044715 * (y * y * y))))
    yb = y.astype(BF16)
    cw = D_MODEL // S5_GLU_SPLIT
    out = jnp.concatenate(
        [_dot(yb, glu_ref[:, c * cw:(c + 1) * cw])
         * _sigmoid(_dot(yb, glu_ref[:, D_MODEL + c * cw:D_MODEL + (c + 1) * cw]))
         for c in range(S5_GLU_SPLIT)], axis=1)
    o_ref[...] = (x + _gate_rows(out, g1_ref[...], bb)).reshape(tb, bb, D_MODEL)

    @pl.when(t_idx == pl.num_programs(1) - 1)
    def _():
        sre_ref[...] = st_re[...]
        sim_ref[...] = st_im[...]


def _s5_layer(x3, sh, sc, g1, bbd, ccd, a_re, a_im, s0re, s0im, dskip, glu_w, tb, bb):
    T, B, _ = x3.shape
    per_b = pl.BlockSpec((bb, D_MODEL), lambda i, t: (i, 0))
    st = pl.BlockSpec((bb, S5_LANES), lambda i, t: (i, 0))
    full = lambda shape: pl.BlockSpec(shape, lambda i, t: (0,) * len(shape))
    return pl.pallas_call(
        functools.partial(_s5_kernel, tb=tb, bb=bb),
        grid=(B // bb, T // tb),
        in_specs=[
            pl.BlockSpec((tb, bb, D_MODEL), lambda i, t: (t, i, 0)),
            per_b, per_b, per_b,
            full((S5_BLOCKS, LANES, 2 * S5_LANES // S5_BLOCKS)),
            full((S5_BLOCKS, 2 * S5_LANES // S5_BLOCKS, LANES)),
            full((SUBLANES, S5_LANES)),
            full((SUBLANES, S5_LANES)),
            st, st,
            full((1, D_MODEL)),
            full((D_MODEL, 2 * D_MODEL)),
        ],
        out_specs=[pl.BlockSpec((tb, bb, D_MODEL), lambda i, t: (t, i, 0)), st, st],
        out_shape=[
            jax.ShapeDtypeStruct((T, B, D_MODEL), F32),
            jax.ShapeDtypeStruct((B, S5_LANES), F32),
            jax.ShapeDtypeStruct((B, S5_LANES), F32),
        ],
        scratch_shapes=[
            pltpu.VMEM((tb * bb, S5_LANES), F32),
            pltpu.VMEM((tb * bb, S5_LANES), F32),
            pltpu.VMEM((bb, S5_LANES), F32),
            pltpu.VMEM((bb, S5_LANES), F32),
        ],
        compiler_params=_cparams(("arbitrary", "arbitrary")),
        name="s5_layer",
    )(x3, sh, sc, g1, bbd, ccd, a_re, a_im, s0re, s0im, dskip, glu_w)


def _block_diag(w):
    _, r, c = w.shape
    w = w.reshape(S5_BLOCKS, 8, r, c)
    eye = jnp.eye(8, dtype=w.dtype)
    return (w[:, :, :, None, :] * eye[None, :, None, :, None]).reshape(S5_BLOCKS, 8 * r, 8 * c)


def _prep_params(p):
    w = {}
    w["ssd_in_w"] = [p["ssd_in_w"][j, :, :ZX_DIM].astype(BF16) for j in range(2)]
    w["ssd_dt_w"] = [jnp.pad(p["ssd_in_w"][j, :, ZX_DIM:], ((0, 0), (0, LANES - N_HEADS))).astype(BF16)
                     for j in range(2)]
    pad_h = lambda v: jnp.pad(v, (0, LANES - N_HEADS)).reshape(1, LANES)
    w["dt_bias"] = [pad_h(p["ssd_dt_bias"][j]) for j in range(2)]
    w["a_log"] = [pad_h(p["ssd_A_log"][j]) for j in range(2)]
    w["d_exp"] = [jnp.repeat(p["ssd_D"][j], HEAD_DIM).reshape(1, D_INNER) for j in range(2)]
    w["norm_w"] = [p["ssd_norm_w"][j].reshape(1, D_INNER) for j in range(2)]
    w["conv_w"] = [p["ssd_conv_w"][j] for j in range(2)]
    w["conv_b"] = [p["ssd_conv_b"][j].reshape(1, CONV_DIM) for j in range(2)]
    w["ssd_out_w"] = [p["ssd_out_w"][j].astype(BF16) for j in range(2)]
    w["ffn_w_in"] = [p["ffn_w_in"][i].astype(BF16) for i in range(DEPTH)]
    w["ffn_w_out"] = [p["ffn_w_out"][i].astype(BF16) for i in range(DEPTH)]
    w["final_w"] = p["final_norm_w"].reshape(1, D_MODEL)
    w["s5"] = []
    for j in range(2):
        ab_re, ab_im, bb_re, bb_im = _s5_disc(
            p["s5_A_re"][j], p["s5_A_im"][j], p["s5_log_dt"][j],
            jnp.swapaxes(p["s5_B_re"][j], 1, 2), jnp.swapaxes(p["s5_B_im"][j], 1, 2))
        bbd = jnp.concatenate([_block_diag(bb_re), _block_diag(bb_im)], axis=2).astype(BF16)
        c_re = jnp.swapaxes(p["s5_C_re"][j], 1, 2)
        c_im = jnp.swapaxes(p["s5_C_im"][j], 1, 2)
        ccd = jnp.concatenate([_block_diag(c_re), _block_diag(c_im)], axis=1).astype(BF16)
        bro = lambda v: jnp.broadcast_to(v.reshape(1, S5_LANES), (SUBLANES, S5_LANES))
        w["s5"].append(dict(bbd=bbd, ccd=ccd, a_re=bro(ab_re), a_im=bro(ab_im),
                            dskip=p["s5_D"][j].reshape(1, D_MODEL),
                            glu_w=p["s5_glu_w"][j].astype(BF16)))
    return w


def _trunk(x, mods, ssm0, conv0, re0, im0, w, tm, s5_tb, s5_bb, ssd_batch_major):
    B, T, _ = x.shape
    assert ssd_batch_major == (T % SSD_CHUNK == 0)
    ssm_new = jnp.zeros(ssm0.shape, ssm0.dtype)
    conv_new, re_new, im_new = [], [], []
    if not ssd_batch_major:
        x = jnp.swapaxes(x, 0, 1)
    for i in range(DEPTH):
        j = i // 2
        final = i == DEPTH - 1
        parts = [mods[i][:, k * D_MODEL:(k + 1) * D_MODEL] for k in range(6)]
        by_rows = [v[None] for v in parts]
        by_batch = [v[:, None] for v in parts]
        ffn_w = (w["ffn_w_in"][i], w["ffn_w_out"][i], w["final_w"], tm, final)
        if i % 2 == 0:
            sh1, sc1, g1, sh2, sc2, g2 = by_batch if ssd_batch_major else by_rows
            x3 = x if ssd_batch_major else x.reshape(1, T * B, D_MODEL)
            zx, dtraw = _inproj(x3, sh1, sc1, w["ssd_in_w"][j], w["ssd_dt_w"][j], tm)
            ssd_w = (w["conv_w"][j], w["conv_b"][j], w["dt_bias"][j], w["a_log"][j], w["d_exp"][j],
                     w["norm_w"][j])
            if ssd_batch_major:
                y, ssm_new, cv_new = _ssd_chunked(zx, dtraw, conv0[j], ssm0, ssm_new, j, *ssd_w)
                x = jnp.swapaxes(_ffn(x3, y, w["ssd_out_w"][j], g1, sh2, sc2, g2, *ffn_w), 0, 1)
            else:
                y, ssm_new, cv_new = _ssd_step(zx.reshape(T, B * ZX_DIM), dtraw[0].reshape(T, B * LANES),
                                               conv0[j], ssm0, ssm_new, j, *ssd_w, B)
                x = _ffn(x3, y.reshape(1, T * B, D_INNER), w["ssd_out_w"][j], g1, sh2, sc2, g2,
                         *ffn_w).reshape(T, B, D_MODEL)
            conv_new.append(cv_new)
        else:
            s5 = w["s5"][j]
            sh1, sc1, g1 = [v[0] for v in by_rows[:3]]
            xs5, sr, si = _s5_layer(x, sh1, sc1, g1, s5["bbd"], s5["ccd"], s5["a_re"], s5["a_im"],
                                    re0[j], im0[j], s5["dskip"], s5["glu_w"], s5_tb, s5_bb)
            re_new.append(sr)
            im_new.append(si)
            x = _ffn(xs5.reshape(1, T * B, D_MODEL), None, None, None, *by_rows[3:],
                     *ffn_w).reshape(T, B, D_MODEL)
            if ssd_batch_major:
                x = jnp.swapaxes(x, 0, 1)
    if not ssd_batch_major:
        x = jnp.swapaxes(x, 0, 1)
    return x, ssm_new, jnp.stack(conv_new), jnp.stack(re_new), jnp.stack(im_new)


def kernel(x_prompt, x_sample, state_ssm, state_conv, state_s5_re, state_s5_im, c_prompt, c_sample,
           ada_w, ada_b, ssd_in_w, ssd_conv_w, ssd_conv_b, ssd_dt_bias, ssd_A_log, ssd_D, ssd_norm_w,
           ssd_out_w, s5_A_re, s5_A_im, s5_log_dt, s5_B_re, s5_B_im, s5_C_re, s5_C_im, s5_D, s5_glu_w,
           ffn_w_in, ffn_w_out, final_norm_w):
    p = dict(ssd_in_w=ssd_in_w, ssd_conv_w=ssd_conv_w, ssd_conv_b=ssd_conv_b, ssd_dt_bias=ssd_dt_bias,
             ssd_A_log=ssd_A_log, ssd_D=ssd_D, ssd_norm_w=ssd_norm_w, ssd_out_w=ssd_out_w,
             s5_A_re=s5_A_re, s5_A_im=s5_A_im, s5_log_dt=s5_log_dt, s5_B_re=s5_B_re, s5_B_im=s5_B_im,
             s5_C_re=s5_C_re, s5_C_im=s5_C_im, s5_D=s5_D, s5_glu_w=s5_glu_w,
             ffn_w_in=ffn_w_in, ffn_w_out=ffn_w_out, final_norm_w=final_norm_w)
    w = _prep_params(p)
    bp, tp, _ = x_prompt.shape
    bs, ts, _ = x_sample.shape

    mods = _ada(jnp.concatenate([c_prompt, c_sample], axis=0), ada_w, ada_b)
    mods_p = [mods[l, :bp] for l in range(DEPTH)]
    mods_s = [mods[l, bp:] for l in range(DEPTH)]

    n_ssd, n_s5 = state_ssm.shape[0], state_s5_re.shape[0]
    rows_hp = N_HEADS * HEAD_DIM
    zeros_p = (jnp.zeros((n_ssd, bp, rows_hp, D_STATE), state_ssm.dtype),
               jnp.zeros((n_ssd, bp, CONV_K - 1, CONV_DIM), state_conv.dtype),
               jnp.zeros((n_s5, bp, S5_LANES), state_s5_re.dtype),
               jnp.zeros((n_s5, bp, S5_LANES), state_s5_im.dtype))
    yp, ssm_p, conv_p, re_p, im_p = _trunk(
        x_prompt, mods_p, *zeros_p, w,
        tm=512, s5_tb=512 // bp, s5_bb=bp, ssd_batch_major=True)
    ys, ssm_s, conv_s, re_s, im_s = _trunk(
        x_sample, mods_s,
        state_ssm.reshape(n_ssd, bs, rows_hp, D_STATE), state_conv,
        state_s5_re.reshape(n_s5, bs, S5_LANES), state_s5_im.reshape(n_s5, bs, S5_LANES), w,
        tm=512, s5_tb=ts, s5_bb=512 // ts, ssd_batch_major=False)

    ssm_shape = lambda b: (n_ssd, b, N_HEADS, HEAD_DIM, D_STATE)
    s5_shape = lambda b: (n_s5, b, S5_GROUPS, S5_STATE)
    return (yp, ys,
            ssm_p.reshape(ssm_shape(bp)), conv_p, re_p.reshape(s5_shape(bp)), im_p.reshape(s5_shape(bp)),
            ssm_s.reshape(ssm_shape(bs)), conv_s, re_s.reshape(s5_shape(bs)), im_s.reshape(s5_shape(bs)))
```

```python
import functools
import math

import jax
import jax.numpy as jnp
from jax import lax
from jax.experimental import pallas as pl
from jax.experimental.pallas import tpu as pltpu

F32 = jnp.float32
BF16 = jnp.bfloat16

D_MODEL = 1024
DEPTH = 4
D_INNER = 2048
HEAD_DIM = 64
N_HEADS = 32
N_GROUPS = 8
HEADS_PER_GROUP = 4
D_STATE = 128
CONV_K = 4
BC_DIM = 2 * N_GROUPS * D_STATE
CONV_DIM = D_INNER + BC_DIM
ZX_DIM = D_INNER + CONV_DIM
SSD_CHUNK = 128
S5_GROUPS = 64
S5_GROUP = 16
S5_STATE = 64
S5_LANES = S5_GROUPS * S5_STATE
S5_BLOCKS = D_MODEL // 128
D_FF = 2816
RMS_EPS = 1e-6
LANES = 128
SUBLANES = 8
CONV_PAD = 8
N_XTILES = D_INNER // LANES
GROUP_W = HEADS_PER_GROUP * HEAD_DIM
SSD_STEP_BATCHES = 4
FFN_SPLIT = 11
S5_GLU_SPLIT = 4
INPROJ_COLS = 512
VMEM_LIMIT = 52 * 1024 * 1024


def _cparams(sem):
    return pltpu.CompilerParams(dimension_semantics=sem, vmem_limit_bytes=VMEM_LIMIT)


def _sigmoid(x):
    return 1.0 / (1.0 + jnp.exp(-x))


def _silu(x):
    return x * _sigmoid(x)


def _rows_by_batch(x, nb):
    tm, d = x.shape
    return x.reshape(tm // nb, nb, d)


def _modulate_rows(x, sh, sc, nb):
    if nb == 1:
        r = lax.rsqrt(jnp.mean(x * x, axis=-1, keepdims=True) + RMS_EPS)
        return (x * r) * (1.0 + sc) + sh
    x3 = _rows_by_batch(x, nb)
    r = lax.rsqrt(jnp.mean(x3 * x3, axis=-1, keepdims=True) + RMS_EPS)
    h = (x3 * r) * (1.0 + sc[None]) + sh[None]
    return h.reshape(x.shape)


def _gate_rows(v, g, nb):
    if nb == 1:
        return v * g
    return (_rows_by_batch(v, nb) * g[None]).reshape(v.shape)


def _dot(a, b):
    return jnp.dot(a, b, preferred_element_type=F32)


def _dot_nt(a, b):
    return lax.dot_general(a, b, (((1,), (1,)), ((), ())), preferred_element_type=F32)


def _dot_tn(a, b):
    return lax.dot_general(a, b, (((0,), (0,)), ((), ())), preferred_element_type=F32)


def _ada_kernel(c_ref, w_ref, b_ref, o_ref):
    s = _silu(c_ref[...]).astype(BF16)
    o_ref[0] = _dot(s, w_ref[0].astype(BF16)) + b_ref[0]


def _ada(c_all, ada_w, ada_b):
    nrow = c_all.shape[0]
    tn = 1536
    return pl.pallas_call(
        _ada_kernel,
        grid=(DEPTH, 6 * D_MODEL // tn),
        in_specs=[
            pl.BlockSpec((nrow, D_MODEL), lambda l, j: (0, 0)),
            pl.BlockSpec((1, D_MODEL, tn), lambda l, j: (l, 0, j)),
            pl.BlockSpec((1, 1, tn), lambda l, j: (l, 0, j)),
        ],
        out_specs=pl.BlockSpec((1, nrow, tn), lambda l, j: (l, 0, j)),
        out_shape=jax.ShapeDtypeStruct((DEPTH, nrow, 6 * D_MODEL), F32),
        compiler_params=_cparams(("arbitrary", "arbitrary")),
        name="ada",
    )(c_all, ada_w, ada_b.reshape(DEPTH, 1, 6 * D_MODEL))


def _inproj_kernel(x_ref, sh_ref, sc_ref, w_ref, wdt_ref, zx_ref, dt_ref, *, nb):
    h = _modulate_rows(x_ref[...], sh_ref[...], sc_ref[...], nb).astype(BF16)
    zx_ref[...] = _dot(h, w_ref[...])
    dt_ref[...] = _dot(h, wdt_ref[...])


def _inproj(x3, sh, sc, w, wdt, tm):
    ng, rows, _ = x3.shape
    nb = sh.shape[1]
    tn = 2048
    per_g = pl.BlockSpec((None, nb, D_MODEL), lambda j, g, i: (g, 0, 0))
    return pl.pallas_call(
        functools.partial(_inproj_kernel, nb=nb),
        grid=(ZX_DIM // tn, ng, rows // tm),
        in_specs=[
            pl.BlockSpec((None, tm, D_MODEL), lambda j, g, i: (g, i, 0)),
            per_g, per_g,
            pl.BlockSpec((D_MODEL, tn), lambda j, g, i: (0, j)),
            pl.BlockSpec((D_MODEL, LANES), lambda j, g, i: (0, 0)),
        ],
        out_specs=[
            pl.BlockSpec((None, tm, tn), lambda j, g, i: (g, i, j)),
            pl.BlockSpec((None, None, tm, LANES), lambda j, g, i: (j, g, i, 0)),
        ],
        out_shape=[
            jax.ShapeDtypeStruct((ng, rows, ZX_DIM), F32),
            jax.ShapeDtypeStruct((ZX_DIM // tn, ng, rows, LANES), F32),
        ],
        compiler_params=_cparams(("arbitrary", "arbitrary", "arbitrary")),
        name="ssd_inproj",
    )(x3, sh, sc, w, wdt)


def _cumsum_rows(a, L):
    rows = lax.broadcasted_iota(jnp.int32, a.shape, 0)
    k = 1
    while k < L:
        a = a + jnp.where(rows >= k, pltpu.roll(a, k, 0), 0.0)
        k *= 2
    return a


def _transpose_rows(a, L):
    if L < LANES:
        a = jnp.concatenate([a, jnp.zeros((LANES - L, LANES), F32)], axis=0)
    return a.T[:, :L]


def _ssd_step_kernel(zx_ref, dt_ref, conv0_ref, ssm0_ref,
                     cw_ref, cb_ref, dtb_ref, alog_ref, dexp_ref, nw_ref, gsum_ref, expand_ref,
                     *rest, L, nbatch):
    y_ref, ssm_out_ref, conv_out_ref, pad = rest[-4:]
    lo = CONV_PAD - (CONV_K - 1)
    row_t = lax.broadcasted_iota(jnp.int32, (L, LANES), 0)
    b_lo, c_lo = D_INNER, D_INNER + N_GROUPS * D_STATE

    xbcs, dts, acss, prods = [], [], [], []
    for i in range(nbatch):
        z0 = i * ZX_DIM
        pad[i, lo:CONV_PAD, :] = conv0_ref[i]
        pad[i, CONV_PAD:CONV_PAD + L, :] = zx_ref[:, z0 + D_INNER:z0 + ZX_DIM]
        acc = cb_ref[...]
        for k in range(CONV_K):
            acc = acc + cw_ref[k:k + 1, :] * pad[i, lo + k:lo + k + L, :]
        xbc = _silu(acc)
        conv_out_ref[i] = pad[i, lo + L:CONV_PAD + L, :]
        dtr = dt_ref[:, i * LANES:(i + 1) * LANES] + dtb_ref[...]
        dt = jnp.maximum(dtr, 0.0) + jnp.log1p(jnp.exp(-jnp.abs(dtr)))
        xbcs.append(xbc)
        dts.append(dt)
        acss.append(_cumsum_rows(dt * (-jnp.exp(alog_ref[...])), L))
        cm = xbc[:, c_lo:]
        prods += [cm * xbc[s:s + 1, b_lo:c_lo] for s in range(L)]

    p_hi, p_lo = _split_bf16(jnp.concatenate(prods, axis=0))
    cbh = _dot(jnp.concatenate([p_hi, p_lo], axis=1).astype(BF16), gsum_ref[...])

    per = L * L + 2 * L
    rows = []
    for i in range(nbatch):
        dt, acs = dts[i], acss[i]
        for s in range(L):
            decay = jnp.exp(jnp.where(row_t >= s, acs - acs[s:s + 1, :], -jnp.inf))
            r0 = (i * L + s) * L
            rows.append(cbh[r0:r0 + L, :] * decay * dt[s:s + 1, :])
        rows += [jnp.exp(acs), jnp.exp(acs[L - 1:L, :] - acs) * dt]
    w_hi, w_lo = _split_bf16(jnp.concatenate(rows, axis=0))
    wide = _dot(jnp.concatenate([w_hi, w_lo], axis=1).astype(BF16), expand_ref[...])

    for i in range(nbatch):
        z0, w0 = i * ZX_DIM, i * per
        xbc = xbcs[i]
        xs = xbc[:, :D_INNER]
        y = dexp_ref[...] * xs
        for s in range(L):
            y = y + wide[w0 + s * L:w0 + (s + 1) * L, :] * xs[s:s + 1, :]
        ea_x = wide[w0 + L * L:w0 + L * L + L, :]
        xw = xs * wide[w0 + L * L + L:w0 + per, :]
        e_last = jnp.exp(acss[i][L - 1:L, :])
        for g in range(N_GROUPS):
            gl = slice(g * GROUP_W, (g + 1) * GROUP_W)
            bg = xbc[:, b_lo + g * D_STATE:b_lo + (g + 1) * D_STATE].astype(BF16)
            cg = xbc[:, c_lo + g * D_STATE:c_lo + (g + 1) * D_STATE].astype(BF16)
            sg = ssm0_ref[i, gl, :]
            yg = y[:, gl] + ea_x[:, gl] * _dot_nt(cg, sg.astype(BF16))
            upd = _dot_tn(xw[:, gl].astype(BF16), bg)
            for r in range(HEADS_PER_GROUP):
                h = g * HEADS_PER_GROUP + r
                rr = slice(r * HEAD_DIM, (r + 1) * HEAD_DIM)
                ssm_out_ref[i, g * GROUP_W + r * HEAD_DIM:g * GROUP_W + (r + 1) * HEAD_DIM, :] = (
                    e_last[:, h:h + 1] * sg[rr, :] + upd[rr, :])
            yg = yg * _silu(zx_ref[:, z0 + g * GROUP_W:z0 + (g + 1) * GROUP_W])
            yg = yg * lax.rsqrt(jnp.mean(yg * yg, axis=-1, keepdims=True) + RMS_EPS)
            y0 = i * D_INNER + g * GROUP_W
            y_ref[:, y0:y0 + GROUP_W] = (yg * nw_ref[:, gl]).astype(y_ref.dtype)


def _split_bf16(v):
    hi = v.astype(BF16).astype(F32)
    return hi, (v - hi).astype(BF16).astype(F32)


def _ssd_chunk_kernel(x_ref, xn_ref, sh_ref, sc_ref, win_ref, wdt_ref, conv0_ref, ssm0_ref,
                      cw_ref, cb_ref, dtb_ref, alog_ref, dexp_ref, nw_ref, expand_ref, *rest, L):
    y_ref, ssm_out_ref, conv_out_ref, pad, state_t, z_scr, dt_scr = rest[-7:]
    c = pl.program_id(1)
    lo = CONV_PAD - (CONV_K - 1)

    def projection_steps(x_rows, slot):
        h = _modulate_rows(x_rows, sh_ref[...], sc_ref[...], 1).astype(BF16)
        pw = INPROJ_COLS

        def z_step(q):
            z_scr[slot, :, q * pw:(q + 1) * pw] = _dot(h, win_ref[:, q * pw:(q + 1) * pw])

        def xbc_step(q):
            res = _dot(h, win_ref[:, D_INNER + q * pw:D_INNER + (q + 1) * pw])
            for r in range(pw // LANES):
                pad[slot, q * (pw // LANES) + r, CONV_PAD:CONV_PAD + L, :] = res[:, r * LANES:(r + 1) * LANES]

        def dt_step():
            dt_scr[slot] = _dot(h, wdt_ref[...])

        return ([functools.partial(xbc_step, q) for q in range(CONV_DIM // pw)]
                + [functools.partial(z_step, q) for q in range(D_INNER // pw)] + [dt_step])

    @pl.when(c == 0)
    def _():
        state_t[...] = ssm0_ref[...].T
        for lt in range(2 * N_XTILES):
            pad[0, lt, lo:CONV_PAD, :] = conv0_ref[0, :, lt * LANES:(lt + 1) * LANES]
        for step in projection_steps(x_ref[:L, :], 0):
            step()

    _ssd_chunk_body(0, slice(0, L), z_scr, dt_scr, pad, state_t, cw_ref, cb_ref, dtb_ref, alog_ref,
                    dexp_ref, nw_ref, expand_ref, y_ref, L, projection_steps(x_ref[L:, :], 1))
    _ssd_chunk_body(1, slice(L, 2 * L), z_scr, dt_scr, pad, state_t, cw_ref, cb_ref, dtb_ref, alog_ref,
                    dexp_ref, nw_ref, expand_ref, y_ref, L, projection_steps(xn_ref[...], 0))
    for lt in range(2 * N_XTILES):
        conv_out_ref[0, :, lt * LANES:(lt + 1) * LANES] = pad[0, lt, lo:CONV_PAD, :]

    @pl.when(c == pl.num_programs(1) - 1)
    def _():
        ssm_out_ref[...] = state_t[...].T


def _ssd_chunk_body(slot, rows, z_scr, dt_scr, pad, state_t, cw_ref, cb_ref, dtb_ref, alog_ref,
                    dexp_ref, nw_ref, expand_ref, y_ref, L, side_steps):
    lo = CONV_PAD - (CONV_K - 1)
    pending = iter(side_steps)

    def run_side_step():
        step = next(pending, None)
        if step is not None:
            step()

    def conv_tile(lt):
        cols = slice(lt * LANES, (lt + 1) * LANES)
        acc = cb_ref[:, cols]
        for k in range(CONV_K):
            acc = acc + cw_ref[k:k + 1, cols] * pad[slot, lt, lo + k:lo + k + L, :]
        return _silu(acc)

    tiles = []
    for lt in range(2 * N_XTILES):
        tiles.append(conv_tile(lt))
        if lt % 4 == 3:
            run_side_step()

    for lt in range(2 * N_XTILES):
        pad[1 - slot, lt, lo:CONV_PAD, :] = pad[slot, lt, lo + L:CONV_PAD + L, :]

    dtr = dt_scr[slot] + dtb_ref[...]
    dt = jnp.maximum(dtr, 0.0) + jnp.log1p(jnp.exp(-jnp.abs(dtr)))
    a = dt * (-jnp.exp(alog_ref[...]))
    acs = _cumsum_rows(a, L)
    acs_t = acs.T
    dt_t = dt.T
    a_last = acs[L - 1:L, :]
    ea_hi, ea_lo = _split_bf16(jnp.exp(acs))
    we_hi, we_lo = _split_bf16(jnp.exp(a_last - acs) * dt)
    parts = jnp.concatenate([jnp.concatenate([ea_hi, ea_lo], axis=1),
                             jnp.concatenate([we_hi, we_lo], axis=1)], axis=0).astype(BF16)
    wide = _dot(parts, expand_ref[...])
    ea_x = wide[:L]
    we_x = wide[L:]

    causal = (lax.broadcasted_iota(jnp.int32, (L, L), 0)
              >= lax.broadcasted_iota(jnp.int32, (L, L), 1))
    lane_head = lax.broadcasted_iota(jnp.int32, (L, GROUP_W), 1) // HEAD_DIM

    for g in range(N_GROUPS):
        gl = slice(g * GROUP_W, (g + 1) * GROUP_W)
        bg = tiles[N_XTILES + g].astype(BF16)
        cg = tiles[N_XTILES + N_GROUPS + g].astype(BF16)
        cb = _dot_nt(cg, bg)
        sg = state_t[:, gl]
        y_off = _dot(cg, sg.astype(BF16))
        ms = []
        for r in range(HEADS_PER_GROUP):
            h = g * HEADS_PER_GROUP + r
            seg = acs[:, h:h + 1] - acs_t[h:h + 1, :]
            decay = jnp.exp(jnp.where(causal, seg, -jnp.inf))
            ms.append((cb * decay * dt_t[h:h + 1, :]).astype(BF16))
        xg = jnp.concatenate([tiles[2 * g], tiles[2 * g + 1]], axis=1)
        xgb = xg.astype(BF16)
        xblk = jnp.concatenate([jnp.where(lane_head == r, xgb, jnp.zeros_like(xgb))
                                for r in range(HEADS_PER_GROUP)], axis=0)
        y_diag = _dot(jnp.concatenate(ms, axis=1), xblk)
        yg = y_diag + ea_x[:, gl] * y_off + dexp_ref[:, gl] * xg
        upd = _dot_tn(bg, (xg * we_x[:, gl]).astype(BF16))
        state_t[:, gl] = ea_x[L - 1:L, gl] * sg + upd
        yg = yg * _silu(z_scr[slot, :, gl])
        yg = yg * lax.rsqrt(jnp.mean(yg * yg, axis=-1, keepdims=True) + RMS_EPS)
        y_ref[rows, gl] = (yg * nw_ref[:, gl]).astype(y_ref.dtype)
        run_side_step()
    for step in pending:
        step()


def _ssm_state_spec(grid_rank, j, nb):
    shape = (None, nb, N_HEADS * HEAD_DIM, D_STATE)
    if grid_rank == 2:
        return pl.BlockSpec(shape, lambda b, c: (j, b, 0, 0))
    return pl.BlockSpec(shape, lambda b: (j, b, 0, 0))


def _ssd_call(body, grid, in_specs, args, y_spec, y_shape, conv_spec, ssm_all, ssm_prev, j, nb, scratch, sem):
    nbatch = ssm_all.shape[1]
    in_specs = list(in_specs) + [pl.BlockSpec(memory_space=pl.ANY)]
    args = list(args) + [ssm_prev]
    aliases = {len(args) - 1: 1}
    return pl.pallas_call(
        body,
        grid=grid,
        in_specs=in_specs,
        out_specs=[y_spec, _ssm_state_spec(len(grid), j, nb), conv_spec],
        out_shape=[
            jax.ShapeDtypeStruct(y_shape, BF16),
            jax.ShapeDtypeStruct(ssm_all.shape, F32),
            jax.ShapeDtypeStruct((nbatch, CONV_K - 1, CONV_DIM), F32),
        ],
        scratch_shapes=scratch,
        input_output_aliases=aliases,
        compiler_params=_cparams(sem),
        name="ssd_core",
    )(*args)


def _ssd_chunked(x, sh, sc, w_in, w_dt, conv0, ssm_all, ssm_prev, j, cw, cb, dtb, alog, dexp, nw):
    B, T, _ = x.shape
    L = SSD_CHUNK
    nsteps = T // (2 * L)
    last_chunk = T // L - 1
    full = lambda shape: pl.BlockSpec(shape, lambda b, c: (0,) * len(shape))
    resident = lambda shape: pl.BlockSpec(shape, lambda b, c: (0,) * len(shape), pipeline_mode=pl.Buffered(1))
    pair = lambda w: pl.BlockSpec((None, 2 * L, w), lambda b, c: (b, c, 0))
    nxt = pl.BlockSpec((None, L, D_MODEL), lambda b, c: (b, jnp.minimum(2 * c + 2, last_chunk), 0))
    per_b = pl.BlockSpec((None, 1, D_MODEL), lambda b, c: (b, 0, 0))
    conv_spec = pl.BlockSpec((1, CONV_K - 1, CONV_DIM), lambda b, c: (b, 0, 0))
    sel = (jnp.arange(D_INNER)[None, :] // HEAD_DIM == jnp.arange(LANES)[:, None]).astype(BF16)
    in_specs = [
        pair(D_MODEL), nxt, per_b, per_b, resident((D_MODEL, ZX_DIM)), resident((D_MODEL, LANES)),
        conv_spec, _ssm_state_spec(2, j, None),
        full((CONV_K, CONV_DIM)), full((1, CONV_DIM)), full((1, LANES)), full((1, LANES)),
        full((1, D_INNER)), full((1, D_INNER)), resident((2 * LANES, D_INNER)),
    ]
    args = [x, x, sh, sc, w_in, w_dt, conv0, ssm_all, cw, cb, dtb, alog, dexp, nw,
            jnp.concatenate([sel, sel], axis=0)]
    scratch = [
        pltpu.VMEM((2, 2 * N_XTILES, CONV_PAD + L, LANES), F32),
        pltpu.VMEM((D_STATE, N_HEADS * HEAD_DIM), F32),
        pltpu.VMEM((2, L, D_INNER), F32),
        pltpu.VMEM((2, L, LANES), F32),
    ]
    return _ssd_call(functools.partial(_ssd_chunk_kernel, L=L), (B, nsteps), in_specs, args,
                     pair(D_INNER), (B, T, D_INNER), conv_spec, ssm_all, ssm_prev, j, None, scratch,
                     ("arbitrary", "arbitrary"))


def _ssd_step(zx, dtraw, conv0, ssm_all, ssm_prev, j, cw, cb, dtb, alog, dexp, nw, B):
    T = zx.shape[0]
    nb = SSD_STEP_BATCHES
    full = lambda shape: pl.BlockSpec(shape, lambda b: (0,) * len(shape))
    col = lambda w: pl.BlockSpec((T, nb * w), lambda b: (0, b))
    conv_spec = pl.BlockSpec((nb, CONV_K - 1, CONV_DIM), lambda b: (b, 0, 0))
    in_specs = [
        col(ZX_DIM), col(LANES), conv_spec, _ssm_state_spec(1, j, nb),
        full((CONV_K, CONV_DIM)), full((1, CONV_DIM)), full((1, LANES)), full((1, LANES)),
        full((1, D_INNER)), full((1, D_INNER)),
        full((2 * N_GROUPS * D_STATE, LANES)), full((2 * LANES, D_INNER)),
    ]
    head_ids = jnp.arange(LANES)
    gsum = ((jnp.arange(N_GROUPS * D_STATE)[:, None] // D_STATE == head_ids[None, :] // HEADS_PER_GROUP)
            & (head_ids[None, :] < N_HEADS)).astype(BF16)
    sel = (jnp.arange(D_INNER)[None, :] // HEAD_DIM == head_ids[:, None]).astype(BF16)
    args = [zx, dtraw, conv0, ssm_all, cw, cb, dtb, alog, dexp, nw,
            jnp.concatenate([gsum, gsum], axis=0), jnp.concatenate([sel, sel], axis=0)]
    scratch = [pltpu.VMEM((nb, CONV_PAD + T, CONV_DIM), F32)]
    return _ssd_call(functools.partial(_ssd_step_kernel, L=T, nbatch=nb), (B // nb,), in_specs, args,
                     col(D_INNER), (T, B * D_INNER), conv_spec, ssm_all, ssm_prev, j, nb, scratch,
                     ("arbitrary",))


def _ffn_kernel(*refs, nb, has_proj, final):
    if has_proj:
        x_ref, y_ref, ow_ref, g1_ref, sh_ref, sc_ref, g2_ref, win_ref, wo_ref, fw_ref, o_ref = refs
    else:
        x_ref, sh_ref, sc_ref, g2_ref, win_ref, wo_ref, fw_ref, o_ref = refs
    x = x_ref[...]
    if has_proj:
        x = x + _gate_rows(_dot(y_ref[...], ow_ref[...]), g1_ref[...], nb)
    h = _modulate_rows(x, sh_ref[...], sc_ref[...], nb).astype(BF16)
    tf = D_FF // FFN_SPLIT
    acc = None
    for c in range(FFN_SPLIT):
        gate = _dot(h, win_ref[:, c * tf:(c + 1) * tf])
        up = _dot(h, win_ref[:, D_FF + c * tf:D_FF + (c + 1) * tf])
        part = _dot((_silu(gate) * up).astype(BF16), wo_ref[c * tf:(c + 1) * tf, :])
        acc = part if acc is None else acc + part
    x2 = x + _gate_rows(acc, g2_ref[...], nb)
    if final:
        x2 = x2 * lax.rsqrt(jnp.mean(x2 * x2, axis=-1, keepdims=True) + RMS_EPS) * fw_ref[...]
    o_ref[...] = x2


def _ffn(x3, y3, out_w, g1, sh, sc, g2, w_in, w_out, fw, tm, final):
    ng, rows, _ = x3.shape
    nb = sh.shape[1]
    has_proj = y3 is not None
    row_blk = lambda w: pl.BlockSpec((None, tm, w), lambda g, i: (g, i, 0))
    per_g = pl.BlockSpec((None, nb, D_MODEL), lambda g, i: (g, 0, 0))
    resident = lambda shape: pl.BlockSpec(shape, lambda g, i: (0,) * len(shape), pipeline_mode=pl.Buffered(1))
    in_specs = [row_blk(D_MODEL)]
    args = [x3]
    if has_proj:
        in_specs += [row_blk(D_INNER), resident((D_INNER, D_MODEL)), per_g]
        args += [y3, out_w, g1]
    in_specs += [per_g, per_g, per_g,
                 resident((D_MODEL, 2 * D_FF)), resident((D_FF, D_MODEL)), resident((1, D_MODEL))]
    args += [sh, sc, g2, w_in, w_out, fw]
    return pl.pallas_call(
        functools.partial(_ffn_kernel, nb=nb, has_proj=has_proj, final=final),
        grid=(ng, rows // tm),
        in_specs=in_specs,
        out_specs=row_blk(D_MODEL),
        out_shape=jax.ShapeDtypeStruct((ng, rows, D_MODEL), F32),
        compiler_params=_cparams(("arbitrary", "arbitrary")),
        name="proj_ffn" if has_proj else "ffn",
    )(*args)


def _s5_disc_kernel(are_ref, aim_ref, ldt_ref, bre_ref, bim_ref, abre_ref, abim_ref, bbre_ref, bbim_ref):
    lre, lim = are_ref[...], aim_ref[...]
    dt = jnp.exp(ldt_ref[...])
    mag = jnp.exp(lre * dt)
    ab_re, ab_im = mag * jnp.cos(lim * dt), mag * jnp.sin(lim * dt)
    den = lre * lre + lim * lim
    nr, ni = ab_re - 1.0, ab_im
    q_re = (nr * lre + ni * lim) / den
    q_im = (ni * lre - nr * lim) / den
    abre_ref[...] = ab_re
    abim_ref[...] = ab_im
    br, bi = bre_ref[...], bim_ref[...]
    bbre_ref[...] = q_re[:, None, :] * br - q_im[:, None, :] * bi
    bbim_ref[...] = q_re[:, None, :] * bi + q_im[:, None, :] * br


def _s5_disc(a_re, a_im, log_dt, b_re, b_im):
    gn = jax.ShapeDtypeStruct((S5_GROUPS, S5_STATE), F32)
    gin = jax.ShapeDtypeStruct((S5_GROUPS, S5_GROUP, S5_STATE), F32)
    return pl.pallas_call(
        _s5_disc_kernel,
        out_shape=[gn, gn, gin, gin],
        name="s5_disc",
    )(a_re, a_im, log_dt.reshape(S5_GROUPS, 1), b_re, b_im)


def _s5_kernel(x_ref, sh_ref, sc_ref, g1_ref, bb_ref, cc_ref, are_ref, aim_ref, s0re_ref, s0im_ref,
               dskip_ref, glu_ref, o_ref, sre_ref, sim_ref, xs_re, xs_im, st_re, st_im, *, tb, bb):
    t_idx = pl.program_id(1)
    rows = tb * bb

    @pl.when(t_idx == 0)
    def _():
        st_re[...] = s0re_ref[...]
        st_im[...] = s0im_ref[...]

    x = x_ref[...].reshape(rows, D_MODEL)
    h = _modulate_rows(x, sh_ref[...], sc_ref[...], bb)
    hb = h.astype(BF16)
    half = S5_LANES // S5_BLOCKS
    ys = []
    for k in range(S5_BLOCKS):
        ks = slice(k * half, (k + 1) * half)
        bu = _dot(hb[:, k * LANES:(k + 1) * LANES], bb_ref[k])
        ar, ai = are_ref[:, ks], aim_ref[:, ks]
        for rg in range(bb // SUBLANES):
            rs = slice(rg * SUBLANES, (rg + 1) * SUBLANES)
            xr, xi = st_re[rs, ks], st_im[rs, ks]
            for t in range(tb):
                sl = slice(t * bb + rg * SUBLANES, t * bb + (rg + 1) * SUBLANES)
                xr, xi = (ar * xr - ai * xi + bu[sl, :half], ar * xi + ai * xr + bu[sl, half:])
                xs_re[sl, ks] = xr
                xs_im[sl, ks] = xi
            st_re[rs, ks] = xr
            st_im[rs, ks] = xi
        ys.append(_dot(xs_re[:, ks].astype(BF16), cc_ref[k, :half, :])
                  - _dot(xs_im[:, ks].astype(BF16), cc_ref[k, half:, :]))
    y = jnp.concatenate(ys, axis=1) + dskip_ref[...] * h
    y = 0.5 * y * (1.0 + jnp.tanh(math.sqrt(2.0 / math.pi) * (y + 0.044715 * (y * y * y))))
    yb = y.astype(BF16)
    cw = D_MODEL // S5_GLU_SPLIT
    out = jnp.concatenate(
        [_dot(yb, glu_ref[:, c * cw:(c + 1) * cw])
         * _sigmoid(_dot(yb, glu_ref[:, D_MODEL + c * cw:D_MODEL + (c + 1) * cw]))
         for c in range(S5_GLU_SPLIT)], axis=1)
    o_ref[...] = (x + _gate_rows(out, g1_ref[...], bb)).reshape(tb, bb, D_MODEL)

    @pl.when(t_idx == pl.num_programs(1) - 1)
    def _():
        sre_ref[...] = st_re[...]
        sim_ref[...] = st_im[...]


def _s5_layer(x3, sh, sc, g1, bbd, ccd, a_re, a_im, s0re, s0im, dskip, glu_w, tb, bb):
    T, B, _ = x3.shape
    per_b = pl.BlockSpec((bb, D_MODEL), lambda i, t: (i, 0))
    st = pl.BlockSpec((bb, S5_LANES), lambda i, t: (i, 0))
    full = lambda shape: pl.BlockSpec(shape, lambda i, t: (0,) * len(shape))
    return pl.pallas_call(
        functools.partial(_s5_kernel, tb=tb, bb=bb),
        grid=(B // bb, T // tb),
        in_specs=[
            pl.BlockSpec((tb, bb, D_MODEL), lambda i, t: (t, i, 0)),
            per_b, per_b, per_b,
            full((S5_BLOCKS, LANES, 2 * S5_LANES // S5_BLOCKS)),
            full((S5_BLOCKS, 2 * S5_LANES // S5_BLOCKS, LANES)),
            full((SUBLANES, S5_LANES)),
            full((SUBLANES, S5_LANES)),
            st, st,
            full((1, D_MODEL)),
            full((D_MODEL, 2 * D_MODEL)),
        ],
        out_specs=[pl.BlockSpec((tb, bb, D_MODEL), lambda i, t: (t, i, 0)), st, st],
        out_shape=[
            jax.ShapeDtypeStruct((T, B, D_MODEL), F32),
            jax.ShapeDtypeStruct((B, S5_LANES), F32),
            jax.ShapeDtypeStruct((B, S5_LANES), F32),
        ],
        scratch_shapes=[
            pltpu.VMEM((tb * bb, S5_LANES), F32),
            pltpu.VMEM((tb * bb, S5_LANES), F32),
            pltpu.VMEM((bb, S5_LANES), F32),
            pltpu.VMEM((bb, S5_LANES), F32),
        ],
        compiler_params=_cparams(("arbitrary", "arbitrary")),
        name="s5_layer",
    )(x3, sh, sc, g1, bbd, ccd, a_re, a_im, s0re, s0im, dskip, glu_w)


def _block_diag(w):
    _, r, c = w.shape
    w = w.reshape(S5_BLOCKS, 8, r, c)
    eye = jnp.eye(8, dtype=w.dtype)
    return (w[:, :, :, None, :] * eye[None, :, None, :, None]).reshape(S5_BLOCKS, 8 * r, 8 * c)


def _prep_params(p):
    w = {}
    w["ssd_in_w"] = [p["ssd_in_w"][j, :, :ZX_DIM].astype(BF16) for j in range(2)]
    w["ssd_dt_w"] = [jnp.pad(p["ssd_in_w"][j, :, ZX_DIM:], ((0, 0), (0, LANES - N_HEADS))).astype(BF16)
                     for j in range(2)]
    pad_h = lambda v: jnp.pad(v, (0, LANES - N_HEADS)).reshape(1, LANES)
    w["dt_bias"] = [pad_h(p["ssd_dt_bias"][j]) for j in range(2)]
    w["a_log"] = [pad_h(p["ssd_A_log"][j]) for j in range(2)]
    w["d_exp"] = [jnp.repeat(p["ssd_D"][j], HEAD_DIM).reshape(1, D_INNER) for j in range(2)]
    w["norm_w"] = [p["ssd_norm_w"][j].reshape(1, D_INNER) for j in range(2)]
    w["conv_w"] = [p["ssd_conv_w"][j] for j in range(2)]
    w["conv_b"] = [p["ssd_conv_b"][j].reshape(1, CONV_DIM) for j in range(2)]
    w["ssd_out_w"] = [p["ssd_out_w"][j].astype(BF16) for j in range(2)]
    w["ffn_w_in"] = [p["ffn_w_in"][i].astype(BF16) for i in range(DEPTH)]
    w["ffn_w_out"] = [p["ffn_w_out"][i].astype(BF16) for i in range(DEPTH)]
    w["final_w"] = p["final_norm_w"].reshape(1, D_MODEL)
    w["s5"] = []
    for j in range(2):
        ab_re, ab_im, bb_re, bb_im = _s5_disc(
            p["s5_A_re"][j], p["s5_A_im"][j], p["s5_log_dt"][j],
            jnp.swapaxes(p["s5_B_re"][j], 1, 2), jnp.swapaxes(p["s5_B_im"][j], 1, 2))
        bbd = jnp.concatenate([_block_diag(bb_re), _block_diag(bb_im)], axis=2).astype(BF16)
        c_re = jnp.swapaxes(p["s5_C_re"][j], 1, 2)
        c_im = jnp.swapaxes(p["s5_C_im"][j], 1, 2)
        ccd = jnp.concatenate([_block_diag(c_re), _block_diag(c_im)], axis=1).astype(BF16)
        bro = lambda v: jnp.broadcast_to(v.reshape(1, S5_LANES), (SUBLANES, S5_LANES))
        w["s5"].append(dict(bbd=bbd, ccd=ccd, a_re=bro(ab_re), a_im=bro(ab_im),
                            dskip=p["s5_D"][j].reshape(1, D_MODEL),
                            glu_w=p["s5_glu_w"][j].astype(BF16)))
    return w


def _trunk(x, mods, ssm0, conv0, re0, im0, w, tm, s5_tb, s5_bb, ssd_batch_major):
    B, T, _ = x.shape
    assert ssd_batch_major == (T % (2 * SSD_CHUNK) == 0)
    ssm_new = jnp.zeros(ssm0.shape, ssm0.dtype)
    conv_new, re_new, im_new = [], [], []
    if not ssd_batch_major:
        x = jnp.swapaxes(x, 0, 1)
    for i in range(DEPTH):
        j = i // 2
        final = i == DEPTH - 1
        parts = [mods[i][:, k * D_MODEL:(k + 1) * D_MODEL] for k in range(6)]
        by_rows = [v[None] for v in parts]
        by_batch = [v[:, None] for v in parts]
        ffn_w = (w["ffn_w_in"][i], w["ffn_w_out"][i], w["final_w"], tm, final)
        if i % 2 == 0:
            sh1, sc1, g1, sh2, sc2, g2 = by_batch if ssd_batch_major else by_rows
            x3 = x if ssd_batch_major else x.reshape(1, T * B, D_MODEL)
            ssd_w = (w["conv_w"][j], w["conv_b"][j], w["dt_bias"][j], w["a_log"][j], w["d_exp"][j],
                     w["norm_w"][j])
            if ssd_batch_major:
                y, ssm_new, cv_new = _ssd_chunked(x3, sh1, sc1, w["ssd_in_w"][j], w["ssd_dt_w"][j],
                                                  conv0[j], ssm0, ssm_new, j, *ssd_w)
                x = jnp.swapaxes(_ffn(x3, y, w["ssd_out_w"][j], g1, sh2, sc2, g2, *ffn_w), 0, 1)
            else:
                zx, dtraw = _inproj(x3, sh1, sc1, w["ssd_in_w"][j], w["ssd_dt_w"][j], tm)
                y, ssm_new, cv_new = _ssd_step(zx.reshape(T, B * ZX_DIM), dtraw[0].reshape(T, B * LANES),
                                               conv0[j], ssm0, ssm_new, j, *ssd_w, B)
                x = _ffn(x3, y.reshape(1, T * B, D_INNER), w["ssd_out_w"][j], g1, sh2, sc2, g2,
                         *ffn_w).reshape(T, B, D_MODEL)
            conv_new.append(cv_new)
        else:
            s5 = w["s5"][j]
            sh1, sc1, g1 = [v[0] for v in by_rows[:3]]
            xs5, sr, si = _s5_layer(x, sh1, sc1, g1, s5["bbd"], s5["ccd"], s5["a_re"], s5["a_im"],
                                    re0[j], im0[j], s5["dskip"], s5["glu_w"], s5_tb, s5_bb)
            re_new.append(sr)
            im_new.append(si)
            x = _ffn(xs5.reshape(1, T * B, D_MODEL), None, None, None, *by_rows[3:],
                     *ffn_w).reshape(T, B, D_MODEL)
            if ssd_batch_major:
                x = jnp.swapaxes(x, 0, 1)
    if not ssd_batch_major:
        x = jnp.swapaxes(x, 0, 1)
    return x, ssm_new, jnp.stack(conv_new), jnp.stack(re_new), jnp.stack(im_new)


def kernel(x_prompt, x_sample, state_ssm, state_conv, state_s5_re, state_s5_im, c_prompt, c_sample,
           ada_w, ada_b, ssd_in_w, ssd_conv_w, ssd_conv_b, ssd_dt_bias, ssd_A_log, ssd_D, ssd_norm_w,
           ssd_out_w, s5_A_re, s5_A_im, s5_log_dt, s5_B_re, s5_B_im, s5_C_re, s5_C_im, s5_D, s5_glu_w,
           ffn_w_in, ffn_w_out, final_norm_w):
    p = dict(ssd_in_w=ssd_in_w, ssd_conv_w=ssd_conv_w, ssd_conv_b=ssd_conv_b, ssd_dt_bias=ssd_dt_bias,
             ssd_A_log=ssd_A_log, ssd_D=ssd_D, ssd_norm_w=ssd_norm_w, ssd_out_w=ssd_out_w,
             s5_A_re=s5_A_re, s5_A_im=s5_A_im, s5_log_dt=s5_log_dt, s5_B_re=s5_B_re, s5_B_im=s5_B_im,
             s5_C_re=s5_C_re, s5_C_im=s5_C_im, s5_D=s5_D, s5_glu_w=s5_glu_w,
             ffn_w_in=ffn_w_in, ffn_w_out=ffn_w_out, final_norm_w=final_norm_w)
    w = _prep_params(p)
    bp, tp, _ = x_prompt.shape
    bs, ts, _ = x_sample.shape

    mods = _ada(jnp.concatenate([c_prompt, c_sample], axis=0), ada_w, ada_b)
    mods_p = [mods[l, :bp] for l in range(DEPTH)]
    mods_s = [mods[l, bp:] for l in range(DEPTH)]

    n_ssd, n_s5 = state_ssm.shape[0], state_s5_re.shape[0]
    rows_hp = N_HEADS * HEAD_DIM
    zeros_p = (jnp.zeros((n_ssd, bp, rows_hp, D_STATE), state_ssm.dtype),
               jnp.zeros((n_ssd, bp, CONV_K - 1, CONV_DIM), state_conv.dtype),
               jnp.zeros((n_s5, bp, S5_LANES), state_s5_re.dtype),
               jnp.zeros((n_s5, bp, S5_LANES), state_s5_im.dtype))
    yp, ssm_p, conv_p, re_p, im_p = _trunk(
        x_prompt, mods_p, *zeros_p, w,
        tm=512, s5_tb=512 // bp, s5_bb=bp, ssd_batch_major=True)
    ys, ssm_s, conv_s, re_s, im_s = _trunk(
        x_sample, mods_s,
        state_ssm.reshape(n_ssd, bs, rows_hp, D_STATE), state_conv,
        state_s5_re.reshape(n_s5, bs, S5_LANES), state_s5_im.reshape(n_s5, bs, S5_LANES), w,
        tm=512, s5_tb=ts, s5_bb=512 // ts, ssd_batch_major=False)

    ssm_shape = lambda b: (n_ssd, b, N_HEADS, HEAD_DIM, D_STATE)
    s5_shape = lambda b: (n_s5, b, S5_GROUPS, S5_STATE)
    return (yp, ys,
            ssm_p.reshape(ssm_shape(bp)), conv_p, re_p.reshape(s5_shape(bp)), im_p.reshape(s5_shape(bp)),
            ssm_s.reshape(ssm_shape(bs)), conv_s, re_s.reshape(s5_shape(bs)), im_s.reshape(s5_shape(bs)))
```

```python
import functools
import math

import jax
import jax.numpy as jnp
from jax import lax
from jax.experimental import pallas as pl
from jax.experimental.pallas import tpu as pltpu

F32 = jnp.float32
BF16 = jnp.bfloat16

D_MODEL = 1024
DEPTH = 4
D_INNER = 2048
HEAD_DIM = 64
N_HEADS = 32
N_GROUPS = 8
HEADS_PER_GROUP = 4
D_STATE = 128
CONV_K = 4
BC_DIM = 2 * N_GROUPS * D_STATE
CONV_DIM = D_INNER + BC_DIM
ZX_DIM = D_INNER + CONV_DIM
SSD_CHUNK = 128
S5_GROUPS = 64
S5_GROUP = 16
S5_STATE = 64
S5_LANES = S5_GROUPS * S5_STATE
S5_BLOCKS = D_MODEL // 128
D_FF = 2816
RMS_EPS = 1e-6
LANES = 128
SUBLANES = 8
CONV_PAD = 8
N_XTILES = D_INNER // LANES
GROUP_W = HEADS_PER_GROUP * HEAD_DIM
SSD_STEP_BATCHES = 4
FFN_SPLIT = 11
S5_GLU_SPLIT = 4
S5_BU_SPLIT = 4
INPROJ_COLS = 512
VMEM_LIMIT = 52 * 1024 * 1024


def _cparams(sem):
    return pltpu.CompilerParams(dimension_semantics=sem, vmem_limit_bytes=VMEM_LIMIT)


def _sigmoid(x):
    return 1.0 / (1.0 + jnp.exp(-x))


def _silu(x):
    return x * _sigmoid(x)


def _rows_by_batch(x, nb):
    tm, d = x.shape
    return x.reshape(tm // nb, nb, d)


def _modulate_rows(x, sh, sc, nb):
    if nb == 1:
        r = lax.rsqrt(jnp.mean(x * x, axis=-1, keepdims=True) + RMS_EPS)
        return (x * r) * (1.0 + sc) + sh
    x3 = _rows_by_batch(x, nb)
    r = lax.rsqrt(jnp.mean(x3 * x3, axis=-1, keepdims=True) + RMS_EPS)
    h = (x3 * r) * (1.0 + sc[None]) + sh[None]
    return h.reshape(x.shape)


def _gate_rows(v, g, nb):
    if nb == 1:
        return v * g
    return (_rows_by_batch(v, nb) * g[None]).reshape(v.shape)


def _dot(a, b):
    return jnp.dot(a, b, preferred_element_type=F32)


def _dot_nt(a, b):
    return lax.dot_general(a, b, (((1,), (1,)), ((), ())), preferred_element_type=F32)


def _dot_tn(a, b):
    return lax.dot_general(a, b, (((0,), (0,)), ((), ())), preferred_element_type=F32)


def _ada_kernel(c_ref, w_ref, b_ref, o_ref):
    s = _silu(c_ref[...]).astype(BF16)
    o_ref[0] = _dot(s, w_ref[0].astype(BF16)) + b_ref[0]


def _ada(c_all, ada_w, ada_b):
    nrow = c_all.shape[0]
    tn = 1536
    return pl.pallas_call(
        _ada_kernel,
        grid=(DEPTH, 6 * D_MODEL // tn),
        in_specs=[
            pl.BlockSpec((nrow, D_MODEL), lambda l, j: (0, 0)),
            pl.BlockSpec((1, D_MODEL, tn), lambda l, j: (l, 0, j)),
            pl.BlockSpec((1, 1, tn), lambda l, j: (l, 0, j)),
        ],
        out_specs=pl.BlockSpec((1, nrow, tn), lambda l, j: (l, 0, j)),
        out_shape=jax.ShapeDtypeStruct((DEPTH, nrow, 6 * D_MODEL), F32),
        compiler_params=_cparams(("arbitrary", "arbitrary")),
        name="ada",
    )(c_all, ada_w, ada_b.reshape(DEPTH, 1, 6 * D_MODEL))


def _inproj_kernel(x_ref, sh_ref, sc_ref, w_ref, wdt_ref, zx_ref, dt_ref, *, nb):
    h = _modulate_rows(x_ref[...], sh_ref[...], sc_ref[...], nb).astype(BF16)
    zx_ref[...] = _dot(h, w_ref[...])
    dt_ref[...] = _dot(h, wdt_ref[...])


def _inproj(x3, sh, sc, w, wdt, tm):
    ng, rows, _ = x3.shape
    nb = sh.shape[1]
    tn = 2048
    per_g = pl.BlockSpec((None, nb, D_MODEL), lambda j, g, i: (g, 0, 0))
    return pl.pallas_call(
        functools.partial(_inproj_kernel, nb=nb),
        grid=(ZX_DIM // tn, ng, rows // tm),
        in_specs=[
            pl.BlockSpec((None, tm, D_MODEL), lambda j, g, i: (g, i, 0)),
            per_g, per_g,
            pl.BlockSpec((D_MODEL, tn), lambda j, g, i: (0, j)),
            pl.BlockSpec((D_MODEL, LANES), lambda j, g, i: (0, 0)),
        ],
        out_specs=[
            pl.BlockSpec((None, tm, tn), lambda j, g, i: (g, i, j)),
            pl.BlockSpec((None, None, tm, LANES), lambda j, g, i: (j, g, i, 0)),
        ],
        out_shape=[
            jax.ShapeDtypeStruct((ng, rows, ZX_DIM), F32),
            jax.ShapeDtypeStruct((ZX_DIM // tn, ng, rows, LANES), F32),
        ],
        compiler_params=_cparams(("arbitrary", "arbitrary", "arbitrary")),
        name="ssd_inproj",
    )(x3, sh, sc, w, wdt)


def _cumsum_rows(a, L):
    rows = lax.broadcasted_iota(jnp.int32, a.shape, 0)
    k = 1
    while k < L:
        a = a + jnp.where(rows >= k, pltpu.roll(a, k, 0), 0.0)
        k *= 2
    return a


def _transpose_rows(a, L):
    if L < LANES:
        a = jnp.concatenate([a, jnp.zeros((LANES - L, LANES), F32)], axis=0)
    return a.T[:, :L]


def _ssd_step_kernel(zx_ref, dt_ref, conv0_ref, ssm0_ref,
                     cw_ref, cb_ref, dtb_ref, alog_ref, dexp_ref, nw_ref, gsum_ref, expand_ref,
                     *rest, L, nbatch):
    y_ref, ssm_out_ref, conv_out_ref, pad, xbc_scr = rest[-5:]
    lo = CONV_PAD - (CONV_K - 1)
    row_t = lax.broadcasted_iota(jnp.int32, (L, LANES), 0)
    b_lo, c_lo = D_INNER, D_INNER + N_GROUPS * D_STATE

    def row_bcast(i, s, lt0, lt1):
        return jnp.concatenate([jnp.broadcast_to(xbc_scr[i, lt, s:s + 1, :], (L, LANES))
                                for lt in range(lt0, lt1)], axis=1)

    xbcs, dts, acss, prods = [], [], [], []
    for i in range(nbatch):
        z0 = i * ZX_DIM
        conv_tiles = []
        for lt in range(2 * N_XTILES):
            cols = slice(lt * LANES, (lt + 1) * LANES)
            pad[i, lt, lo:CONV_PAD, :] = conv0_ref[i, :, cols]
            pad[i, lt, CONV_PAD:CONV_PAD + L, :] = zx_ref[:, z0 + D_INNER + lt * LANES:z0 + D_INNER + (lt + 1) * LANES]
            acc = cb_ref[:, cols]
            for k in range(CONV_K):
                acc = acc + cw_ref[k:k + 1, cols] * pad[i, lt, lo + k:lo + k + L, :]
            conv_tiles.append(_silu(acc))
            xbc_scr[i, lt] = conv_tiles[lt]
            conv_out_ref[i, :, cols] = pad[i, lt, lo + L:CONV_PAD + L, :]
        xbc = jnp.concatenate(conv_tiles, axis=1)
        dtr = dt_ref[:, i * LANES:(i + 1) * LANES] + dtb_ref[...]
        dt = jnp.maximum(dtr, 0.0) + jnp.log1p(jnp.exp(-jnp.abs(dtr)))
        xbcs.append(xbc)
        dts.append(dt)
        acss.append(_cumsum_rows(dt * (-jnp.exp(alog_ref[...])), L))
        cm = xbc[:, c_lo:]
        prods += [cm * row_bcast(i, s, b_lo // LANES, c_lo // LANES) for s in range(L)]

    p_hi, p_lo = _split_bf16(jnp.concatenate(prods, axis=0))
    cbh = _dot(jnp.concatenate([p_hi, p_lo], axis=1).astype(BF16), gsum_ref[...])

    per = L * L + 2 * L
    rows = []
    for i in range(nbatch):
        dt, acs = dts[i], acss[i]
        for s in range(L):
            decay = jnp.exp(jnp.where(row_t >= s, acs - acs[s:s + 1, :], -jnp.inf))
            r0 = (i * L + s) * L
            rows.append(cbh[r0:r0 + L, :] * decay * dt[s:s + 1, :])
        rows += [jnp.exp(acs), jnp.exp(acs[L - 1:L, :] - acs) * dt]
    w_hi, w_lo = _split_bf16(jnp.concatenate(rows, axis=0))
    wide = _dot(jnp.concatenate([w_hi, w_lo], axis=1).astype(BF16), expand_ref[...])

    for i in range(nbatch):
        z0, w0 = i * ZX_DIM, i * per
        xbc = xbcs[i]
        xs = xbc[:, :D_INNER]
        y = dexp_ref[...] * xs
        for s in range(L):
            y = y + wide[w0 + s * L:w0 + (s + 1) * L, :] * row_bcast(i, s, 0, N_XTILES)
        ea_x = wide[w0 + L * L:w0 + L * L + L, :]
        xw = xs * wide[w0 + L * L + L:w0 + per, :]
        e_last = jnp.exp(acss[i][L - 1:L, :])
        for g in range(N_GROUPS):
            gl = slice(g * GROUP_W, (g + 1) * GROUP_W)
            bg = xbc[:, b_lo + g * D_STATE:b_lo + (g + 1) * D_STATE].astype(BF16)
            cg = xbc[:, c_lo + g * D_STATE:c_lo + (g + 1) * D_STATE].astype(BF16)
            sg = ssm0_ref[i, gl, :]
            yg = y[:, gl] + ea_x[:, gl] * _dot_nt(cg, sg.astype(BF16))
            upd = _dot_tn(xw[:, gl].astype(BF16), bg)
            for r in range(HEADS_PER_GROUP):
                h = g * HEADS_PER_GROUP + r
                rr = slice(r * HEAD_DIM, (r + 1) * HEAD_DIM)
                ssm_out_ref[i, g * GROUP_W + r * HEAD_DIM:g * GROUP_W + (r + 1) * HEAD_DIM, :] = (
                    e_last[:, h:h + 1] * sg[rr, :] + upd[rr, :])
            yg = yg * _silu(zx_ref[:, z0 + g * GROUP_W:z0 + (g + 1) * GROUP_W])
            yg = yg * lax.rsqrt(jnp.mean(yg * yg, axis=-1, keepdims=True) + RMS_EPS)
            y0 = i * D_INNER + g * GROUP_W
            y_ref[:, y0:y0 + GROUP_W] = (yg * nw_ref[:, gl]).astype(y_ref.dtype)


def _split_bf16(v):
    hi = v.astype(BF16).astype(F32)
    return hi, (v - hi).astype(BF16).astype(F32)


def _ssd_chunk_kernel(x_ref, xn_ref, sh_ref, sc_ref, win_ref, wdt_ref, conv0_ref, ssm0_ref,
                      cw_ref, cb_ref, dtb_ref, alog_ref, dexp_ref, nw_ref, expand_ref, *rest, L):
    y_ref, ssm_out_ref, conv_out_ref, pad, state_t, z_scr, dt_scr = rest[-7:]
    c = pl.program_id(1)
    lo = CONV_PAD - (CONV_K - 1)

    def projection_steps(x_rows, slot):
        h = _modulate_rows(x_rows, sh_ref[...], sc_ref[...], 1).astype(BF16)
        pw = INPROJ_COLS

        def z_step(q):
            z_scr[slot, :, q * pw:(q + 1) * pw] = _dot(h, win_ref[:, q * pw:(q + 1) * pw])

        def xbc_step(q):
            res = _dot(h, win_ref[:, D_INNER + q * pw:D_INNER + (q + 1) * pw])
            for r in range(pw // LANES):
                pad[slot, q * (pw // LANES) + r, CONV_PAD:CONV_PAD + L, :] = res[:, r * LANES:(r + 1) * LANES]

        def dt_step():
            dt_scr[slot] = _dot(h, wdt_ref[...])

        return ([functools.partial(xbc_step, q) for q in range(CONV_DIM // pw)]
                + [functools.partial(z_step, q) for q in range(D_INNER // pw)] + [dt_step])

    @pl.when(c == 0)
    def _():
        state_t[...] = ssm0_ref[...].T
        for lt in range(2 * N_XTILES):
            pad[0, lt, lo:CONV_PAD, :] = conv0_ref[0, :, lt * LANES:(lt + 1) * LANES]
        for step in projection_steps(x_ref[:L, :], 0):
            step()

    body = functools.partial(_ssd_chunk_body, z_scr=z_scr, dt_scr=dt_scr, pad=pad, state_t=state_t,
                             cw_ref=cw_ref, cb_ref=cb_ref, dtb_ref=dtb_ref, alog_ref=alog_ref,
                             dexp_ref=dexp_ref, nw_ref=nw_ref, expand_ref=expand_ref, y_ref=y_ref, L=L)
    body(0, 1, slice(0, L), projection_steps(x_ref[L:, :], 1))
    body(1, 0, slice(L, 2 * L), projection_steps(xn_ref[...], 0))
    for lt in range(2 * N_XTILES):
        conv_out_ref[0, :, lt * LANES:(lt + 1) * LANES] = pad[0, lt, lo:CONV_PAD, :]

    @pl.when(c == pl.num_programs(1) - 1)
    def _():
        ssm_out_ref[...] = state_t[...].T


def _ssd_chunk_body(slot, next_slot, rows, side_steps, *, z_scr, dt_scr, pad, state_t, cw_ref, cb_ref,
                    dtb_ref, alog_ref, dexp_ref, nw_ref, expand_ref, y_ref, L):
    lo = CONV_PAD - (CONV_K - 1)
    n_points = 2 * N_XTILES // 4 + N_GROUPS
    run_at = {(k * n_points) // len(side_steps): step for k, step in enumerate(side_steps)}
    assert len(run_at) == len(side_steps)
    point = [0]

    def side_point():
        step = run_at.get(point[0])
        point[0] += 1
        if step is not None:
            step()

    def conv_tile(lt):
        cols = slice(lt * LANES, (lt + 1) * LANES)
        acc = cb_ref[:, cols]
        for k in range(CONV_K):
            acc = acc + cw_ref[k:k + 1, cols] * pad[slot, lt, lo + k:lo + k + L, :]
        return _silu(acc)

    tiles = []
    for lt in range(2 * N_XTILES):
        tiles.append(conv_tile(lt))
        if lt % 4 == 3:
            side_point()

    for lt in range(2 * N_XTILES):
        pad[next_slot, lt, lo:CONV_PAD, :] = pad[slot, lt, lo + L:CONV_PAD + L, :]

    dtr = dt_scr[slot] + dtb_ref[...]
    dt = jnp.maximum(dtr, 0.0) + jnp.log1p(jnp.exp(-jnp.abs(dtr)))
    a = dt * (-jnp.exp(alog_ref[...]))
    acs = _cumsum_rows(a, L)
    acs_t = acs.T
    dt_t = dt.T
    a_last = acs[L - 1:L, :]
    ea_hi, ea_lo = _split_bf16(jnp.exp(acs))
    we_hi, we_lo = _split_bf16(jnp.exp(a_last - acs) * dt)
    parts = jnp.concatenate([jnp.concatenate([ea_hi, ea_lo], axis=1),
                             jnp.concatenate([we_hi, we_lo], axis=1)], axis=0).astype(BF16)
    wide = _dot(parts, expand_ref[...])
    ea_x = wide[:L]
    we_x = wide[L:]

    causal = (lax.broadcasted_iota(jnp.int32, (L, L), 0)
              >= lax.broadcasted_iota(jnp.int32, (L, L), 1))
    lane_head = lax.broadcasted_iota(jnp.int32, (L, GROUP_W), 1) // HEAD_DIM

    for g in range(N_GROUPS):
        gl = slice(g * GROUP_W, (g + 1) * GROUP_W)
        bg = tiles[N_XTILES + g].astype(BF16)
        cg = tiles[N_XTILES + N_GROUPS + g].astype(BF16)
        cb = _dot_nt(cg, bg)
        sg = state_t[:, gl]
        y_off = _dot(cg, sg.astype(BF16))
        ms = []
        for r in range(HEADS_PER_GROUP):
            h = g * HEADS_PER_GROUP + r
            seg = acs[:, h:h + 1] - acs_t[h:h + 1, :]
            decay = jnp.exp(jnp.where(causal, seg, -jnp.inf))
            ms.append((cb * decay * dt_t[h:h + 1, :]).astype(BF16))
        xg = jnp.concatenate([tiles[2 * g], tiles[2 * g + 1]], axis=1)
        xgb = xg.astype(BF16)
        xblk = jnp.concatenate([jnp.where(lane_head == r, xgb, jnp.zeros_like(xgb))
                                for r in range(HEADS_PER_GROUP)], axis=0)
        y_diag = _dot(jnp.concatenate(ms, axis=1), xblk)
        yg = y_diag + ea_x[:, gl] * y_off + dexp_ref[:, gl] * xg
        upd = _dot_tn(bg, (xg * we_x[:, gl]).astype(BF16))
        state_t[:, gl] = ea_x[L - 1:L, gl] * sg + upd
        yg = yg * _silu(z_scr[slot, :, gl])
        yg = yg * lax.rsqrt(jnp.mean(yg * yg, axis=-1, keepdims=True) + RMS_EPS)
        y_ref[rows, gl] = (yg * nw_ref[:, gl]).astype(y_ref.dtype)
        side_point()


def _ssm_state_spec(grid_rank, j, nb):
    shape = (None, nb, N_HEADS * HEAD_DIM, D_STATE)
    if grid_rank == 2:
        return pl.BlockSpec(shape, lambda b, c: (j, b, 0, 0))
    return pl.BlockSpec(shape, lambda b: (j, b, 0, 0))


def _ssd_call(body, grid, in_specs, args, y_spec, y_shape, conv_spec, ssm_all, ssm_prev, j, nb, scratch, sem):
    nbatch = ssm_all.shape[1]
    in_specs = list(in_specs) + [pl.BlockSpec(memory_space=pl.ANY)]
    args = list(args) + [ssm_prev]
    aliases = {len(args) - 1: 1}
    return pl.pallas_call(
        body,
        grid=grid,
        in_specs=in_specs,
        out_specs=[y_spec, _ssm_state_spec(len(grid), j, nb), conv_spec],
        out_shape=[
            jax.ShapeDtypeStruct(y_shape, BF16),
            jax.ShapeDtypeStruct(ssm_all.shape, F32),
            jax.ShapeDtypeStruct((nbatch, CONV_K - 1, CONV_DIM), F32),
        ],
        scratch_shapes=scratch,
        input_output_aliases=aliases,
        compiler_params=_cparams(sem),
        name="ssd_core",
    )(*args)


def _ssd_chunked(x, sh, sc, w_in, w_dt, conv0, ssm_all, ssm_prev, j, cw, cb, dtb, alog, dexp, nw):
    B, T, _ = x.shape
    L = SSD_CHUNK
    nsteps = T // (2 * L)
    last_chunk = T // L - 1
    full = lambda shape: pl.BlockSpec(shape, lambda b, c: (0,) * len(shape))
    resident = lambda shape: pl.BlockSpec(shape, lambda b, c: (0,) * len(shape), pipeline_mode=pl.Buffered(1))
    pair = lambda w: pl.BlockSpec((None, 2 * L, w), lambda b, c: (b, c, 0))
    nxt = pl.BlockSpec((None, L, D_MODEL), lambda b, c: (b, jnp.minimum(2 * c + 2, last_chunk), 0))
    per_b = pl.BlockSpec((None, 1, D_MODEL), lambda b, c: (b, 0, 0))
    conv_spec = pl.BlockSpec((1, CONV_K - 1, CONV_DIM), lambda b, c: (b, 0, 0))
    sel = (jnp.arange(D_INNER)[None, :] // HEAD_DIM == jnp.arange(LANES)[:, None]).astype(BF16)
    in_specs = [
        pair(D_MODEL), nxt, per_b, per_b, resident((D_MODEL, ZX_DIM)), resident((D_MODEL, LANES)),
        conv_spec, _ssm_state_spec(2, j, None),
        full((CONV_K, CONV_DIM)), full((1, CONV_DIM)), full((1, LANES)), full((1, LANES)),
        full((1, D_INNER)), full((1, D_INNER)), resident((2 * LANES, D_INNER)),
    ]
    args = [x, x, sh, sc, w_in, w_dt, conv0, ssm_all, cw, cb, dtb, alog, dexp, nw,
            jnp.concatenate([sel, sel], axis=0)]
    scratch = [
        pltpu.VMEM((2, 2 * N_XTILES, CONV_PAD + L, LANES), F32),
        pltpu.VMEM((D_STATE, N_HEADS * HEAD_DIM), F32),
        pltpu.VMEM((2, L, D_INNER), F32),
        pltpu.VMEM((2, L, LANES), F32),
    ]
    return _ssd_call(functools.partial(_ssd_chunk_kernel, L=L), (B, nsteps), in_specs, args,
                     pair(D_INNER), (B, T, D_INNER), conv_spec, ssm_all, ssm_prev, j, None, scratch,
                     ("arbitrary", "arbitrary"))


def _ssd_step(zx, dtraw, conv0, ssm_all, ssm_prev, j, cw, cb, dtb, alog, dexp, nw, B):
    T = zx.shape[0]
    nb = SSD_STEP_BATCHES
    full = lambda shape: pl.BlockSpec(shape, lambda b: (0,) * len(shape))
    col = lambda w: pl.BlockSpec((T, nb * w), lambda b: (0, b))
    conv_spec = pl.BlockSpec((nb, CONV_K - 1, CONV_DIM), lambda b: (b, 0, 0))
    in_specs = [
        col(ZX_DIM), col(LANES), conv_spec, _ssm_state_spec(1, j, nb),
        full((CONV_K, CONV_DIM)), full((1, CONV_DIM)), full((1, LANES)), full((1, LANES)),
        full((1, D_INNER)), full((1, D_INNER)),
        full((2 * N_GROUPS * D_STATE, LANES)), full((2 * LANES, D_INNER)),
    ]
    head_ids = jnp.arange(LANES)
    gsum = ((jnp.arange(N_GROUPS * D_STATE)[:, None] // D_STATE == head_ids[None, :] // HEADS_PER_GROUP)
            & (head_ids[None, :] < N_HEADS)).astype(BF16)
    sel = (jnp.arange(D_INNER)[None, :] // HEAD_DIM == head_ids[:, None]).astype(BF16)
    args = [zx, dtraw, conv0, ssm_all, cw, cb, dtb, alog, dexp, nw,
            jnp.concatenate([gsum, gsum], axis=0), jnp.concatenate([sel, sel], axis=0)]
    scratch = [pltpu.VMEM((nb, 2 * N_XTILES, CONV_PAD + T, LANES), F32),
               pltpu.VMEM((nb, 2 * N_XTILES, T, LANES), F32)]
    return _ssd_call(functools.partial(_ssd_step_kernel, L=T, nbatch=nb), (B // nb,), in_specs, args,
                     col(D_INNER), (T, B * D_INNER), conv_spec, ssm_all, ssm_prev, j, nb, scratch,
                     ("arbitrary",))


def _ffn_kernel(*refs, nb, has_proj, final):
    if has_proj:
        x_ref, y_ref, ow_ref, g1_ref, sh_ref, sc_ref, g2_ref, win_ref, wo_ref, fw_ref, o_ref = refs
    else:
        x_ref, sh_ref, sc_ref, g2_ref, win_ref, wo_ref, fw_ref, o_ref = refs
    x = x_ref[...]
    if has_proj:
        x = x + _gate_rows(_dot(y_ref[...], ow_ref[...]), g1_ref[...], nb)
    h = _modulate_rows(x, sh_ref[...], sc_ref[...], nb).astype(BF16)
    tf = D_FF // FFN_SPLIT
    acc = None
    for c in range(FFN_SPLIT):
        gate = _dot(h, win_ref[:, c * tf:(c + 1) * tf])
        up = _dot(h, win_ref[:, D_FF + c * tf:D_FF + (c + 1) * tf])
        part = _dot((_silu(gate) * up).astype(BF16), wo_ref[c * tf:(c + 1) * tf, :])
        acc = part if acc is None else acc + part
    x2 = x + _gate_rows(acc, g2_ref[...], nb)
    if final:
        x2 = x2 * lax.rsqrt(jnp.mean(x2 * x2, axis=-1, keepdims=True) + RMS_EPS) * fw_ref[...]
    o_ref[...] = x2


def _ffn(x3, y3, out_w, g1, sh, sc, g2, w_in, w_out, fw, tm, final):
    ng, rows, _ = x3.shape
    nb = sh.shape[1]
    has_proj = y3 is not None
    row_blk = lambda w: pl.BlockSpec((None, tm, w), lambda g, i: (g, i, 0))
    per_g = pl.BlockSpec((None, nb, D_MODEL), lambda g, i: (g, 0, 0))
    resident = lambda shape: pl.BlockSpec(shape, lambda g, i: (0,) * len(shape), pipeline_mode=pl.Buffered(1))
    in_specs = [row_blk(D_MODEL)]
    args = [x3]
    if has_proj:
        in_specs += [row_blk(D_INNER), resident((D_INNER, D_MODEL)), per_g]
        args += [y3, out_w, g1]
    in_specs += [per_g, per_g, per_g,
                 resident((D_MODEL, 2 * D_FF)), resident((D_FF, D_MODEL)), resident((1, D_MODEL))]
    args += [sh, sc, g2, w_in, w_out, fw]
    return pl.pallas_call(
        functools.partial(_ffn_kernel, nb=nb, has_proj=has_proj, final=final),
        grid=(ng, rows // tm),
        in_specs=in_specs,
        out_specs=row_blk(D_MODEL),
        out_shape=jax.ShapeDtypeStruct((ng, rows, D_MODEL), F32),
        compiler_params=_cparams(("arbitrary", "arbitrary")),
        name="proj_ffn" if has_proj else "ffn",
    )(*args)


def _s5_disc_kernel(are_ref, aim_ref, ldt_ref, bre_ref, bim_ref, abre_ref, abim_ref, bbre_ref, bbim_ref):
    lre, lim = are_ref[...], aim_ref[...]
    dt = jnp.exp(ldt_ref[...])
    mag = jnp.exp(lre * dt)
    ab_re, ab_im = mag * jnp.cos(lim * dt), mag * jnp.sin(lim * dt)
    den = lre * lre + lim * lim
    nr, ni = ab_re - 1.0, ab_im
    q_re = (nr * lre + ni * lim) / den
    q_im = (ni * lre - nr * lim) / den
    abre_ref[...] = ab_re
    abim_ref[...] = ab_im
    br, bi = bre_ref[...], bim_ref[...]
    bbre_ref[...] = q_re[:, None, :] * br - q_im[:, None, :] * bi
    bbim_ref[...] = q_re[:, None, :] * bi + q_im[:, None, :] * br


def _s5_disc(a_re, a_im, log_dt, b_re, b_im):
    gn = jax.ShapeDtypeStruct((S5_GROUPS, S5_STATE), F32)
    gin = jax.ShapeDtypeStruct((S5_GROUPS, S5_GROUP, S5_STATE), F32)
    return pl.pallas_call(
        _s5_disc_kernel,
        out_shape=[gn, gn, gin, gin],
        name="s5_disc",
    )(a_re, a_im, log_dt.reshape(S5_GROUPS, 1), b_re, b_im)


def _s5_kernel(x_ref, sh_ref, sc_ref, g1_ref, bb_ref, cc_ref, are_ref, aim_ref, s0re_ref, s0im_ref,
               dskip_ref, glu_ref, o_ref, sre_ref, sim_ref, xs_re, xs_im, st_re, st_im, *, tb, bb):
    t_idx = pl.program_id(1)
    rows = tb * bb

    @pl.when(t_idx == 0)
    def _():
        st_re[...] = s0re_ref[...]
        st_im[...] = s0im_ref[...]

    x = x_ref[...].reshape(rows, D_MODEL)
    h = _modulate_rows(x, sh_ref[...], sc_ref[...], bb)
    hb = h.astype(BF16)
    half = S5_LANES // S5_BLOCKS
    pw = 2 * half // S5_BU_SPLIT

    def bu_pieces(k):
        out = [None] * S5_BU_SPLIT

        def piece(j):
            out[j] = _dot(hb[:, k * LANES:(k + 1) * LANES], bb_ref[k, :, j * pw:(j + 1) * pw])

        return out, [functools.partial(piece, j) for j in range(S5_BU_SPLIT)]

    def y_pieces(k):
        out = [None] * 2

        def piece(j):
            src = xs_im if j else xs_re
            out[j] = _dot(src[:, k * half:(k + 1) * half].astype(BF16), cc_ref[k, j * half:(j + 1) * half, :])

        return out, [functools.partial(piece, j) for j in range(2)]

    n_iter = (bb // SUBLANES) * tb
    bu_next, steps = bu_pieces(0)
    for step in steps:
        step()
    y_parts = [None] * S5_BLOCKS
    for k in range(S5_BLOCKS):
        ks = slice(k * half, (k + 1) * half)
        bu = jnp.concatenate(bu_next, axis=1)
        side = []
        if k + 1 < S5_BLOCKS:
            bu_next, steps = bu_pieces(k + 1)
            side += steps
        if k > 0:
            y_parts[k - 1], steps = y_pieces(k - 1)
            side += steps
        run_at = {((j + 1) * n_iter) // (len(side) + 1): step for j, step in enumerate(side)}
        ar, ai = are_ref[:, ks], aim_ref[:, ks]
        it = 0
        for rg in range(bb // SUBLANES):
            rs = slice(rg * SUBLANES, (rg + 1) * SUBLANES)
            xr, xi = st_re[rs, ks], st_im[rs, ks]
            for t in range(tb):
                sl = slice(t * bb + rg * SUBLANES, t * bb + (rg + 1) * SUBLANES)
                xr, xi = (ar * xr - ai * xi + bu[sl, :half], ar * xi + ai * xr + bu[sl, half:])
                xs_re[sl, ks] = xr
                xs_im[sl, ks] = xi
                it += 1
                if it in run_at:
                    run_at[it]()
            st_re[rs, ks] = xr
            st_im[rs, ks] = xi
    y_parts[S5_BLOCKS - 1], steps = y_pieces(S5_BLOCKS - 1)
    for step in steps:
        step()
    y = jnp.concatenate([p[0] - p[1] for p in y_parts], axis=1) + dskip_ref[...] * h
    y = 0.5 * y * (1.0 + jnp.tanh(math.sqrt(2.0 / math.pi) * (y + 0.044715 * (y * y * y))))
    yb = y.astype(BF16)
    cw = D_MODEL // S5_GLU_SPLIT
    out = jnp.concatenate(
        [_dot(yb, glu_ref[:, c * cw:(c + 1) * cw])
         * _sigmoid(_dot(yb, glu_ref[:, D_MODEL + c * cw:D_MODEL + (c + 1) * cw]))
         for c in range(S5_GLU_SPLIT)], axis=1)
    o_ref[...] = (x + _gate_rows(out, g1_ref[...], bb)).reshape(tb, bb, D_MODEL)

    @pl.when(t_idx == pl.num_programs(1) - 1)
    def _():
        sre_ref[...] = st_re[...]
        sim_ref[...] = st_im[...]


def _s5_layer(x3, sh, sc, g1, bbd, ccd, a_re, a_im, s0re, s0im, dskip, glu_w, tb, bb):
    T, B, _ = x3.shape
    per_b = pl.BlockSpec((bb, D_MODEL), lambda i, t: (i, 0))
    st = pl.BlockSpec((bb, S5_LANES), lambda i, t: (i, 0))
    full = lambda shape: pl.BlockSpec(shape, lambda i, t: (0,) * len(shape))
    return pl.pallas_call(
        functools.partial(_s5_kernel, tb=tb, bb=bb),
        grid=(B // bb, T // tb),
        in_specs=[
            pl.BlockSpec((tb, bb, D_MODEL), lambda i, t: (t, i, 0)),
            per_b, per_b, per_b,
            full((S5_BLOCKS, LANES, 2 * S5_LANES // S5_BLOCKS)),
            full((S5_BLOCKS, 2 * S5_LANES // S5_BLOCKS, LANES)),
            full((SUBLANES, S5_LANES)),
            full((SUBLANES, S5_LANES)),
            st, st,
            full((1, D_MODEL)),
            full((D_MODEL, 2 * D_MODEL)),
        ],
        out_specs=[pl.BlockSpec((tb, bb, D_MODEL), lambda i, t: (t, i, 0)), st, st],
        out_shape=[
            jax.ShapeDtypeStruct((T, B, D_MODEL), F32),
            jax.ShapeDtypeStruct((B, S5_LANES), F32),
            jax.ShapeDtypeStruct((B, S5_LANES), F32),
        ],
        scratch_shapes=[
            pltpu.VMEM((tb * bb, S5_LANES), F32),
            pltpu.VMEM((tb * bb, S5_LANES), F32),
            pltpu.VMEM((bb, S5_LANES), F32),
            pltpu.VMEM((bb, S5_LANES), F32),
        ],
        compiler_params=_cparams(("arbitrary", "arbitrary")),
        name="s5_layer",
    )(x3, sh, sc, g1, bbd, ccd, a_re, a_im, s0re, s0im, dskip, glu_w)


def _block_diag(w):
    _, r, c = w.shape
    w = w.reshape(S5_BLOCKS, 8, r, c)
    eye = jnp.eye(8, dtype=w.dtype)
    return (w[:, :, :, None, :] * eye[None, :, None, :, None]).reshape(S5_BLOCKS, 8 * r, 8 * c)


def _prep_params(p):
    w = {}
    w["ssd_in_w"] = [p["ssd_in_w"][j, :, :ZX_DIM].astype(BF16) for j in range(2)]
    w["ssd_dt_w"] = [jnp.pad(p["ssd_in_w"][j, :, ZX_DIM:], ((0, 0), (0, LANES - N_HEADS))).astype(BF16)
                     for j in range(2)]
    pad_h = lambda v: jnp.pad(v, (0, LANES - N_HEADS)).reshape(1, LANES)
    w["dt_bias"] = [pad_h(p["ssd_dt_bias"][j]) for j in range(2)]
    w["a_log"] = [pad_h(p["ssd_A_log"][j]) for j in range(2)]
    w["d_exp"] = [jnp.repeat(p["ssd_D"][j], HEAD_DIM).reshape(1, D_INNER) for j in range(2)]
    w["norm_w"] = [p["ssd_norm_w"][j].reshape(1, D_INNER) for j in range(2)]
    w["conv_w"] = [p["ssd_conv_w"][j] for j in range(2)]
    w["conv_b"] = [p["ssd_conv_b"][j].reshape(1, CONV_DIM) for j in range(2)]
    w["ssd_out_w"] = [p["ssd_out_w"][j].astype(BF16) for j in range(2)]
    w["ffn_w_in"] = [p["ffn_w_in"][i].astype(BF16) for i in range(DEPTH)]
    w["ffn_w_out"] = [p["ffn_w_out"][i].astype(BF16) for i in range(DEPTH)]
    w["final_w"] = p["final_norm_w"].reshape(1, D_MODEL)
    w["s5"] = []
    for j in range(2):
        ab_re, ab_im, bb_re, bb_im = _s5_disc(
            p["s5_A_re"][j], p["s5_A_im"][j], p["s5_log_dt"][j],
            jnp.swapaxes(p["s5_B_re"][j], 1, 2), jnp.swapaxes(p["s5_B_im"][j], 1, 2))
        bbd = jnp.concatenate([_block_diag(bb_re), _block_diag(bb_im)], axis=2).astype(BF16)
        c_re = jnp.swapaxes(p["s5_C_re"][j], 1, 2)
        c_im = jnp.swapaxes(p["s5_C_im"][j], 1, 2)
        ccd = jnp.concatenate([_block_diag(c_re), _block_diag(c_im)], axis=1).astype(BF16)
        bro = lambda v: jnp.broadcast_to(v.reshape(1, S5_LANES), (SUBLANES, S5_LANES))
        w["s5"].append(dict(bbd=bbd, ccd=ccd, a_re=bro(ab_re), a_im=bro(ab_im),
                            dskip=p["s5_D"][j].reshape(1, D_MODEL),
                            glu_w=p["s5_glu_w"][j].astype(BF16)))
    return w


def _trunk(x, mods, ssm0, conv0, re0, im0, w, tm, s5_tb, s5_bb, ssd_batch_major):
    B, T, _ = x.shape
    assert ssd_batch_major == (T % (2 * SSD_CHUNK) == 0)
    ssm_new = jnp.zeros(ssm0.shape, ssm0.dtype)
    conv_new, re_new, im_new = [], [], []
    if not ssd_batch_major:
        x = jnp.swapaxes(x, 0, 1)
    for i in range(DEPTH):
        j = i // 2
        final = i == DEPTH - 1
        parts = [mods[i][:, k * D_MODEL:(k + 1) * D_MODEL] for k in range(6)]
        by_rows = [v[None] for v in parts]
        by_batch = [v[:, None] for v in parts]
        ffn_w = (w["ffn_w_in"][i], w["ffn_w_out"][i], w["final_w"], tm, final)
        if i % 2 == 0:
            sh1, sc1, g1, sh2, sc2, g2 = by_batch if ssd_batch_major else by_rows
            x3 = x if ssd_batch_major else x.reshape(1, T * B, D_MODEL)
            ssd_w = (w["conv_w"][j], w["conv_b"][j], w["dt_bias"][j], w["a_log"][j], w["d_exp"][j],
                     w["norm_w"][j])
            if ssd_batch_major:
                y, ssm_new, cv_new = _ssd_chunked(x3, sh1, sc1, w["ssd_in_w"][j], w["ssd_dt_w"][j],
                                                  conv0[j], ssm0, ssm_new, j, *ssd_w)
                x = jnp.swapaxes(_ffn(x3, y, w["ssd_out_w"][j], g1, sh2, sc2, g2, *ffn_w), 0, 1)
            else:
                zx, dtraw = _inproj(x3, sh1, sc1, w["ssd_in_w"][j], w["ssd_dt_w"][j], tm)
                y, ssm_new, cv_new = _ssd_step(zx.reshape(T, B * ZX_DIM), dtraw[0].reshape(T, B * LANES),
                                               conv0[j], ssm0, ssm_new, j, *ssd_w, B)
                x = _ffn(x3, y.reshape(1, T * B, D_INNER), w["ssd_out_w"][j], g1, sh2, sc2, g2,
                         *ffn_w).reshape(T, B, D_MODEL)
            conv_new.append(cv_new)
        else:
            s5 = w["s5"][j]
            sh1, sc1, g1 = [v[0] for v in by_rows[:3]]
            xs5, sr, si = _s5_layer(x, sh1, sc1, g1, s5["bbd"], s5["ccd"], s5["a_re"], s5["a_im"],
                                    re0[j], im0[j], s5["dskip"], s5["glu_w"], s5_tb, s5_bb)
            re_new.append(sr)
            im_new.append(si)
            x = _ffn(xs5.reshape(1, T * B, D_MODEL), None, None, None, *by_rows[3:],
                     *ffn_w).reshape(T, B, D_MODEL)
            if ssd_batch_major:
                x = jnp.swapaxes(x, 0, 1)
    if not ssd_batch_major:
        x = jnp.swapaxes(x, 0, 1)
    return x, ssm_new, jnp.stack(conv_new), jnp.stack(re_new), jnp.stack(im_new)


def kernel(x_prompt, x_sample, state_ssm, state_conv, state_s5_re, state_s5_im, c_prompt, c_sample,
           ada_w, ada_b, ssd_in_w, ssd_conv_w, ssd_conv_b, ssd_dt_bias, ssd_A_log, ssd_D, ssd_norm_w,
           ssd_out_w, s5_A_re, s5_A_im, s5_log_dt, s5_B_re, s5_B_im, s5_C_re, s5_C_im, s5_D, s5_glu_w,
           ffn_w_in, ffn_w_out, final_norm_w):
    p = dict(ssd_in_w=ssd_in_w, ssd_conv_w=ssd_conv_w, ssd_conv_b=ssd_conv_b, ssd_dt_bias=ssd_dt_bias,
             ssd_A_log=ssd_A_log, ssd_D=ssd_D, ssd_norm_w=ssd_norm_w, ssd_out_w=ssd_out_w,
             s5_A_re=s5_A_re, s5_A_im=s5_A_im, s5_log_dt=s5_log_dt, s5_B_re=s5_B_re, s5_B_im=s5_B_im,
             s5_C_re=s5_C_re, s5_C_im=s5_C_im, s5_D=s5_D, s5_glu_w=s5_glu_w,
             ffn_w_in=ffn_w_in, ffn_w_out=ffn_w_out, final_norm_w=final_norm_w)
    w = _prep_params(p)
    bp, tp, _ = x_prompt.shape
    bs, ts, _ = x_sample.shape

    mods = _ada(jnp.concatenate([c_prompt, c_sample], axis=0), ada_w, ada_b)
    mods_p = [mods[l, :bp] for l in range(DEPTH)]
    mods_s = [mods[l, bp:] for l in range(DEPTH)]

    n_ssd, n_s5 = state_ssm.shape[0], state_s5_re.shape[0]
    rows_hp = N_HEADS * HEAD_DIM
    zeros_p = (jnp.zeros((n_ssd, bp, rows_hp, D_STATE), state_ssm.dtype),
               jnp.zeros((n_ssd, bp, CONV_K - 1, CONV_DIM), state_conv.dtype),
               jnp.zeros((n_s5, bp, S5_LANES), state_s5_re.dtype),
               jnp.zeros((n_s5, bp, S5_LANES), state_s5_im.dtype))
    yp, ssm_p, conv_p, re_p, im_p = _trunk(
        x_prompt, mods_p, *zeros_p, w,
        tm=512, s5_tb=512 // bp, s5_bb=bp, ssd_batch_major=True)
    ys, ssm_s, conv_s, re_s, im_s = _trunk(
        x_sample, mods_s,
        state_ssm.reshape(n_ssd, bs, rows_hp, D_STATE), state_conv,
        state_s5_re.reshape(n_s5, bs, S5_LANES), state_s5_im.reshape(n_s5, bs, S5_LANES), w,
        tm=512, s5_tb=ts, s5_bb=512 // ts, ssd_batch_major=False)

    ssm_shape = lambda b: (n_ssd, b, N_HEADS, HEAD_DIM, D_STATE)
    s5_shape = lambda b: (n_s5, b, S5_GROUPS, S5_STATE)
    return (yp, ys,
            ssm_p.reshape(ssm_shape(bp)), conv_p, re_p.reshape(s5_shape(bp)), im_p.reshape(s5_shape(bp)),
            ssm_s.reshape(ssm_shape(bs)), conv_s, re_s.reshape(s5_shape(bs)), im_s.reshape(s5_shape(bs)))
```

```python
import functools
import math

import jax
import jax.numpy as jnp
from jax import lax
from jax.experimental import pallas as pl
from jax.experimental.pallas import tpu as pltpu

F32 = jnp.float32
BF16 = jnp.bfloat16

D_MODEL = 1024
DEPTH = 4
D_INNER = 2048
HEAD_DIM = 64
N_HEADS = 32
N_GROUPS = 8
HEADS_PER_GROUP = 4
D_STATE = 128
CONV_K = 4
BC_DIM = 2 * N_GROUPS * D_STATE
CONV_DIM = D_INNER + BC_DIM
ZX_DIM = D_INNER + CONV_DIM
SSD_CHUNK = 128
S5_GROUPS = 64
S5_GROUP = 16
S5_STATE = 64
S5_LANES = S5_GROUPS * S5_STATE
S5_BLOCKS = D_MODEL // 128
D_FF = 2816
RMS_EPS = 1e-6
LANES = 128
SUBLANES = 8
CONV_PAD = 8
N_XTILES = D_INNER // LANES
GROUP_W = HEADS_PER_GROUP * HEAD_DIM
SSD_STEP_BATCHES = 4
FFN_SPLIT = 11
S5_GLU_SPLIT = 4
S5_BU_SPLIT = 4
INPROJ_COLS = 512
VMEM_LIMIT = 52 * 1024 * 1024


def _cparams(sem):
    return pltpu.CompilerParams(dimension_semantics=sem, vmem_limit_bytes=VMEM_LIMIT)


def _sigmoid(x):
    return 1.0 / (1.0 + jnp.exp(-x))


def _silu(x):
    hx = 0.5 * x
    return hx + hx * jnp.tanh(hx)


def _rows_by_batch(x, nb):
    tm, d = x.shape
    return x.reshape(tm // nb, nb, d)


def _modulate_rows(x, sh, sc, nb, ng=1):
    if ng > 1:
        x3 = x.reshape(ng, x.shape[0] // ng, x.shape[1])
        r = lax.rsqrt(jnp.mean(x3 * x3, axis=-1, keepdims=True) + RMS_EPS)
        return ((x3 * r) * (1.0 + sc) + sh).reshape(x.shape)
    if nb == 1:
        r = lax.rsqrt(jnp.mean(x * x, axis=-1, keepdims=True) + RMS_EPS)
        return (x * r) * (1.0 + sc) + sh
    x3 = _rows_by_batch(x, nb)
    r = lax.rsqrt(jnp.mean(x3 * x3, axis=-1, keepdims=True) + RMS_EPS)
    h = (x3 * r) * (1.0 + sc[None]) + sh[None]
    return h.reshape(x.shape)


def _gate_rows(v, g, nb, ng=1):
    if ng > 1:
        return (v.reshape(ng, v.shape[0] // ng, v.shape[1]) * g).reshape(v.shape)
    if nb == 1:
        return v * g
    return (_rows_by_batch(v, nb) * g[None]).reshape(v.shape)


def _dot(a, b):
    return jnp.dot(a, b, preferred_element_type=F32)


def _dot_nt(a, b):
    return lax.dot_general(a, b, (((1,), (1,)), ((), ())), preferred_element_type=F32)


def _dot_tn(a, b):
    return lax.dot_general(a, b, (((0,), (0,)), ((), ())), preferred_element_type=F32)


def _ada_kernel(c_ref, w_ref, b_ref, o_ref):
    s = _silu(c_ref[...]).astype(BF16)
    o_ref[0] = _dot(s, w_ref[0].astype(BF16)) + b_ref[0]


def _ada(c_all, ada_w, ada_b):
    nrow = c_all.shape[0]
    tn = 1536
    return pl.pallas_call(
        _ada_kernel,
        grid=(DEPTH, 6 * D_MODEL // tn),
        in_specs=[
            pl.BlockSpec((nrow, D_MODEL), lambda l, j: (0, 0)),
            pl.BlockSpec((1, D_MODEL, tn), lambda l, j: (l, 0, j)),
            pl.BlockSpec((1, 1, tn), lambda l, j: (l, 0, j)),
        ],
        out_specs=pl.BlockSpec((1, nrow, tn), lambda l, j: (l, 0, j)),
        out_shape=jax.ShapeDtypeStruct((DEPTH, nrow, 6 * D_MODEL), F32),
        compiler_params=_cparams(("arbitrary", "arbitrary")),
        name="ada",
    )(c_all, ada_w, ada_b.reshape(DEPTH, 1, 6 * D_MODEL))


def _inproj_kernel(x_ref, sh_ref, sc_ref, w_ref, wdt_ref, zx_ref, dt_ref, *, nb):
    h = _modulate_rows(x_ref[...], sh_ref[...], sc_ref[...], nb).astype(BF16)
    zx_ref[...] = _dot(h, w_ref[...])
    dt_ref[...] = _dot(h, wdt_ref[...])


def _inproj(x3, sh, sc, w, wdt, tm):
    ng, rows, _ = x3.shape
    nb = sh.shape[1]
    tn = 2048
    per_g = pl.BlockSpec((None, nb, D_MODEL), lambda j, g, i: (g, 0, 0))
    return pl.pallas_call(
        functools.partial(_inproj_kernel, nb=nb),
        grid=(ZX_DIM // tn, ng, rows // tm),
        in_specs=[
            pl.BlockSpec((None, tm, D_MODEL), lambda j, g, i: (g, i, 0)),
            per_g, per_g,
            pl.BlockSpec((D_MODEL, tn), lambda j, g, i: (0, j)),
            pl.BlockSpec((D_MODEL, LANES), lambda j, g, i: (0, 0)),
        ],
        out_specs=[
            pl.BlockSpec((None, tm, tn), lambda j, g, i: (g, i, j)),
            pl.BlockSpec((None, None, tm, LANES), lambda j, g, i: (j, g, i, 0)),
        ],
        out_shape=[
            jax.ShapeDtypeStruct((ng, rows, ZX_DIM), F32),
            jax.ShapeDtypeStruct((ZX_DIM // tn, ng, rows, LANES), F32),
        ],
        compiler_params=_cparams(("arbitrary", "arbitrary", "arbitrary")),
        name="ssd_inproj",
    )(x3, sh, sc, w, wdt)


def _cumsum_rows(a, L):
    rows = lax.broadcasted_iota(jnp.int32, a.shape, 0)
    k = 1
    while k < L:
        a = a + jnp.where(rows >= k, pltpu.roll(a, k, 0), 0.0)
        k *= 2
    return a


def _transpose_rows(a, L):
    if L < LANES:
        a = jnp.concatenate([a, jnp.zeros((LANES - L, LANES), F32)], axis=0)
    return a.T[:, :L]


def _ssd_step_kernel(zx_ref, dt_ref, conv0_ref, ssm0_ref,
                     cw_ref, cb_ref, dtb_ref, alog_ref, dexp_ref, nw_ref, gsum_ref, expand_ref,
                     *rest, L, nbatch):
    y_ref, ssm_out_ref, conv_out_ref, pad, xbc_scr = rest[-5:]
    lo = CONV_PAD - (CONV_K - 1)
    row_t = lax.broadcasted_iota(jnp.int32, (L, LANES), 0)
    b_lo, c_lo = D_INNER, D_INNER + N_GROUPS * D_STATE

    def row_bcast(i, s, lt0, lt1):
        return jnp.concatenate([jnp.broadcast_to(xbc_scr[i, lt, s:s + 1, :], (L, LANES))
                                for lt in range(lt0, lt1)], axis=1)

    xbcs, dts, acss, prods = [], [], [], []
    for i in range(nbatch):
        z0 = i * ZX_DIM
        conv_tiles = []
        for lt in range(2 * N_XTILES):
            cols = slice(lt * LANES, (lt + 1) * LANES)
            pad[i, lt, lo:CONV_PAD, :] = conv0_ref[i, :, cols]
            pad[i, lt, CONV_PAD:CONV_PAD + L, :] = zx_ref[:, z0 + D_INNER + lt * LANES:z0 + D_INNER + (lt + 1) * LANES]
            acc = cb_ref[:, cols]
            for k in range(CONV_K):
                acc = acc + cw_ref[k:k + 1, cols] * pad[i, lt, lo + k:lo + k + L, :]
            conv_tiles.append(_silu(acc))
            xbc_scr[i, lt] = conv_tiles[lt]
            conv_out_ref[i, :, cols] = pad[i, lt, lo + L:CONV_PAD + L, :]
        xbc = jnp.concatenate(conv_tiles, axis=1)
        dtr = dt_ref[:, i * LANES:(i + 1) * LANES] + dtb_ref[...]
        dt = jnp.maximum(dtr, 0.0) + jnp.log1p(jnp.exp(-jnp.abs(dtr)))
        xbcs.append(xbc)
        dts.append(dt)
        acss.append(_cumsum_rows(dt * (-jnp.exp(alog_ref[...])), L))
        cm = xbc[:, c_lo:]
        prods += [cm * row_bcast(i, s, b_lo // LANES, c_lo // LANES) for s in range(L)]

    p_hi, p_lo = _split_bf16(jnp.concatenate(prods, axis=0))
    cbh = _dot(jnp.concatenate([p_hi, p_lo], axis=1).astype(BF16), gsum_ref[...])

    per = L * L + 2 * L
    rows = []
    for i in range(nbatch):
        dt, acs = dts[i], acss[i]
        for s in range(L):
            decay = jnp.exp(jnp.where(row_t >= s, acs - acs[s:s + 1, :], -jnp.inf))
            r0 = (i * L + s) * L
            rows.append(cbh[r0:r0 + L, :] * decay * dt[s:s + 1, :])
        rows += [jnp.exp(acs), jnp.exp(acs[L - 1:L, :] - acs) * dt]
    w_hi, w_lo = _split_bf16(jnp.concatenate(rows, axis=0))
    wide = _dot(jnp.concatenate([w_hi, w_lo], axis=1).astype(BF16), expand_ref[...])

    for i in range(nbatch):
        z0, w0 = i * ZX_DIM, i * per
        xbc = xbcs[i]
        xs = xbc[:, :D_INNER]
        y = dexp_ref[...] * xs
        for s in range(L):
            y = y + wide[w0 + s * L:w0 + (s + 1) * L, :] * row_bcast(i, s, 0, N_XTILES)
        ea_x = wide[w0 + L * L:w0 + L * L + L, :]
        xw = xs * wide[w0 + L * L + L:w0 + per, :]
        e_last = jnp.exp(acss[i][L - 1:L, :])
        for g in range(N_GROUPS):
            gl = slice(g * GROUP_W, (g + 1) * GROUP_W)
            bg = xbc[:, b_lo + g * D_STATE:b_lo + (g + 1) * D_STATE].astype(BF16)
            cg = xbc[:, c_lo + g * D_STATE:c_lo + (g + 1) * D_STATE].astype(BF16)
            sg = ssm0_ref[i, gl, :]
            yg = y[:, gl] + ea_x[:, gl] * _dot_nt(cg, sg.astype(BF16))
            upd = _dot_tn(xw[:, gl].astype(BF16), bg)
            for r in range(HEADS_PER_GROUP):
                h = g * HEADS_PER_GROUP + r
                rr = slice(r * HEAD_DIM, (r + 1) * HEAD_DIM)
                ssm_out_ref[i, g * GROUP_W + r * HEAD_DIM:g * GROUP_W + (r + 1) * HEAD_DIM, :] = (
                    e_last[:, h:h + 1] * sg[rr, :] + upd[rr, :])
            yg = yg * _silu(zx_ref[:, z0 + g * GROUP_W:z0 + (g + 1) * GROUP_W])
            yg = yg * lax.rsqrt(jnp.mean(yg * yg, axis=-1, keepdims=True) + RMS_EPS)
            y0 = i * D_INNER + g * GROUP_W
            y_ref[:, y0:y0 + GROUP_W] = (yg * nw_ref[:, gl]).astype(y_ref.dtype)


def _split_bf16(v):
    hi = v.astype(BF16).astype(F32)
    return hi, (v - hi).astype(BF16).astype(F32)


def _ssd_chunk_kernel(x_ref, xn_ref, sh_ref, sc_ref, win_ref, wdt_ref, conv0_ref, ssm0_ref,
                      cw_ref, cb_ref, dtb_ref, alog_ref, dexp_ref, nw_ref, expand_ref, *rest, L):
    y_ref, ssm_out_ref, conv_out_ref, pad, state_t, z_scr, dt_scr = rest[-7:]
    c = pl.program_id(1)
    lo = CONV_PAD - (CONV_K - 1)

    def projection_steps(x_rows, slot):
        h = _modulate_rows(x_rows, sh_ref[...], sc_ref[...], 1).astype(BF16)
        pw = INPROJ_COLS

        def z_step(q):
            z_scr[slot, :, q * pw:(q + 1) * pw] = _dot(h, win_ref[:, q * pw:(q + 1) * pw])

        def xbc_step(q):
            res = _dot(h, win_ref[:, D_INNER + q * pw:D_INNER + (q + 1) * pw])
            for r in range(pw // LANES):
                pad[slot, q * (pw // LANES) + r, CONV_PAD:CONV_PAD + L, :] = res[:, r * LANES:(r + 1) * LANES]

        def dt_step():
            dt_scr[slot] = _dot(h, wdt_ref[...])

        return ([functools.partial(xbc_step, q) for q in range(CONV_DIM // pw)]
                + [functools.partial(z_step, q) for q in range(D_INNER // pw)] + [dt_step])

    @pl.when(c == 0)
    def _():
        state_t[...] = ssm0_ref[...].T
        for lt in range(2 * N_XTILES):
            pad[0, lt, lo:CONV_PAD, :] = conv0_ref[0, :, lt * LANES:(lt + 1) * LANES]
        for step in projection_steps(x_ref[:L, :], 0):
            step()

    body = functools.partial(_ssd_chunk_body, z_scr=z_scr, dt_scr=dt_scr, pad=pad, state_t=state_t,
                             cw_ref=cw_ref, cb_ref=cb_ref, dtb_ref=dtb_ref, alog_ref=alog_ref,
                             dexp_ref=dexp_ref, nw_ref=nw_ref, expand_ref=expand_ref, y_ref=y_ref, L=L)
    body(0, 1, slice(0, L), projection_steps(x_ref[L:, :], 1))
    body(1, 0, slice(L, 2 * L), projection_steps(xn_ref[...], 0))
    for lt in range(2 * N_XTILES):
        conv_out_ref[0, :, lt * LANES:(lt + 1) * LANES] = pad[0, lt, lo:CONV_PAD, :]

    @pl.when(c == pl.num_programs(1) - 1)
    def _():
        ssm_out_ref[...] = state_t[...].T


def _ssd_chunk_body(slot, next_slot, rows, side_steps, *, z_scr, dt_scr, pad, state_t, cw_ref, cb_ref,
                    dtb_ref, alog_ref, dexp_ref, nw_ref, expand_ref, y_ref, L):
    lo = CONV_PAD - (CONV_K - 1)
    assert len(side_steps) <= 2 * N_XTILES // 4 + N_GROUPS
    pending = iter(side_steps)

    def side_point():
        step = next(pending, None)
        if step is not None:
            step()

    def conv_tile(lt):
        cols = slice(lt * LANES, (lt + 1) * LANES)
        acc = cb_ref[:, cols]
        for k in range(CONV_K):
            acc = acc + cw_ref[k:k + 1, cols] * pad[slot, lt, lo + k:lo + k + L, :]
        return _silu(acc)

    tiles = []
    for lt in range(2 * N_XTILES):
        tiles.append(conv_tile(lt))
        if lt % 4 == 3:
            side_point()

    for lt in range(2 * N_XTILES):
        pad[next_slot, lt, lo:CONV_PAD, :] = pad[slot, lt, lo + L:CONV_PAD + L, :]

    dtr = dt_scr[slot] + dtb_ref[...]
    dt = jnp.maximum(dtr, 0.0) + jnp.log1p(jnp.exp(-jnp.abs(dtr)))
    a = dt * (-jnp.exp(alog_ref[...]))
    acs = _cumsum_rows(a, L)
    acs_t = acs.T
    dt_t = dt.T
    a_last = acs[L - 1:L, :]
    ea_hi, ea_lo = _split_bf16(jnp.exp(acs))
    we_hi, we_lo = _split_bf16(jnp.exp(a_last - acs) * dt)
    parts = jnp.concatenate([jnp.concatenate([ea_hi, ea_lo], axis=1),
                             jnp.concatenate([we_hi, we_lo], axis=1)], axis=0).astype(BF16)
    wide = _dot(parts, expand_ref[...])
    ea_x = wide[:L]
    we_x = wide[L:]

    causal = (lax.broadcasted_iota(jnp.int32, (L, L), 0)
              >= lax.broadcasted_iota(jnp.int32, (L, L), 1))
    lane_head = lax.broadcasted_iota(jnp.int32, (L, GROUP_W), 1) // HEAD_DIM

    for g in range(N_GROUPS):
        gl = slice(g * GROUP_W, (g + 1) * GROUP_W)
        bg = tiles[N_XTILES + g].astype(BF16)
        cg = tiles[N_XTILES + N_GROUPS + g].astype(BF16)
        cb = _dot_nt(cg, bg)
        sg = state_t[:, gl]
        y_off = _dot(cg, sg.astype(BF16))
        ms = []
        for r in range(HEADS_PER_GROUP):
            h = g * HEADS_PER_GROUP + r
            seg = acs[:, h:h + 1] - acs_t[h:h + 1, :]
            decay = jnp.exp(jnp.where(causal, seg, -jnp.inf))
            ms.append((cb * decay * dt_t[h:h + 1, :]).astype(BF16))
        xg = jnp.concatenate([tiles[2 * g], tiles[2 * g + 1]], axis=1)
        xgb = xg.astype(BF16)
        xblk = jnp.concatenate([jnp.where(lane_head == r, xgb, jnp.zeros_like(xgb))
                                for r in range(HEADS_PER_GROUP)], axis=0)
        y_diag = _dot(jnp.concatenate(ms, axis=1), xblk)
        yg = y_diag + ea_x[:, gl] * y_off + dexp_ref[:, gl] * xg
        upd = _dot_tn(bg, (xg * we_x[:, gl]).astype(BF16))
        state_t[:, gl] = ea_x[L - 1:L, gl] * sg + upd
        yg = yg * _silu(z_scr[slot, :, gl])
        yg = yg * lax.rsqrt(jnp.mean(yg * yg, axis=-1, keepdims=True) + RMS_EPS)
        y_ref[rows, gl] = (yg * nw_ref[:, gl]).astype(y_ref.dtype)
        side_point()


def _ssm_state_spec(grid_rank, j, nb):
    shape = (None, nb, N_HEADS * HEAD_DIM, D_STATE)
    if grid_rank == 2:
        return pl.BlockSpec(shape, lambda b, c: (j, b, 0, 0))
    return pl.BlockSpec(shape, lambda b: (j, b, 0, 0))


def _ssd_call(body, grid, in_specs, args, y_spec, y_shape, conv_spec, ssm_all, ssm_prev, j, nb, scratch, sem):
    nbatch = ssm_all.shape[1]
    in_specs = list(in_specs) + [pl.BlockSpec(memory_space=pl.ANY)]
    args = list(args) + [ssm_prev]
    aliases = {len(args) - 1: 1}
    return pl.pallas_call(
        body,
        grid=grid,
        in_specs=in_specs,
        out_specs=[y_spec, _ssm_state_spec(len(grid), j, nb), conv_spec],
        out_shape=[
            jax.ShapeDtypeStruct(y_shape, BF16),
            jax.ShapeDtypeStruct(ssm_all.shape, F32),
            jax.ShapeDtypeStruct((nbatch, CONV_K - 1, CONV_DIM), F32),
        ],
        scratch_shapes=scratch,
        input_output_aliases=aliases,
        compiler_params=_cparams(sem),
        name="ssd_core",
    )(*args)


def _ssd_chunked(x, sh, sc, w_in, w_dt, conv0, ssm_all, ssm_prev, j, cw, cb, dtb, alog, dexp, nw):
    B, T, _ = x.shape
    L = SSD_CHUNK
    nsteps = T // (2 * L)
    last_chunk = T // L - 1
    full = lambda shape: pl.BlockSpec(shape, lambda b, c: (0,) * len(shape))
    resident = lambda shape: pl.BlockSpec(shape, lambda b, c: (0,) * len(shape), pipeline_mode=pl.Buffered(1))
    pair = lambda w: pl.BlockSpec((None, 2 * L, w), lambda b, c: (b, c, 0))
    nxt = pl.BlockSpec((None, L, D_MODEL), lambda b, c: (b, jnp.minimum(2 * c + 2, last_chunk), 0))
    per_b = pl.BlockSpec((None, 1, D_MODEL), lambda b, c: (b, 0, 0))
    conv_spec = pl.BlockSpec((1, CONV_K - 1, CONV_DIM), lambda b, c: (b, 0, 0))
    sel = (jnp.arange(D_INNER)[None, :] // HEAD_DIM == jnp.arange(LANES)[:, None]).astype(BF16)
    in_specs = [
        pair(D_MODEL), nxt, per_b, per_b, resident((D_MODEL, ZX_DIM)), resident((D_MODEL, LANES)),
        conv_spec, _ssm_state_spec(2, j, None),
        full((CONV_K, CONV_DIM)), full((1, CONV_DIM)), full((1, LANES)), full((1, LANES)),
        full((1, D_INNER)), full((1, D_INNER)), resident((2 * LANES, D_INNER)),
    ]
    args = [x, x, sh, sc, w_in, w_dt, conv0, ssm_all, cw, cb, dtb, alog, dexp, nw,
            jnp.concatenate([sel, sel], axis=0)]
    scratch = [
        pltpu.VMEM((2, 2 * N_XTILES, CONV_PAD + L, LANES), F32),
        pltpu.VMEM((D_STATE, N_HEADS * HEAD_DIM), F32),
        pltpu.VMEM((2, L, D_INNER), F32),
        pltpu.VMEM((2, L, LANES), F32),
    ]
    return _ssd_call(functools.partial(_ssd_chunk_kernel, L=L), (B, nsteps), in_specs, args,
                     pair(D_INNER), (B, T, D_INNER), conv_spec, ssm_all, ssm_prev, j, None, scratch,
                     ("arbitrary", "arbitrary"))


def _ssd_step(zx, dtraw, conv0, ssm_all, ssm_prev, j, cw, cb, dtb, alog, dexp, nw, B):
    T = zx.shape[0]
    nb = SSD_STEP_BATCHES
    full = lambda shape: pl.BlockSpec(shape, lambda b: (0,) * len(shape))
    col = lambda w: pl.BlockSpec((T, nb * w), lambda b: (0, b))
    conv_spec = pl.BlockSpec((nb, CONV_K - 1, CONV_DIM), lambda b: (b, 0, 0))
    in_specs = [
        col(ZX_DIM), col(LANES), conv_spec, _ssm_state_spec(1, j, nb),
        full((CONV_K, CONV_DIM)), full((1, CONV_DIM)), full((1, LANES)), full((1, LANES)),
        full((1, D_INNER)), full((1, D_INNER)),
        full((2 * N_GROUPS * D_STATE, LANES)), full((2 * LANES, D_INNER)),
    ]
    head_ids = jnp.arange(LANES)
    gsum = ((jnp.arange(N_GROUPS * D_STATE)[:, None] // D_STATE == head_ids[None, :] // HEADS_PER_GROUP)
            & (head_ids[None, :] < N_HEADS)).astype(BF16)
    sel = (jnp.arange(D_INNER)[None, :] // HEAD_DIM == head_ids[:, None]).astype(BF16)
    args = [zx, dtraw, conv0, ssm_all, cw, cb, dtb, alog, dexp, nw,
            jnp.concatenate([gsum, gsum], axis=0), jnp.concatenate([sel, sel], axis=0)]
    scratch = [pltpu.VMEM((nb, 2 * N_XTILES, CONV_PAD + T, LANES), F32),
               pltpu.VMEM((nb, 2 * N_XTILES, T, LANES), F32)]
    return _ssd_call(functools.partial(_ssd_step_kernel, L=T, nbatch=nb), (B // nb,), in_specs, args,
                     col(D_INNER), (T, B * D_INNER), conv_spec, ssm_all, ssm_prev, j, nb, scratch,
                     ("arbitrary",))


def _ffn_kernel(*refs, nb, ng, has_proj, final, relayout):
    if relayout:
        refs, relay = refs[:-1], refs[-1]
    if has_proj:
        x_ref, y_ref, ow_ref, g1_ref, sh_ref, sc_ref, g2_ref, win_ref, wo_ref, fw_ref, o_ref = refs
    else:
        x_ref, sh_ref, sc_ref, g2_ref, win_ref, wo_ref, fw_ref, o_ref = refs
    x = x_ref[...]
    tm = x.size // D_MODEL
    x = x.reshape(tm, D_MODEL)
    if has_proj:
        y = y_ref[...].reshape(tm, D_INNER)
        x = x + _gate_rows(_dot(y, ow_ref[...]), g1_ref[...], nb, ng)
    h = _modulate_rows(x, sh_ref[...], sc_ref[...], nb, ng).astype(BF16)
    tf = D_FF // FFN_SPLIT
    acc = None
    for c in range(FFN_SPLIT):
        gate = _dot(h, win_ref[:, c * tf:(c + 1) * tf])
        up = _dot(h, win_ref[:, D_FF + c * tf:D_FF + (c + 1) * tf])
        part = _dot((_silu(gate) * up).astype(BF16), wo_ref[c * tf:(c + 1) * tf, :])
        acc = part if acc is None else acc + part
    x2 = x + _gate_rows(acc, g2_ref[...], nb, ng)
    if final:
        x2 = x2 * lax.rsqrt(jnp.mean(x2 * x2, axis=-1, keepdims=True) + RMS_EPS) * fw_ref[...]
    n_lt = D_MODEL // LANES
    if relayout == "to_batch_major":
        for lt in range(n_lt):
            relay[lt] = x2[:, lt * LANES:(lt + 1) * LANES]
        for b in range(nb):
            o_ref[b] = jnp.concatenate([relay[lt, pl.ds(b, tm // nb, stride=nb), :] for lt in range(n_lt)], axis=1)
    elif relayout == "to_time_major":
        tg = tm // ng
        for lt in range(n_lt):
            for g in range(ng):
                relay[lt, pl.ds(g, tg, stride=ng), :] = x2[g * tg:(g + 1) * tg, lt * LANES:(lt + 1) * LANES]
        o_ref[...] = jnp.concatenate([relay[lt] for lt in range(n_lt)], axis=1).reshape(o_ref.shape)
    else:
        o_ref[...] = x2.reshape(o_ref.shape)


def _ffn(x3, y3, out_w, g1, sh, sc, g2, w_in, w_out, fw, tm, final, relayout=None):
    ng, rows, _ = x3.shape
    nb = sh.shape[1]
    has_proj = y3 is not None
    resident = lambda shape: pl.BlockSpec(shape, lambda g, i: (0,) * len(shape), pipeline_mode=pl.Buffered(1))
    if relayout == "to_time_major":
        assert nb == 1
        tg = tm // ng
        row_blk = lambda w: pl.BlockSpec((ng, tg, w), lambda g, i: (0, i, 0))
        per_g = pl.BlockSpec((ng, 1, D_MODEL), lambda g, i: (0, 0, 0))
        grid = (1, rows // tg)
        out_spec = pl.BlockSpec((tg, ng, D_MODEL), lambda g, i: (i, 0, 0))
        out_shape = (rows, ng, D_MODEL)
        kernel_ng = ng
    else:
        row_blk = lambda w: pl.BlockSpec((None, tm, w), lambda g, i: (g, i, 0))
        per_g = pl.BlockSpec((None, nb, D_MODEL), lambda g, i: (g, 0, 0))
        grid = (ng, rows // tm)
        kernel_ng = 1
        if relayout == "to_batch_major":
            assert ng == 1
            out_spec = pl.BlockSpec((nb, tm // nb, D_MODEL), lambda g, i: (0, i, 0))
            out_shape = (nb, rows // nb, D_MODEL)
        else:
            out_spec = row_blk(D_MODEL)
            out_shape = (ng, rows, D_MODEL)
    in_specs = [row_blk(D_MODEL)]
    args = [x3]
    if has_proj:
        in_specs += [row_blk(D_INNER), resident((D_INNER, D_MODEL)), per_g]
        args += [y3, out_w, g1]
    in_specs += [per_g, per_g, per_g,
                 resident((D_MODEL, 2 * D_FF)), resident((D_FF, D_MODEL)), resident((1, D_MODEL))]
    args += [sh, sc, g2, w_in, w_out, fw]
    return pl.pallas_call(
        functools.partial(_ffn_kernel, nb=nb, ng=kernel_ng, has_proj=has_proj, final=final, relayout=relayout),
        grid=grid,
        in_specs=in_specs,
        out_specs=out_spec,
        out_shape=jax.ShapeDtypeStruct(out_shape, F32),
        scratch_shapes=[pltpu.VMEM((D_MODEL // LANES, tm, LANES), F32)] if relayout else [],
        compiler_params=_cparams(("arbitrary", "arbitrary")),
        name="proj_ffn" if has_proj else "ffn",
    )(*args)


def _s5_disc_kernel(are_ref, aim_ref, ldt_ref, bre_ref, bim_ref, abre_ref, abim_ref, bbre_ref, bbim_ref):
    lre, lim = are_ref[...], aim_ref[...]
    dt = jnp.exp(ldt_ref[...])
    mag = jnp.exp(lre * dt)
    ab_re, ab_im = mag * jnp.cos(lim * dt), mag * jnp.sin(lim * dt)
    den = lre * lre + lim * lim
    nr, ni = ab_re - 1.0, ab_im
    q_re = (nr * lre + ni * lim) / den
    q_im = (ni * lre - nr * lim) / den
    abre_ref[...] = ab_re
    abim_ref[...] = ab_im
    br, bi = bre_ref[...], bim_ref[...]
    bbre_ref[...] = q_re[:, None, :] * br - q_im[:, None, :] * bi
    bbim_ref[...] = q_re[:, None, :] * bi + q_im[:, None, :] * br


def _s5_disc(a_re, a_im, log_dt, b_re, b_im):
    gn = jax.ShapeDtypeStruct((S5_GROUPS, S5_STATE), F32)
    gin = jax.ShapeDtypeStruct((S5_GROUPS, S5_GROUP, S5_STATE), F32)
    return pl.pallas_call(
        _s5_disc_kernel,
        out_shape=[gn, gn, gin, gin],
        name="s5_disc",
    )(a_re, a_im, log_dt.reshape(S5_GROUPS, 1), b_re, b_im)


def _s5_kernel(x_ref, sh_ref, sc_ref, g1_ref, bb_ref, cc_ref, are_ref, aim_ref, s0re_ref, s0im_ref,
               dskip_ref, glu_ref, o_ref, sre_ref, sim_ref, xs_re, xs_im, st_re, st_im, *, tb, bb):
    t_idx = pl.program_id(1)
    rows = tb * bb

    @pl.when(t_idx == 0)
    def _():
        st_re[...] = s0re_ref[...]
        st_im[...] = s0im_ref[...]

    x = x_ref[...].reshape(rows, D_MODEL)
    h = _modulate_rows(x, sh_ref[...], sc_ref[...], bb)
    hb = h.astype(BF16)
    half = S5_LANES // S5_BLOCKS
    pw = 2 * half // S5_BU_SPLIT

    def bu_pieces(k):
        out = [None] * S5_BU_SPLIT

        def piece(j):
            out[j] = _dot(hb[:, k * LANES:(k + 1) * LANES], bb_ref[k, :, j * pw:(j + 1) * pw])

        return out, [functools.partial(piece, j) for j in range(S5_BU_SPLIT)]

    def y_pieces(k):
        out = [None] * 2

        def piece(j):
            src = xs_im if j else xs_re
            out[j] = _dot(src[:, k * half:(k + 1) * half].astype(BF16), cc_ref[k, j * half:(j + 1) * half, :])

        return out, [functools.partial(piece, j) for j in range(2)]

    n_iter = (bb // SUBLANES) * tb
    bu_next, steps = bu_pieces(0)
    for step in steps:
        step()
    y_parts = [None] * S5_BLOCKS
    for k in range(S5_BLOCKS):
        ks = slice(k * half, (k + 1) * half)
        bu = jnp.concatenate(bu_next, axis=1)
        side = []
        if k + 1 < S5_BLOCKS:
            bu_next, steps = bu_pieces(k + 1)
            side += steps
        if k > 0:
            y_parts[k - 1], steps = y_pieces(k - 1)
            side += steps
        run_at = {((j + 1) * n_iter) // (len(side) + 1): step for j, step in enumerate(side)}
        ar, ai = are_ref[:, ks], aim_ref[:, ks]
        it = 0
        for rg in range(bb // SUBLANES):
            rs = slice(rg * SUBLANES, (rg + 1) * SUBLANES)
            xr, xi = st_re[rs, ks], st_im[rs, ks]
            for t in range(tb):
                sl = slice(t * bb + rg * SUBLANES, t * bb + (rg + 1) * SUBLANES)
                xr, xi = (ar * xr - ai * xi + bu[sl, :half], ar * xi + ai * xr + bu[sl, half:])
                xs_re[sl, ks] = xr
                xs_im[sl, ks] = xi
                it += 1
                if it in run_at:
                    run_at[it]()
            st_re[rs, ks] = xr
            st_im[rs, ks] = xi
    y_parts[S5_BLOCKS - 1], steps = y_pieces(S5_BLOCKS - 1)
    for step in steps:
        step()
    y = jnp.concatenate([p[0] - p[1] for p in y_parts], axis=1) + dskip_ref[...] * h
    y = 0.5 * y * (1.0 + jnp.tanh(math.sqrt(2.0 / math.pi) * (y + 0.044715 * (y * y * y))))
    yb = y.astype(BF16)
    cw = D_MODEL // S5_GLU_SPLIT
    out = jnp.concatenate(
        [_dot(yb, glu_ref[:, c * cw:(c + 1) * cw])
         * _sigmoid(_dot(yb, glu_ref[:, D_MODEL + c * cw:D_MODEL + (c + 1) * cw]))
         for c in range(S5_GLU_SPLIT)], axis=1)
    o_ref[...] = (x + _gate_rows(out, g1_ref[...], bb)).reshape(tb, bb, D_MODEL)

    @pl.when(t_idx == pl.num_programs(1) - 1)
    def _():
        sre_ref[...] = st_re[...]
        sim_ref[...] = st_im[...]


def _s5_layer(x3, sh, sc, g1, bbd, ccd, a_re, a_im, s0re, s0im, dskip, glu_w, tb, bb):
    T, B, _ = x3.shape
    per_b = pl.BlockSpec((bb, D_MODEL), lambda i, t: (i, 0))
    st = pl.BlockSpec((bb, S5_LANES), lambda i, t: (i, 0))
    full = lambda shape: pl.BlockSpec(shape, lambda i, t: (0,) * len(shape))
    return pl.pallas_call(
        functools.partial(_s5_kernel, tb=tb, bb=bb),
        grid=(B // bb, T // tb),
        in_specs=[
            pl.BlockSpec((tb, bb, D_MODEL), lambda i, t: (t, i, 0)),
            per_b, per_b, per_b,
            full((S5_BLOCKS, LANES, 2 * S5_LANES // S5_BLOCKS)),
            full((S5_BLOCKS, 2 * S5_LANES // S5_BLOCKS, LANES)),
            full((SUBLANES, S5_LANES)),
            full((SUBLANES, S5_LANES)),
            st, st,
            full((1, D_MODEL)),
            full((D_MODEL, 2 * D_MODEL)),
        ],
        out_specs=[pl.BlockSpec((tb, bb, D_MODEL), lambda i, t: (t, i, 0)), st, st],
        out_shape=[
            jax.ShapeDtypeStruct((T, B, D_MODEL), F32),
            jax.ShapeDtypeStruct((B, S5_LANES), F32),
            jax.ShapeDtypeStruct((B, S5_LANES), F32),
        ],
        scratch_shapes=[
            pltpu.VMEM((tb * bb, S5_LANES), F32),
            pltpu.VMEM((tb * bb, S5_LANES), F32),
            pltpu.VMEM((bb, S5_LANES), F32),
            pltpu.VMEM((bb, S5_LANES), F32),
        ],
        compiler_params=_cparams(("arbitrary", "arbitrary")),
        name="s5_layer",
    )(x3, sh, sc, g1, bbd, ccd, a_re, a_im, s0re, s0im, dskip, glu_w)


def _block_diag(w):
    _, r, c = w.shape
    w = w.reshape(S5_BLOCKS, 8, r, c)
    eye = jnp.eye(8, dtype=w.dtype)
    return (w[:, :, :, None, :] * eye[None, :, None, :, None]).reshape(S5_BLOCKS, 8 * r, 8 * c)


def _prep_params(p):
    w = {}
    w["ssd_in_w"] = [p["ssd_in_w"][j, :, :ZX_DIM].astype(BF16) for j in range(2)]
    w["ssd_dt_w"] = [jnp.pad(p["ssd_in_w"][j, :, ZX_DIM:], ((0, 0), (0, LANES - N_HEADS))).astype(BF16)
                     for j in range(2)]
    pad_h = lambda v: jnp.pad(v, (0, LANES - N_HEADS)).reshape(1, LANES)
    w["dt_bias"] = [pad_h(p["ssd_dt_bias"][j]) for j in range(2)]
    w["a_log"] = [pad_h(p["ssd_A_log"][j]) for j in range(2)]
    w["d_exp"] = [jnp.repeat(p["ssd_D"][j], HEAD_DIM).reshape(1, D_INNER) for j in range(2)]
    w["norm_w"] = [p["ssd_norm_w"][j].reshape(1, D_INNER) for j in range(2)]
    w["conv_w"] = [p["ssd_conv_w"][j] for j in range(2)]
    w["conv_b"] = [p["ssd_conv_b"][j].reshape(1, CONV_DIM) for j in range(2)]
    w["ssd_out_w"] = [p["ssd_out_w"][j].astype(BF16) for j in range(2)]
    w["ffn_w_in"] = [p["ffn_w_in"][i].astype(BF16) for i in range(DEPTH)]
    w["ffn_w_out"] = [p["ffn_w_out"][i].astype(BF16) for i in range(DEPTH)]
    w["final_w"] = p["final_norm_w"].reshape(1, D_MODEL)
    w["s5"] = []
    for j in range(2):
        ab_re, ab_im, bb_re, bb_im = _s5_disc(
            p["s5_A_re"][j], p["s5_A_im"][j], p["s5_log_dt"][j],
            jnp.swapaxes(p["s5_B_re"][j], 1, 2), jnp.swapaxes(p["s5_B_im"][j], 1, 2))
        bbd = jnp.concatenate([_block_diag(bb_re), _block_diag(bb_im)], axis=2).astype(BF16)
        c_re = jnp.swapaxes(p["s5_C_re"][j], 1, 2)
        c_im = jnp.swapaxes(p["s5_C_im"][j], 1, 2)
        ccd = jnp.concatenate([_block_diag(c_re), _block_diag(c_im)], axis=1).astype(BF16)
        bro = lambda v: jnp.broadcast_to(v.reshape(1, S5_LANES), (SUBLANES, S5_LANES))
        w["s5"].append(dict(bbd=bbd, ccd=ccd, a_re=bro(ab_re), a_im=bro(ab_im),
                            dskip=p["s5_D"][j].reshape(1, D_MODEL),
                            glu_w=p["s5_glu_w"][j].astype(BF16)))
    return w


def _trunk(x, mods, ssm0, conv0, re0, im0, w, tm, s5_tb, s5_bb, ssd_batch_major):
    B, T, _ = x.shape
    assert ssd_batch_major == (T % (2 * SSD_CHUNK) == 0)
    ssm_new = jnp.zeros(ssm0.shape, ssm0.dtype)
    conv_new, re_new, im_new = [], [], []
    if not ssd_batch_major:
        x = jnp.swapaxes(x, 0, 1)
    for i in range(DEPTH):
        j = i // 2
        final = i == DEPTH - 1
        parts = [mods[i][:, k * D_MODEL:(k + 1) * D_MODEL] for k in range(6)]
        by_rows = [v[None] for v in parts]
        by_batch = [v[:, None] for v in parts]
        ffn_w = (w["ffn_w_in"][i], w["ffn_w_out"][i], w["final_w"], tm, final)
        if i % 2 == 0:
            sh1, sc1, g1, sh2, sc2, g2 = by_batch if ssd_batch_major else by_rows
            x3 = x if ssd_batch_major else x.reshape(1, T * B, D_MODEL)
            ssd_w = (w["conv_w"][j], w["conv_b"][j], w["dt_bias"][j], w["a_log"][j], w["d_exp"][j],
                     w["norm_w"][j])
            if ssd_batch_major:
                y, ssm_new, cv_new = _ssd_chunked(x3, sh1, sc1, w["ssd_in_w"][j], w["ssd_dt_w"][j],
                                                  conv0[j], ssm0, ssm_new, j, *ssd_w)
                x = _ffn(x3, y, w["ssd_out_w"][j], g1, sh2, sc2, g2, *ffn_w, relayout="to_time_major")
            else:
                zx, dtraw = _inproj(x3, sh1, sc1, w["ssd_in_w"][j], w["ssd_dt_w"][j], tm)
                y, ssm_new, cv_new = _ssd_step(zx.reshape(T, B * ZX_DIM), dtraw[0].reshape(T, B * LANES),
                                               conv0[j], ssm0, ssm_new, j, *ssd_w, B)
                x = _ffn(x3, y.reshape(1, T * B, D_INNER), w["ssd_out_w"][j], g1, sh2, sc2, g2,
                         *ffn_w).reshape(T, B, D_MODEL)
            conv_new.append(cv_new)
        else:
            s5 = w["s5"][j]
            sh1, sc1, g1 = [v[0] for v in by_rows[:3]]
            xs5, sr, si = _s5_layer(x, sh1, sc1, g1, s5["bbd"], s5["ccd"], s5["a_re"], s5["a_im"],
                                    re0[j], im0[j], s5["dskip"], s5["glu_w"], s5_tb, s5_bb)
            re_new.append(sr)
            im_new.append(si)
            if ssd_batch_major:
                x = _ffn(xs5.reshape(1, T * B, D_MODEL), None, None, None, *by_rows[3:], *ffn_w,
                         relayout="to_batch_major")
            else:
                x = _ffn(xs5.reshape(1, T * B, D_MODEL), None, None, None, *by_rows[3:],
                         *ffn_w).reshape(T, B, D_MODEL)
    if not ssd_batch_major:
        x = jnp.swapaxes(x, 0, 1)
    return x, ssm_new, jnp.stack(conv_new), jnp.stack(re_new), jnp.stack(im_new)


def kernel(x_prompt, x_sample, state_ssm, state_conv, state_s5_re, state_s5_im, c_prompt, c_sample,
           ada_w, ada_b, ssd_in_w, ssd_conv_w, ssd_conv_b, ssd_dt_bias, ssd_A_log, ssd_D, ssd_norm_w,
           ssd_out_w, s5_A_re, s5_A_im, s5_log_dt, s5_B_re, s5_B_im, s5_C_re, s5_C_im, s5_D, s5_glu_w,
           ffn_w_in, ffn_w_out, final_norm_w):
    p = dict(ssd_in_w=ssd_in_w, ssd_conv_w=ssd_conv_w, ssd_conv_b=ssd_conv_b, ssd_dt_bias=ssd_dt_bias,
             ssd_A_log=ssd_A_log, ssd_D=ssd_D, ssd_norm_w=ssd_norm_w, ssd_out_w=ssd_out_w,
             s5_A_re=s5_A_re, s5_A_im=s5_A_im, s5_log_dt=s5_log_dt, s5_B_re=s5_B_re, s5_B_im=s5_B_im,
             s5_C_re=s5_C_re, s5_C_im=s5_C_im, s5_D=s5_D, s5_glu_w=s5_glu_w,
             ffn_w_in=ffn_w_in, ffn_w_out=ffn_w_out, final_norm_w=final_norm_w)
    w = _prep_params(p)
    bp, tp, _ = x_prompt.shape
    bs, ts, _ = x_sample.shape

    mods = _ada(jnp.concatenate([c_prompt, c_sample], axis=0), ada_w, ada_b)
    mods_p = [mods[l, :bp] for l in range(DEPTH)]
    mods_s = [mods[l, bp:] for l in range(DEPTH)]

    n_ssd, n_s5 = state_ssm.shape[0], state_s5_re.shape[0]
    rows_hp = N_HEADS * HEAD_DIM
    zeros_p = (jnp.zeros((n_ssd, bp, rows_hp, D_STATE), state_ssm.dtype),
               jnp.zeros((n_ssd, bp, CONV_K - 1, CONV_DIM), state_conv.dtype),
               jnp.zeros((n_s5, bp, S5_LANES), state_s5_re.dtype),
               jnp.zeros((n_s5, bp, S5_LANES), state_s5_im.dtype))
    yp, ssm_p, conv_p, re_p, im_p = _trunk(
        x_prompt, mods_p, *zeros_p, w,
        tm=512, s5_tb=512 // bp, s5_bb=bp, ssd_batch_major=True)
    ys, ssm_s, conv_s, re_s, im_s = _trunk(
        x_sample, mods_s,
        state_ssm.reshape(n_ssd, bs, rows_hp, D_STATE), state_conv,
        state_s5_re.reshape(n_s5, bs, S5_LANES), state_s5_im.reshape(n_s5, bs, S5_LANES), w,
        tm=512, s5_tb=ts, s5_bb=512 // ts, ssd_batch_major=False)

    ssm_shape = lambda b: (n_ssd, b, N_HEADS, HEAD_DIM, D_STATE)
    s5_shape = lambda b: (n_s5, b, S5_GROUPS, S5_STATE)
    return (yp, ys,
            ssm_p.reshape(ssm_shape(bp)), conv_p, re_p.reshape(s5_shape(bp)), im_p.reshape(s5_shape(bp)),
            ssm_s.reshape(ssm_shape(bs)), conv_s, re_s.reshape(s5_shape(bs)), im_s.reshape(s5_shape(bs)))
```

```python
import functools
import math

import jax
import jax.numpy as jnp
from jax import lax
from jax.experimental import pallas as pl
from jax.experimental.pallas import tpu as pltpu

F32 = jnp.float32
BF16 = jnp.bfloat16

D_MODEL = 1024
DEPTH = 4
D_INNER = 2048
HEAD_DIM = 64
N_HEADS = 32
N_GROUPS = 8
HEADS_PER_GROUP = 4
D_STATE = 128
CONV_K = 4
BC_DIM = 2 * N_GROUPS * D_STATE
CONV_DIM = D_INNER + BC_DIM
ZX_DIM = D_INNER + CONV_DIM
SSD_CHUNK = 128
S5_GROUPS = 64
S5_GROUP = 16
S5_STATE = 64
S5_LANES = S5_GROUPS * S5_STATE
S5_BLOCKS = D_MODEL // 128
D_FF = 2816
RMS_EPS = 1e-6
LANES = 128
SUBLANES = 8
CONV_PAD = 8
N_XTILES = D_INNER // LANES
GROUP_W = HEADS_PER_GROUP * HEAD_DIM
SSD_STEP_BATCHES = 4
FFN_SPLIT = 11
S5_GLU_SPLIT = 4
S5_BU_SPLIT = 4
INPROJ_COLS = 512
VMEM_LIMIT = 52 * 1024 * 1024


def _cparams(sem):
    return pltpu.CompilerParams(dimension_semantics=sem, vmem_limit_bytes=VMEM_LIMIT)


def _sigmoid(x):
    return 1.0 / (1.0 + jnp.exp(-x))


def _silu(x):
    hx = 0.5 * x
    return hx + hx * jnp.tanh(hx)


def _rows_by_batch(x, nb):
    tm, d = x.shape
    return x.reshape(tm // nb, nb, d)


def _modulate_rows(x, sh, sc, nb, ng=1):
    if ng > 1:
        x3 = x.reshape(ng, x.shape[0] // ng, x.shape[1])
        r = lax.rsqrt(jnp.mean(x3 * x3, axis=-1, keepdims=True) + RMS_EPS)
        return ((x3 * r) * (1.0 + sc) + sh).reshape(x.shape)
    if nb == 1:
        r = lax.rsqrt(jnp.mean(x * x, axis=-1, keepdims=True) + RMS_EPS)
        return (x * r) * (1.0 + sc) + sh
    x3 = _rows_by_batch(x, nb)
    r = lax.rsqrt(jnp.mean(x3 * x3, axis=-1, keepdims=True) + RMS_EPS)
    h = (x3 * r) * (1.0 + sc[None]) + sh[None]
    return h.reshape(x.shape)


def _gate_rows(v, g, nb, ng=1):
    if ng > 1:
        return (v.reshape(ng, v.shape[0] // ng, v.shape[1]) * g).reshape(v.shape)
    if nb == 1:
        return v * g
    return (_rows_by_batch(v, nb) * g[None]).reshape(v.shape)


def _dot(a, b):
    return jnp.dot(a, b, preferred_element_type=F32)


def _dot_nt(a, b):
    return lax.dot_general(a, b, (((1,), (1,)), ((), ())), preferred_element_type=F32)


def _dot_tn(a, b):
    return lax.dot_general(a, b, (((0,), (0,)), ((), ())), preferred_element_type=F32)


def _ada_kernel(c_ref, w_ref, b_ref, o_ref):
    s = _silu(c_ref[...]).astype(BF16)
    o_ref[0] = _dot(s, w_ref[0].astype(BF16)) + b_ref[0]


def _ada(c_all, ada_w, ada_b):
    nrow = c_all.shape[0]
    tn = 1536
    return pl.pallas_call(
        _ada_kernel,
        grid=(DEPTH, 6 * D_MODEL // tn),
        in_specs=[
            pl.BlockSpec((nrow, D_MODEL), lambda l, j: (0, 0)),
            pl.BlockSpec((1, D_MODEL, tn), lambda l, j: (l, 0, j)),
            pl.BlockSpec((1, 1, tn), lambda l, j: (l, 0, j)),
        ],
        out_specs=pl.BlockSpec((1, nrow, tn), lambda l, j: (l, 0, j)),
        out_shape=jax.ShapeDtypeStruct((DEPTH, nrow, 6 * D_MODEL), F32),
        compiler_params=_cparams(("arbitrary", "arbitrary")),
        name="ada",
    )(c_all, ada_w, ada_b.reshape(DEPTH, 1, 6 * D_MODEL))


def _inproj_kernel(x_ref, sh_ref, sc_ref, w_ref, wdt_ref, zx_ref, dt_ref, *, nb):
    h = _modulate_rows(x_ref[...], sh_ref[...], sc_ref[...], nb).astype(BF16)
    zx_ref[...] = _dot(h, w_ref[...])
    dt_ref[...] = _dot(h, wdt_ref[...])


def _inproj(x3, sh, sc, w, layer, wdt, tm):
    ng, rows, _ = x3.shape
    nb = sh.shape[1]
    tn = 2048
    per_g = pl.BlockSpec((None, nb, D_MODEL), lambda j, g, i: (g, 0, 0))
    return pl.pallas_call(
        functools.partial(_inproj_kernel, nb=nb),
        grid=(ZX_DIM // tn, ng, rows // tm),
        in_specs=[
            pl.BlockSpec((None, tm, D_MODEL), lambda j, g, i: (g, i, 0)),
            per_g, per_g,
            pl.BlockSpec((None, D_MODEL, tn), lambda j, g, i: (layer, 0, j)),
            pl.BlockSpec((D_MODEL, LANES), lambda j, g, i: (0, 0)),
        ],
        out_specs=[
            pl.BlockSpec((None, tm, tn), lambda j, g, i: (g, i, j)),
            pl.BlockSpec((None, None, tm, LANES), lambda j, g, i: (j, g, i, 0)),
        ],
        out_shape=[
            jax.ShapeDtypeStruct((ng, rows, ZX_DIM), F32),
            jax.ShapeDtypeStruct((ZX_DIM // tn, ng, rows, LANES), F32),
        ],
        compiler_params=_cparams(("arbitrary", "arbitrary", "arbitrary")),
        name="ssd_inproj",
    )(x3, sh, sc, w, wdt)


def _cumsum_rows(a, L):
    rows = lax.broadcasted_iota(jnp.int32, a.shape, 0)
    k = 1
    while k < L:
        a = a + jnp.where(rows >= k, pltpu.roll(a, k, 0), 0.0)
        k *= 2
    return a


def _transpose_rows(a, L):
    if L < LANES:
        a = jnp.concatenate([a, jnp.zeros((LANES - L, LANES), F32)], axis=0)
    return a.T[:, :L]


def _ssd_step_kernel(zx_ref, dt_ref, conv0_ref, ssm0_ref,
                     cw_ref, cb_ref, dtb_ref, alog_ref, dexp_ref, nw_ref, gsum_ref, expand_ref,
                     *rest, L, nbatch):
    y_ref, ssm_out_ref, conv_out_ref, pad, xbc_scr = rest[-5:]
    lo = CONV_PAD - (CONV_K - 1)
    row_t = lax.broadcasted_iota(jnp.int32, (L, LANES), 0)
    b_lo, c_lo = D_INNER, D_INNER + N_GROUPS * D_STATE

    def row_bcast(i, s, lt0, lt1):
        return jnp.concatenate([jnp.broadcast_to(xbc_scr[i, lt, s:s + 1, :], (L, LANES))
                                for lt in range(lt0, lt1)], axis=1)

    xbcs, dts, acss, prods = [], [], [], []
    for i in range(nbatch):
        z0 = i * ZX_DIM
        conv_tiles = []
        for lt in range(2 * N_XTILES):
            cols = slice(lt * LANES, (lt + 1) * LANES)
            pad[i, lt, lo:CONV_PAD, :] = conv0_ref[i, :, cols]
            pad[i, lt, CONV_PAD:CONV_PAD + L, :] = zx_ref[:, z0 + D_INNER + lt * LANES:z0 + D_INNER + (lt + 1) * LANES]
            acc = cb_ref[:, cols]
            for k in range(CONV_K):
                acc = acc + cw_ref[k:k + 1, cols] * pad[i, lt, lo + k:lo + k + L, :]
            conv_tiles.append(_silu(acc))
            xbc_scr[i, lt] = conv_tiles[lt]
            conv_out_ref[i, :, cols] = pad[i, lt, lo + L:CONV_PAD + L, :]
        xbc = jnp.concatenate(conv_tiles, axis=1)
        dtr = dt_ref[:, i * LANES:(i + 1) * LANES] + dtb_ref[...]
        dt = jnp.maximum(dtr, 0.0) + jnp.log1p(jnp.exp(-jnp.abs(dtr)))
        xbcs.append(xbc)
        dts.append(dt)
        acss.append(_cumsum_rows(dt * (-jnp.exp(alog_ref[...])), L))
        cm = xbc[:, c_lo:]
        prods += [cm * row_bcast(i, s, b_lo // LANES, c_lo // LANES) for s in range(L)]

    p_hi, p_lo = _split_bf16(jnp.concatenate(prods, axis=0))
    cbh = _dot(jnp.concatenate([p_hi, p_lo], axis=1).astype(BF16), gsum_ref[...])

    per = L * L + 2 * L
    rows = []
    for i in range(nbatch):
        dt, acs = dts[i], acss[i]
        for s in range(L):
            decay = jnp.exp(jnp.where(row_t >= s, acs - acs[s:s + 1, :], -jnp.inf))
            r0 = (i * L + s) * L
            rows.append(cbh[r0:r0 + L, :] * decay * dt[s:s + 1, :])
        rows += [jnp.exp(acs), jnp.exp(acs[L - 1:L, :] - acs) * dt]
    w_hi, w_lo = _split_bf16(jnp.concatenate(rows, axis=0))
    wide = _dot(jnp.concatenate([w_hi, w_lo], axis=1).astype(BF16), expand_ref[...])

    for i in range(nbatch):
        z0, w0 = i * ZX_DIM, i * per
        xbc = xbcs[i]
        xs = xbc[:, :D_INNER]
        y = dexp_ref[...] * xs
        for s in range(L):
            y = y + wide[w0 + s * L:w0 + (s + 1) * L, :] * row_bcast(i, s, 0, N_XTILES)
        ea_x = wide[w0 + L * L:w0 + L * L + L, :]
        xw = xs * wide[w0 + L * L + L:w0 + per, :]
        e_last = jnp.exp(acss[i][L - 1:L, :])
        for g in range(N_GROUPS):
            gl = slice(g * GROUP_W, (g + 1) * GROUP_W)
            bg = xbc[:, b_lo + g * D_STATE:b_lo + (g + 1) * D_STATE].astype(BF16)
            cg = xbc[:, c_lo + g * D_STATE:c_lo + (g + 1) * D_STATE].astype(BF16)
            sg = ssm0_ref[i, gl, :]
            yg = y[:, gl] + ea_x[:, gl] * _dot_nt(cg, sg.astype(BF16))
            upd = _dot_tn(xw[:, gl].astype(BF16), bg)
            for r in range(HEADS_PER_GROUP):
                h = g * HEADS_PER_GROUP + r
                rr = slice(r * HEAD_DIM, (r + 1) * HEAD_DIM)
                ssm_out_ref[i, g * GROUP_W + r * HEAD_DIM:g * GROUP_W + (r + 1) * HEAD_DIM, :] = (
                    e_last[:, h:h + 1] * sg[rr, :] + upd[rr, :])
            yg = yg * _silu(zx_ref[:, z0 + g * GROUP_W:z0 + (g + 1) * GROUP_W])
            yg = yg * lax.rsqrt(jnp.mean(yg * yg, axis=-1, keepdims=True) + RMS_EPS)
            y0 = i * D_INNER + g * GROUP_W
            y_ref[:, y0:y0 + GROUP_W] = (yg * nw_ref[:, gl]).astype(y_ref.dtype)


def _split_bf16(v):
    hi = v.astype(BF16).astype(F32)
    return hi, (v - hi).astype(BF16).astype(F32)


def _ssd_chunk_kernel(x_ref, xn_ref, sh_ref, sc_ref, win_ref, wdt_ref, conv0_ref, ssm0_ref,
                      cw_ref, cb_ref, dtb_ref, alog_ref, dexp_ref, nw_ref, expand_ref, *rest, L):
    y_ref, ssm_out_ref, conv_out_ref, pad, state_t, z_scr, dt_scr = rest[-7:]
    c = pl.program_id(1)
    lo = CONV_PAD - (CONV_K - 1)

    def projection_steps(x_rows, slot):
        h = _modulate_rows(x_rows, sh_ref[...], sc_ref[...], 1).astype(BF16)
        pw = INPROJ_COLS

        def z_step(q):
            z_scr[slot, :, q * pw:(q + 1) * pw] = _dot(h, win_ref[:, q * pw:(q + 1) * pw])

        def xbc_step(q):
            res = _dot(h, win_ref[:, D_INNER + q * pw:D_INNER + (q + 1) * pw])
            for r in range(pw // LANES):
                pad[slot, q * (pw // LANES) + r, CONV_PAD:CONV_PAD + L, :] = res[:, r * LANES:(r + 1) * LANES]

        def dt_step():
            dt_scr[slot] = _dot(h, wdt_ref[...])

        return ([functools.partial(xbc_step, q) for q in range(CONV_DIM // pw)]
                + [functools.partial(z_step, q) for q in range(D_INNER // pw)] + [dt_step])

    @pl.when(c == 0)
    def _():
        state_t[...] = ssm0_ref[...].T
        for lt in range(2 * N_XTILES):
            pad[0, lt, lo:CONV_PAD, :] = conv0_ref[0, :, lt * LANES:(lt + 1) * LANES]
        for step in projection_steps(x_ref[:L, :], 0):
            step()

    body = functools.partial(_ssd_chunk_body, z_scr=z_scr, dt_scr=dt_scr, pad=pad, state_t=state_t,
                             cw_ref=cw_ref, cb_ref=cb_ref, dtb_ref=dtb_ref, alog_ref=alog_ref,
                             dexp_ref=dexp_ref, nw_ref=nw_ref, expand_ref=expand_ref, y_ref=y_ref, L=L)
    body(0, 1, slice(0, L), projection_steps(x_ref[L:, :], 1))
    body(1, 0, slice(L, 2 * L), projection_steps(xn_ref[...], 0))
    for lt in range(2 * N_XTILES):
        conv_out_ref[0, :, lt * LANES:(lt + 1) * LANES] = pad[0, lt, lo:CONV_PAD, :]

    @pl.when(c == pl.num_programs(1) - 1)
    def _():
        ssm_out_ref[...] = state_t[...].T


def _ssd_chunk_body(slot, next_slot, rows, side_steps, *, z_scr, dt_scr, pad, state_t, cw_ref, cb_ref,
                    dtb_ref, alog_ref, dexp_ref, nw_ref, expand_ref, y_ref, L):
    lo = CONV_PAD - (CONV_K - 1)
    assert len(side_steps) <= 2 * N_XTILES // 4 + N_GROUPS
    pending = iter(side_steps)

    def side_point():
        step = next(pending, None)
        if step is not None:
            step()

    def conv_tile(lt):
        cols = slice(lt * LANES, (lt + 1) * LANES)
        acc = cb_ref[:, cols]
        for k in range(CONV_K):
            acc = acc + cw_ref[k:k + 1, cols] * pad[slot, lt, lo + k:lo + k + L, :]
        return _silu(acc)

    tiles = []
    for lt in range(2 * N_XTILES):
        tiles.append(conv_tile(lt))
        if lt % 4 == 3:
            side_point()

    for lt in range(2 * N_XTILES):
        pad[next_slot, lt, lo:CONV_PAD, :] = pad[slot, lt, lo + L:CONV_PAD + L, :]

    dtr = dt_scr[slot] + dtb_ref[...]
    dt = jnp.maximum(dtr, 0.0) + jnp.log1p(jnp.exp(-jnp.abs(dtr)))
    a = dt * (-jnp.exp(alog_ref[...]))
    acs = _cumsum_rows(a, L)
    acs_t = acs.T
    dt_t = dt.T
    a_last = acs[L - 1:L, :]
    ea_hi, ea_lo = _split_bf16(jnp.exp(acs))
    we_hi, we_lo = _split_bf16(jnp.exp(a_last - acs) * dt)
    parts = jnp.concatenate([jnp.concatenate([ea_hi, ea_lo], axis=1),
                             jnp.concatenate([we_hi, we_lo], axis=1)], axis=0).astype(BF16)
    wide = _dot(parts, expand_ref[...])
    ea_x = wide[:L]
    we_x = wide[L:]

    causal = (lax.broadcasted_iota(jnp.int32, (L, L), 0)
              >= lax.broadcasted_iota(jnp.int32, (L, L), 1))
    lane_head = lax.broadcasted_iota(jnp.int32, (L, GROUP_W), 1) // HEAD_DIM

    for g in range(N_GROUPS):
        gl = slice(g * GROUP_W, (g + 1) * GROUP_W)
        bg = tiles[N_XTILES + g].astype(BF16)
        cg = tiles[N_XTILES + N_GROUPS + g].astype(BF16)
        cb = _dot_nt(cg, bg)
        sg = state_t[:, gl]
        y_off = _dot(cg, sg.astype(BF16))
        ms = []
        for r in range(HEADS_PER_GROUP):
            h = g * HEADS_PER_GROUP + r
            seg = acs[:, h:h + 1] - acs_t[h:h + 1, :]
            decay = jnp.exp(jnp.where(causal, seg, -jnp.inf))
            ms.append((cb * decay * dt_t[h:h + 1, :]).astype(BF16))
        xg = jnp.concatenate([tiles[2 * g], tiles[2 * g + 1]], axis=1)
        xgb = xg.astype(BF16)
        xblk = jnp.concatenate([jnp.where(lane_head == r, xgb, jnp.zeros_like(xgb))
                                for r in range(HEADS_PER_GROUP)], axis=0)
        y_diag = _dot(jnp.concatenate(ms, axis=1), xblk)
        yg = y_diag + ea_x[:, gl] * y_off + dexp_ref[:, gl] * xg
        upd = _dot_tn(bg, (xg * we_x[:, gl]).astype(BF16))
        state_t[:, gl] = ea_x[L - 1:L, gl] * sg + upd
        yg = yg * _silu(z_scr[slot, :, gl])
        yg = yg * lax.rsqrt(jnp.mean(yg * yg, axis=-1, keepdims=True) + RMS_EPS)
        y_ref[rows, gl] = (yg * nw_ref[:, gl]).astype(y_ref.dtype)
        side_point()


def _ssm_state_spec(grid_rank, j, nb):
    shape = (None, nb, N_HEADS * HEAD_DIM, D_STATE)
    if grid_rank == 2:
        return pl.BlockSpec(shape, lambda b, c: (j, b, 0, 0))
    return pl.BlockSpec(shape, lambda b: (j, b, 0, 0))


def _ssd_call(body, grid, in_specs, args, y_spec, y_shape, conv_spec, ssm_prev, conv_prev, j, nb, scratch, sem):
    in_specs = list(in_specs) + [pl.BlockSpec(memory_space=pl.ANY)] * 2
    args = list(args) + [ssm_prev, conv_prev]
    aliases = {len(args) - 2: 1, len(args) - 1: 2}
    return pl.pallas_call(
        body,
        grid=grid,
        in_specs=in_specs,
        out_specs=[y_spec, _ssm_state_spec(len(grid), j, nb), conv_spec],
        out_shape=[
            jax.ShapeDtypeStruct(y_shape, BF16),
            jax.ShapeDtypeStruct(ssm_prev.shape, F32),
            jax.ShapeDtypeStruct(conv_prev.shape, F32),
        ],
        scratch_shapes=scratch,
        input_output_aliases=aliases,
        compiler_params=_cparams(sem),
        name="ssd_core",
    )(*args)


def _ssd_chunked(x, sh, sc, w_in, w_dt, conv_all, ssm_all, ssm_prev, conv_prev, j, cw, cb, dtb, alog, dexp, nw):
    B, T, _ = x.shape
    L = SSD_CHUNK
    nsteps = T // (2 * L)
    last_chunk = T // L - 1
    full = lambda shape: pl.BlockSpec(shape, lambda b, c: (0,) * len(shape))
    resident = lambda shape: pl.BlockSpec(shape, lambda b, c: (0,) * len(shape), pipeline_mode=pl.Buffered(1))
    pair = lambda w: pl.BlockSpec((None, 2 * L, w), lambda b, c: (b, c, 0))
    nxt = pl.BlockSpec((None, L, D_MODEL), lambda b, c: (b, jnp.minimum(2 * c + 2, last_chunk), 0))
    per_b = pl.BlockSpec((None, 1, D_MODEL), lambda b, c: (b, 0, 0))
    conv_spec = pl.BlockSpec((None, 1, CONV_K - 1, CONV_DIM), lambda b, c: (j, b, 0, 0))
    w_in_spec = pl.BlockSpec((None, D_MODEL, ZX_DIM), lambda b, c: (j, 0, 0), pipeline_mode=pl.Buffered(1))
    sel = (jnp.arange(D_INNER)[None, :] // HEAD_DIM == jnp.arange(LANES)[:, None]).astype(BF16)
    in_specs = [
        pair(D_MODEL), nxt, per_b, per_b, w_in_spec, resident((D_MODEL, LANES)),
        conv_spec, _ssm_state_spec(2, j, None),
        full((CONV_K, CONV_DIM)), full((1, CONV_DIM)), full((1, LANES)), full((1, LANES)),
        full((1, D_INNER)), full((1, D_INNER)), resident((2 * LANES, D_INNER)),
    ]
    args = [x, x, sh, sc, w_in, w_dt, conv_all, ssm_all, cw, cb, dtb, alog, dexp, nw,
            jnp.concatenate([sel, sel], axis=0)]
    scratch = [
        pltpu.VMEM((2, 2 * N_XTILES, CONV_PAD + L, LANES), F32),
        pltpu.VMEM((D_STATE, N_HEADS * HEAD_DIM), F32),
        pltpu.VMEM((2, L, D_INNER), F32),
        pltpu.VMEM((2, L, LANES), F32),
    ]
    return _ssd_call(functools.partial(_ssd_chunk_kernel, L=L), (B, nsteps), in_specs, args,
                     pair(D_INNER), (B, T, D_INNER), conv_spec, ssm_prev, conv_prev, j, None, scratch,
                     ("arbitrary", "arbitrary"))


def _ssd_step(zx, dtraw, conv_all, ssm_all, ssm_prev, conv_prev, j, cw, cb, dtb, alog, dexp, nw, B):
    T = zx.shape[0]
    nb = SSD_STEP_BATCHES
    full = lambda shape: pl.BlockSpec(shape, lambda b: (0,) * len(shape))
    col = lambda w: pl.BlockSpec((T, nb * w), lambda b: (0, b))
    conv_spec = pl.BlockSpec((None, nb, CONV_K - 1, CONV_DIM), lambda b: (j, b, 0, 0))
    in_specs = [
        col(ZX_DIM), col(LANES), conv_spec, _ssm_state_spec(1, j, nb),
        full((CONV_K, CONV_DIM)), full((1, CONV_DIM)), full((1, LANES)), full((1, LANES)),
        full((1, D_INNER)), full((1, D_INNER)),
        full((2 * N_GROUPS * D_STATE, LANES)), full((2 * LANES, D_INNER)),
    ]
    head_ids = jnp.arange(LANES)
    gsum = ((jnp.arange(N_GROUPS * D_STATE)[:, None] // D_STATE == head_ids[None, :] // HEADS_PER_GROUP)
            & (head_ids[None, :] < N_HEADS)).astype(BF16)
    sel = (jnp.arange(D_INNER)[None, :] // HEAD_DIM == head_ids[:, None]).astype(BF16)
    args = [zx, dtraw, conv_all, ssm_all, cw, cb, dtb, alog, dexp, nw,
            jnp.concatenate([gsum, gsum], axis=0), jnp.concatenate([sel, sel], axis=0)]
    scratch = [pltpu.VMEM((nb, 2 * N_XTILES, CONV_PAD + T, LANES), F32),
               pltpu.VMEM((nb, 2 * N_XTILES, T, LANES), F32)]
    return _ssd_call(functools.partial(_ssd_step_kernel, L=T, nbatch=nb), (B // nb,), in_specs, args,
                     col(D_INNER), (T, B * D_INNER), conv_spec, ssm_prev, conv_prev, j, nb, scratch,
                     ("arbitrary",))


def _ffn_kernel(*refs, nb, ng, has_proj, final, relayout):
    if relayout:
        refs, relay = refs[:-1], refs[-1]
    if has_proj:
        x_ref, y_ref, ow_ref, g1_ref, sh_ref, sc_ref, g2_ref, win_ref, wo_ref, fw_ref, o_ref = refs
    else:
        x_ref, sh_ref, sc_ref, g2_ref, win_ref, wo_ref, fw_ref, o_ref = refs
    x = x_ref[...]
    tm = x.size // D_MODEL
    x = x.reshape(tm, D_MODEL)
    if has_proj:
        y = y_ref[...].reshape(tm, D_INNER)
        x = x + _gate_rows(_dot(y, ow_ref[...]), g1_ref[...], nb, ng)
    h = _modulate_rows(x, sh_ref[...], sc_ref[...], nb, ng).astype(BF16)
    tf = D_FF // FFN_SPLIT
    acc = None
    for c in range(FFN_SPLIT):
        gate = _dot(h, win_ref[:, c * tf:(c + 1) * tf])
        up = _dot(h, win_ref[:, D_FF + c * tf:D_FF + (c + 1) * tf])
        part = _dot((_silu(gate) * up).astype(BF16), wo_ref[c * tf:(c + 1) * tf, :])
        acc = part if acc is None else acc + part
    x2 = x + _gate_rows(acc, g2_ref[...], nb, ng)
    if final:
        x2 = x2 * lax.rsqrt(jnp.mean(x2 * x2, axis=-1, keepdims=True) + RMS_EPS) * fw_ref[...]
    n_lt = D_MODEL // LANES
    if relayout == "to_batch_major":
        for lt in range(n_lt):
            relay[lt] = x2[:, lt * LANES:(lt + 1) * LANES]
        for b in range(nb):
            o_ref[b] = jnp.concatenate([relay[lt, pl.ds(b, tm // nb, stride=nb), :] for lt in range(n_lt)], axis=1)
    elif relayout == "to_time_major":
        tg = tm // ng
        for lt in range(n_lt):
            for g in range(ng):
                relay[lt, pl.ds(g, tg, stride=ng), :] = x2[g * tg:(g + 1) * tg, lt * LANES:(lt + 1) * LANES]
        o_ref[...] = jnp.concatenate([relay[lt] for lt in range(n_lt)], axis=1).reshape(o_ref.shape)
    else:
        o_ref[...] = x2.reshape(o_ref.shape)


def _ffn(x3, y3, out_w, proj_layer, g1, sh, sc, g2, w_in, w_out, layer, fw, tm, final, relayout=None):
    ng, rows, _ = x3.shape
    nb = sh.shape[1]
    has_proj = y3 is not None
    resident = lambda shape, l: pl.BlockSpec((None,) + shape, lambda g, i: (l,) + (0,) * len(shape),
                                             pipeline_mode=pl.Buffered(1))
    if relayout == "to_time_major":
        assert nb == 1
        tg = tm // ng
        row_blk = lambda w: pl.BlockSpec((ng, tg, w), lambda g, i: (0, i, 0))
        per_g = pl.BlockSpec((ng, 1, D_MODEL), lambda g, i: (0, 0, 0))
        grid = (1, rows // tg)
        out_spec = pl.BlockSpec((tg, ng, D_MODEL), lambda g, i: (i, 0, 0))
        out_shape = (rows, ng, D_MODEL)
        kernel_ng = ng
    else:
        row_blk = lambda w: pl.BlockSpec((None, tm, w), lambda g, i: (g, i, 0))
        per_g = pl.BlockSpec((None, nb, D_MODEL), lambda g, i: (g, 0, 0))
        grid = (ng, rows // tm)
        kernel_ng = 1
        if relayout == "to_batch_major":
            assert ng == 1
            out_spec = pl.BlockSpec((nb, tm // nb, D_MODEL), lambda g, i: (0, i, 0))
            out_shape = (nb, rows // nb, D_MODEL)
        else:
            out_spec = row_blk(D_MODEL)
            out_shape = (ng, rows, D_MODEL)
    in_specs = [row_blk(D_MODEL)]
    args = [x3]
    if has_proj:
        in_specs += [row_blk(D_INNER), resident((D_INNER, D_MODEL), proj_layer), per_g]
        args += [y3, out_w, g1]
    in_specs += [per_g, per_g, per_g,
                 resident((D_MODEL, 2 * D_FF), layer), resident((D_FF, D_MODEL), layer),
                 pl.BlockSpec((1, D_MODEL), lambda g, i: (0, 0))]
    args += [sh, sc, g2, w_in, w_out, fw]
    return pl.pallas_call(
        functools.partial(_ffn_kernel, nb=nb, ng=kernel_ng, has_proj=has_proj, final=final, relayout=relayout),
        grid=grid,
        in_specs=in_specs,
        out_specs=out_spec,
        out_shape=jax.ShapeDtypeStruct(out_shape, F32),
        scratch_shapes=[pltpu.VMEM((D_MODEL // LANES, tm, LANES), F32)] if relayout else [],
        compiler_params=_cparams(("arbitrary", "arbitrary")),
        name="proj_ffn" if has_proj else "ffn",
    )(*args)


def _s5_disc_kernel(are_ref, aim_ref, ldt_ref, bre_ref, bim_ref, abre_ref, abim_ref, bbre_ref, bbim_ref):
    lre, lim = are_ref[...], aim_ref[...]
    dt = jnp.exp(ldt_ref[...])
    mag = jnp.exp(lre * dt)
    ab_re, ab_im = mag * jnp.cos(lim * dt), mag * jnp.sin(lim * dt)
    den = lre * lre + lim * lim
    nr, ni = ab_re - 1.0, ab_im
    q_re = (nr * lre + ni * lim) / den
    q_im = (ni * lre - nr * lim) / den
    abre_ref[...] = ab_re
    abim_ref[...] = ab_im
    br, bi = bre_ref[...], bim_ref[...]
    bbre_ref[...] = q_re[:, None, :] * br - q_im[:, None, :] * bi
    bbim_ref[...] = q_re[:, None, :] * bi + q_im[:, None, :] * br


def _s5_disc(a_re, a_im, log_dt, b_re, b_im):
    gn = jax.ShapeDtypeStruct((S5_GROUPS, S5_STATE), F32)
    gin = jax.ShapeDtypeStruct((S5_GROUPS, S5_GROUP, S5_STATE), F32)
    return pl.pallas_call(
        _s5_disc_kernel,
        out_shape=[gn, gn, gin, gin],
        name="s5_disc",
    )(a_re, a_im, log_dt.reshape(S5_GROUPS, 1), b_re, b_im)


def _s5_kernel(x_ref, sh_ref, sc_ref, g1_ref, bb_ref, cc_ref, are_ref, aim_ref, s0re_ref, s0im_ref,
               dskip_ref, glu_ref, o_ref, sre_ref, sim_ref, xs_re, xs_im, st_re, st_im, *, tb, bb):
    t_idx = pl.program_id(1)
    rows = tb * bb

    @pl.when(t_idx == 0)
    def _():
        st_re[...] = s0re_ref[...]
        st_im[...] = s0im_ref[...]

    x = x_ref[...].reshape(rows, D_MODEL)
    h = _modulate_rows(x, sh_ref[...], sc_ref[...], bb)
    hb = h.astype(BF16)
    half = S5_LANES // S5_BLOCKS
    pw = 2 * half // S5_BU_SPLIT

    def bu_pieces(k):
        out = [None] * S5_BU_SPLIT

        def piece(j):
            out[j] = _dot(hb[:, k * LANES:(k + 1) * LANES], bb_ref[k, :, j * pw:(j + 1) * pw])

        return out, [functools.partial(piece, j) for j in range(S5_BU_SPLIT)]

    def y_pieces(k):
        out = [None] * 2

        def piece(j):
            src = xs_im if j else xs_re
            out[j] = _dot(src[:, k * half:(k + 1) * half].astype(BF16), cc_ref[k, j * half:(j + 1) * half, :])

        return out, [functools.partial(piece, j) for j in range(2)]

    n_iter = (bb // SUBLANES) * tb
    bu_next, steps = bu_pieces(0)
    for step in steps:
        step()
    y_parts = [None] * S5_BLOCKS
    for k in range(S5_BLOCKS):
        ks = slice(k * half, (k + 1) * half)
        bu = jnp.concatenate(bu_next, axis=1)
        side = []
        if k + 1 < S5_BLOCKS:
            bu_next, steps = bu_pieces(k + 1)
            side += steps
        if k > 0:
            y_parts[k - 1], steps = y_pieces(k - 1)
            side += steps
        run_at = {((j + 1) * n_iter) // (len(side) + 1): step for j, step in enumerate(side)}
        ar, ai = are_ref[:, ks], aim_ref[:, ks]
        it = 0
        for rg in range(bb // SUBLANES):
            rs = slice(rg * SUBLANES, (rg + 1) * SUBLANES)
            xr, xi = st_re[rs, ks], st_im[rs, ks]
            for t in range(tb):
                sl = slice(t * bb + rg * SUBLANES, t * bb + (rg + 1) * SUBLANES)
                xr, xi = (ar * xr - ai * xi + bu[sl, :half], ar * xi + ai * xr + bu[sl, half:])
                xs_re[sl, ks] = xr
                xs_im[sl, ks] = xi
                it += 1
                if it in run_at:
                    run_at[it]()
            st_re[rs, ks] = xr
            st_im[rs, ks] = xi
    y_parts[S5_BLOCKS - 1], steps = y_pieces(S5_BLOCKS - 1)
    for step in steps:
        step()
    y = jnp.concatenate([p[0] - p[1] for p in y_parts], axis=1) + dskip_ref[...] * h
    y = 0.5 * y * (1.0 + jnp.tanh(math.sqrt(2.0 / math.pi) * (y + 0.044715 * (y * y * y))))
    yb = y.astype(BF16)
    cw = D_MODEL // S5_GLU_SPLIT
    out = jnp.concatenate(
        [_dot(yb, glu_ref[:, c * cw:(c + 1) * cw])
         * _sigmoid(_dot(yb, glu_ref[:, D_MODEL + c * cw:D_MODEL + (c + 1) * cw]))
         for c in range(S5_GLU_SPLIT)], axis=1)
    o_ref[...] = (x + _gate_rows(out, g1_ref[...], bb)).reshape(tb, bb, D_MODEL)

    @pl.when(t_idx == pl.num_programs(1) - 1)
    def _():
        sre_ref[...] = st_re[...]
        sim_ref[...] = st_im[...]


def _s5_layer(x3, sh, sc, g1, bbd, ccd, a_re, a_im, s0re, s0im, dskip, glu_w, tb, bb):
    T, B, _ = x3.shape
    per_b = pl.BlockSpec((bb, D_MODEL), lambda i, t: (i, 0))
    st = pl.BlockSpec((bb, S5_LANES), lambda i, t: (i, 0))
    full = lambda shape: pl.BlockSpec(shape, lambda i, t: (0,) * len(shape))
    return pl.pallas_call(
        functools.partial(_s5_kernel, tb=tb, bb=bb),
        grid=(B // bb, T // tb),
        in_specs=[
            pl.BlockSpec((tb, bb, D_MODEL), lambda i, t: (t, i, 0)),
            per_b, per_b, per_b,
            full((S5_BLOCKS, LANES, 2 * S5_LANES // S5_BLOCKS)),
            full((S5_BLOCKS, 2 * S5_LANES // S5_BLOCKS, LANES)),
            full((SUBLANES, S5_LANES)),
            full((SUBLANES, S5_LANES)),
            st, st,
            full((1, D_MODEL)),
            full((D_MODEL, 2 * D_MODEL)),
        ],
        out_specs=[pl.BlockSpec((tb, bb, D_MODEL), lambda i, t: (t, i, 0)), st, st],
        out_shape=[
            jax.ShapeDtypeStruct((T, B, D_MODEL), F32),
            jax.ShapeDtypeStruct((B, S5_LANES), F32),
            jax.ShapeDtypeStruct((B, S5_LANES), F32),
        ],
        scratch_shapes=[
            pltpu.VMEM((tb * bb, S5_LANES), F32),
            pltpu.VMEM((tb * bb, S5_LANES), F32),
            pltpu.VMEM((bb, S5_LANES), F32),
            pltpu.VMEM((bb, S5_LANES), F32),
        ],
        compiler_params=_cparams(("arbitrary", "arbitrary")),
        name="s5_layer",
    )(x3, sh, sc, g1, bbd, ccd, a_re, a_im, s0re, s0im, dskip, glu_w)


def _block_diag(w):
    _, r, c = w.shape
    w = w.reshape(S5_BLOCKS, 8, r, c)
    eye = jnp.eye(8, dtype=w.dtype)
    return (w[:, :, :, None, :] * eye[None, :, None, :, None]).reshape(S5_BLOCKS, 8 * r, 8 * c)


def _prep_params(p):
    w = {}
    w["ssd_in_w"] = p["ssd_in_w"].astype(BF16)
    w["ssd_dt_w"] = [jnp.pad(p["ssd_in_w"][j, :, ZX_DIM:], ((0, 0), (0, LANES - N_HEADS))).astype(BF16)
                     for j in range(2)]
    pad_h = lambda v: jnp.pad(v, (0, LANES - N_HEADS)).reshape(1, LANES)
    w["dt_bias"] = [pad_h(p["ssd_dt_bias"][j]) for j in range(2)]
    w["a_log"] = [pad_h(p["ssd_A_log"][j]) for j in range(2)]
    w["d_exp"] = [jnp.repeat(p["ssd_D"][j], HEAD_DIM).reshape(1, D_INNER) for j in range(2)]
    w["norm_w"] = [p["ssd_norm_w"][j].reshape(1, D_INNER) for j in range(2)]
    w["conv_w"] = [p["ssd_conv_w"][j] for j in range(2)]
    w["conv_b"] = [p["ssd_conv_b"][j].reshape(1, CONV_DIM) for j in range(2)]
    w["ssd_out_w"] = p["ssd_out_w"].astype(BF16)
    w["ffn_w_in"] = p["ffn_w_in"].astype(BF16)
    w["ffn_w_out"] = p["ffn_w_out"].astype(BF16)
    w["final_w"] = p["final_norm_w"].reshape(1, D_MODEL)
    w["s5"] = []
    for j in range(2):
        ab_re, ab_im, bb_re, bb_im = _s5_disc(
            p["s5_A_re"][j], p["s5_A_im"][j], p["s5_log_dt"][j],
            jnp.swapaxes(p["s5_B_re"][j], 1, 2), jnp.swapaxes(p["s5_B_im"][j], 1, 2))
        bbd = jnp.concatenate([_block_diag(bb_re), _block_diag(bb_im)], axis=2).astype(BF16)
        c_re = jnp.swapaxes(p["s5_C_re"][j], 1, 2)
        c_im = jnp.swapaxes(p["s5_C_im"][j], 1, 2)
        ccd = jnp.concatenate([_block_diag(c_re), _block_diag(c_im)], axis=1).astype(BF16)
        bro = lambda v: jnp.broadcast_to(v.reshape(1, S5_LANES), (SUBLANES, S5_LANES))
        w["s5"].append(dict(bbd=bbd, ccd=ccd, a_re=bro(ab_re), a_im=bro(ab_im),
                            dskip=p["s5_D"][j].reshape(1, D_MODEL),
                            glu_w=p["s5_glu_w"][j].astype(BF16)))
    return w


def _trunk(x, mods, ssm0, conv0, re0, im0, w, tm, s5_tb, s5_bb, ssd_batch_major):
    B, T, _ = x.shape
    assert ssd_batch_major == (T % (2 * SSD_CHUNK) == 0)
    ssm_new = jnp.zeros(ssm0.shape, ssm0.dtype)
    conv_new = jnp.zeros(conv0.shape, conv0.dtype)
    re_new, im_new = [], []
    if not ssd_batch_major:
        x = jnp.swapaxes(x, 0, 1)
    for i in range(DEPTH):
        j = i // 2
        final = i == DEPTH - 1
        parts = [mods[i][:, k * D_MODEL:(k + 1) * D_MODEL] for k in range(6)]
        by_rows = [v[None] for v in parts]
        by_batch = [v[:, None] for v in parts]
        ffn_w = (w["ffn_w_in"], w["ffn_w_out"], i, w["final_w"], tm, final)
        if i % 2 == 0:
            sh1, sc1, g1, sh2, sc2, g2 = by_batch if ssd_batch_major else by_rows
            x3 = x if ssd_batch_major else x.reshape(1, T * B, D_MODEL)
            ssd_w = (w["conv_w"][j], w["conv_b"][j], w["dt_bias"][j], w["a_log"][j], w["d_exp"][j],
                     w["norm_w"][j])
            if ssd_batch_major:
                y, ssm_new, conv_new = _ssd_chunked(x3, sh1, sc1, w["ssd_in_w"], w["ssd_dt_w"][j],
                                                    conv0, ssm0, ssm_new, conv_new, j, *ssd_w)
                x = _ffn(x3, y, w["ssd_out_w"], j, g1, sh2, sc2, g2, *ffn_w, relayout="to_time_major")
            else:
                zx, dtraw = _inproj(x3, sh1, sc1, w["ssd_in_w"], j, w["ssd_dt_w"][j], tm)
                y, ssm_new, conv_new = _ssd_step(zx.reshape(T, B * ZX_DIM), dtraw[0].reshape(T, B * LANES),
                                                 conv0, ssm0, ssm_new, conv_new, j, *ssd_w, B)
                x = _ffn(x3, y.reshape(1, T * B, D_INNER), w["ssd_out_w"], j, g1, sh2, sc2, g2,
                         *ffn_w).reshape(T, B, D_MODEL)
        else:
            s5 = w["s5"][j]
            sh1, sc1, g1 = [v[0] for v in by_rows[:3]]
            xs5, sr, si = _s5_layer(x, sh1, sc1, g1, s5["bbd"], s5["ccd"], s5["a_re"], s5["a_im"],
                                    re0[j], im0[j], s5["dskip"], s5["glu_w"], s5_tb, s5_bb)
            re_new.append(sr)
            im_new.append(si)
            if ssd_batch_major:
                x = _ffn(xs5.reshape(1, T * B, D_MODEL), None, None, None, None, *by_rows[3:], *ffn_w,
                         relayout="to_batch_major")
            else:
                x = _ffn(xs5.reshape(1, T * B, D_MODEL), None, None, None, None, *by_rows[3:],
                         *ffn_w).reshape(T, B, D_MODEL)
    if not ssd_batch_major:
        x = jnp.swapaxes(x, 0, 1)
    return x, ssm_new, conv_new, jnp.stack(re_new), jnp.stack(im_new)


def kernel(x_prompt, x_sample, state_ssm, state_conv, state_s5_re, state_s5_im, c_prompt, c_sample,
           ada_w, ada_b, ssd_in_w, ssd_conv_w, ssd_conv_b, ssd_dt_bias, ssd_A_log, ssd_D, ssd_norm_w,
           ssd_out_w, s5_A_re, s5_A_im, s5_log_dt, s5_B_re, s5_B_im, s5_C_re, s5_C_im, s5_D, s5_glu_w,
           ffn_w_in, ffn_w_out, final_norm_w):
    p = dict(ssd_in_w=ssd_in_w, ssd_conv_w=ssd_conv_w, ssd_conv_b=ssd_conv_b, ssd_dt_bias=ssd_dt_bias,
             ssd_A_log=ssd_A_log, ssd_D=ssd_D, ssd_norm_w=ssd_norm_w, ssd_out_w=ssd_out_w,
             s5_A_re=s5_A_re, s5_A_im=s5_A_im, s5_log_dt=s5_log_dt, s5_B_re=s5_B_re, s5_B_im=s5_B_im,
             s5_C_re=s5_C_re, s5_C_im=s5_C_im, s5_D=s5_D, s5_glu_w=s5_glu_w,
             ffn_w_in=ffn_w_in, ffn_w_out=ffn_w_out, final_norm_w=final_norm_w)
    w = _prep_params(p)
    bp, tp, _ = x_prompt.shape
    bs, ts, _ = x_sample.shape

    mods = _ada(jnp.concatenate([c_prompt, c_sample], axis=0), ada_w, ada_b)
    mods_p = [mods[l, :bp] for l in range(DEPTH)]
    mods_s = [mods[l, bp:] for l in range(DEPTH)]

    n_ssd, n_s5 = state_ssm.shape[0], state_s5_re.shape[0]
    rows_hp = N_HEADS * HEAD_DIM
    zeros_p = (jnp.zeros((n_ssd, bp, rows_hp, D_STATE), state_ssm.dtype),
               jnp.zeros((n_ssd, bp, CONV_K - 1, CONV_DIM), state_conv.dtype),
               jnp.zeros((n_s5, bp, S5_LANES), state_s5_re.dtype),
               jnp.zeros((n_s5, bp, S5_LANES), state_s5_im.dtype))
    yp, ssm_p, conv_p, re_p, im_p = _trunk(
        x_prompt, mods_p, *zeros_p, w,
        tm=512, s5_tb=512 // bp, s5_bb=bp, ssd_batch_major=True)
    ys, ssm_s, conv_s, re_s, im_s = _trunk(
        x_sample, mods_s,
        state_ssm.reshape(n_ssd, bs, rows_hp, D_STATE), state_conv,
        state_s5_re.reshape(n_s5, bs, S5_LANES), state_s5_im.reshape(n_s5, bs, S5_LANES), w,
        tm=512, s5_tb=ts, s5_bb=512 // ts, ssd_batch_major=False)

    ssm_shape = lambda b: (n_ssd, b, N_HEADS, HEAD_DIM, D_STATE)
    s5_shape = lambda b: (n_s5, b, S5_GROUPS, S5_STATE)
    return (yp, ys,
            ssm_p.reshape(ssm_shape(bp)), conv_p, re_p.reshape(s5_shape(bp)), im_p.reshape(s5_shape(bp)),
            ssm_s.reshape(ssm_shape(bs)), conv_s, re_s.reshape(s5_shape(bs)), im_s.reshape(s5_shape(bs)))
```

```python
import functools
import math

import jax
import jax.numpy as jnp
from jax import lax
from jax.experimental import pallas as pl
from jax.experimental.pallas import tpu as pltpu

F32 = jnp.float32
BF16 = jnp.bfloat16

D_MODEL = 1024
DEPTH = 4
D_INNER = 2048
HEAD_DIM = 64
N_HEADS = 32
N_GROUPS = 8
HEADS_PER_GROUP = 4
D_STATE = 128
CONV_K = 4
BC_DIM = 2 * N_GROUPS * D_STATE
CONV_DIM = D_INNER + BC_DIM
ZX_DIM = D_INNER + CONV_DIM
SSD_CHUNK = 128
S5_GROUPS = 64
S5_GROUP = 16
S5_STATE = 64
S5_LANES = S5_GROUPS * S5_STATE
S5_BLOCKS = D_MODEL // 128
D_FF = 2816
RMS_EPS = 1e-6
LANES = 128
SUBLANES = 8
CONV_PAD = 8
N_XTILES = D_INNER // LANES
GROUP_W = HEADS_PER_GROUP * HEAD_DIM
SSD_STEP_BATCHES = 4
FFN_SPLIT = 11
S5_GLU_SPLIT = 4
S5_BU_SPLIT = 4
INPROJ_COLS = 512
VMEM_LIMIT = 52 * 1024 * 1024


def _cparams(sem):
    return pltpu.CompilerParams(dimension_semantics=sem, vmem_limit_bytes=VMEM_LIMIT)


def _sigmoid(x):
    return 1.0 / (1.0 + jnp.exp(-x))


def _silu(x):
    hx = 0.5 * x
    return hx + hx * jnp.tanh(hx)


def _rows_by_batch(x, nb):
    tm, d = x.shape
    return x.reshape(tm // nb, nb, d)


def _modulate_rows(x, sh, sc, nb, ng=1):
    if ng > 1:
        x3 = x.reshape(ng, x.shape[0] // ng, x.shape[1])
        r = lax.rsqrt(jnp.mean(x3 * x3, axis=-1, keepdims=True) + RMS_EPS)
        return ((x3 * r) * (1.0 + sc) + sh).reshape(x.shape)
    if nb == 1:
        r = lax.rsqrt(jnp.mean(x * x, axis=-1, keepdims=True) + RMS_EPS)
        return (x * r) * (1.0 + sc) + sh
    x3 = _rows_by_batch(x, nb)
    r = lax.rsqrt(jnp.mean(x3 * x3, axis=-1, keepdims=True) + RMS_EPS)
    h = (x3 * r) * (1.0 + sc[None]) + sh[None]
    return h.reshape(x.shape)


def _gate_rows(v, g, nb, ng=1):
    if ng > 1:
        return (v.reshape(ng, v.shape[0] // ng, v.shape[1]) * g).reshape(v.shape)
    if nb == 1:
        return v * g
    return (_rows_by_batch(v, nb) * g[None]).reshape(v.shape)


def _dot(a, b):
    return jnp.dot(a, b, preferred_element_type=F32)


def _dot_nt(a, b):
    return lax.dot_general(a, b, (((1,), (1,)), ((), ())), preferred_element_type=F32)


def _dot_tn(a, b):
    return lax.dot_general(a, b, (((0,), (0,)), ((), ())), preferred_element_type=F32)


def _ada_kernel(c_ref, w_ref, b_ref, o_ref):
    s = _silu(c_ref[...]).astype(BF16)
    o_ref[0] = _dot(s, w_ref[0].astype(BF16)) + b_ref[0]


def _ada(c_all, ada_w, ada_b):
    nrow = c_all.shape[0]
    tn = 1536
    return pl.pallas_call(
        _ada_kernel,
        grid=(DEPTH, 6 * D_MODEL // tn),
        in_specs=[
            pl.BlockSpec((nrow, D_MODEL), lambda l, j: (0, 0)),
            pl.BlockSpec((1, D_MODEL, tn), lambda l, j: (l, 0, j)),
            pl.BlockSpec((1, 1, tn), lambda l, j: (l, 0, j)),
        ],
        out_specs=pl.BlockSpec((1, nrow, tn), lambda l, j: (l, 0, j)),
        out_shape=jax.ShapeDtypeStruct((DEPTH, nrow, 6 * D_MODEL), F32),
        compiler_params=_cparams(("arbitrary", "arbitrary")),
        name="ada",
    )(c_all, ada_w, ada_b.reshape(DEPTH, 1, 6 * D_MODEL))


def _inproj_kernel(x_ref, sh_ref, sc_ref, w_ref, wdt_ref, zx_ref, dt_ref, *, nb):
    h = _modulate_rows(x_ref[...], sh_ref[...], sc_ref[...], nb).astype(BF16)
    zx_ref[...] = _dot(h, w_ref[...])
    dt_ref[...] = _dot(h, wdt_ref[...])


def _inproj(x3, sh, sc, w, layer, wdt, tm):
    ng, rows, _ = x3.shape
    nb = sh.shape[1]
    tn = 2048
    per_g = pl.BlockSpec((None, nb, D_MODEL), lambda j, g, i: (g, 0, 0))
    return pl.pallas_call(
        functools.partial(_inproj_kernel, nb=nb),
        grid=(ZX_DIM // tn, ng, rows // tm),
        in_specs=[
            pl.BlockSpec((None, tm, D_MODEL), lambda j, g, i: (g, i, 0)),
            per_g, per_g,
            pl.BlockSpec((None, D_MODEL, tn), lambda j, g, i: (layer, 0, j)),
            pl.BlockSpec((D_MODEL, LANES), lambda j, g, i: (0, 0)),
        ],
        out_specs=[
            pl.BlockSpec((None, tm, tn), lambda j, g, i: (g, i, j)),
            pl.BlockSpec((None, None, tm, LANES), lambda j, g, i: (j, g, i, 0)),
        ],
        out_shape=[
            jax.ShapeDtypeStruct((ng, rows, ZX_DIM), F32),
            jax.ShapeDtypeStruct((ZX_DIM // tn, ng, rows, LANES), F32),
        ],
        compiler_params=_cparams(("arbitrary", "arbitrary", "arbitrary")),
        name="ssd_inproj",
    )(x3, sh, sc, w, wdt)


def _cumsum_rows(a, L):
    rows = lax.broadcasted_iota(jnp.int32, a.shape, 0)
    k = 1
    while k < L:
        a = a + jnp.where(rows >= k, pltpu.roll(a, k, 0), 0.0)
        k *= 2
    return a


def _transpose_rows(a, L):
    if L < LANES:
        a = jnp.concatenate([a, jnp.zeros((LANES - L, LANES), F32)], axis=0)
    return a.T[:, :L]


def _claim_layer0(ssm_out_ref, conv_out_ref):
    for ref in (ssm_out_ref, conv_out_ref):
        ref[1:] = jnp.zeros((ref.shape[0] - 1,) + ref.shape[1:], ref.dtype)
    return ssm_out_ref.at[0], conv_out_ref.at[0]


def _ssd_step_kernel(zx_ref, dt_ref, conv0_ref, ssm0_ref,
                     cw_ref, cb_ref, dtb_ref, alog_ref, dexp_ref, nw_ref, gsum_ref, expand_ref,
                     *rest, L, nbatch, first):
    y_ref, ssm_out_ref, conv_out_ref, pad, xbc_scr = rest[-5:]
    if first:
        ssm_out_ref, conv_out_ref = _claim_layer0(ssm_out_ref, conv_out_ref)
    lo = CONV_PAD - (CONV_K - 1)
    row_t = lax.broadcasted_iota(jnp.int32, (L, LANES), 0)
    b_lo, c_lo = D_INNER, D_INNER + N_GROUPS * D_STATE

    def row_bcast(i, s, lt0, lt1):
        return jnp.concatenate([jnp.broadcast_to(xbc_scr[i, lt, s:s + 1, :], (L, LANES))
                                for lt in range(lt0, lt1)], axis=1)

    xbcs, dts, acss, prods = [], [], [], []
    for i in range(nbatch):
        z0 = i * ZX_DIM
        conv_tiles = []
        for lt in range(2 * N_XTILES):
            cols = slice(lt * LANES, (lt + 1) * LANES)
            pad[i, lt, lo:CONV_PAD, :] = conv0_ref[i, :, cols]
            pad[i, lt, CONV_PAD:CONV_PAD + L, :] = zx_ref[:, z0 + D_INNER + lt * LANES:z0 + D_INNER + (lt + 1) * LANES]
            acc = cb_ref[:, cols]
            for k in range(CONV_K):
                acc = acc + cw_ref[k:k + 1, cols] * pad[i, lt, lo + k:lo + k + L, :]
            conv_tiles.append(_silu(acc))
            xbc_scr[i, lt] = conv_tiles[lt]
            conv_out_ref[i, :, cols] = pad[i, lt, lo + L:CONV_PAD + L, :]
        xbc = jnp.concatenate(conv_tiles, axis=1)
        dtr = dt_ref[:, i * LANES:(i + 1) * LANES] + dtb_ref[...]
        dt = jnp.maximum(dtr, 0.0) + jnp.log1p(jnp.exp(-jnp.abs(dtr)))
        xbcs.append(xbc)
        dts.append(dt)
        acss.append(_cumsum_rows(dt * (-jnp.exp(alog_ref[...])), L))
        cm = xbc[:, c_lo:]
        prods += [cm * row_bcast(i, s, b_lo // LANES, c_lo // LANES) for s in range(L)]

    p_hi, p_lo = _split_bf16(jnp.concatenate(prods, axis=0))
    cbh = _dot(jnp.concatenate([p_hi, p_lo], axis=1).astype(BF16), gsum_ref[...])

    per = L * L + 2 * L
    rows = []
    for i in range(nbatch):
        dt, acs = dts[i], acss[i]
        for s in range(L):
            decay = jnp.exp(jnp.where(row_t >= s, acs - acs[s:s + 1, :], -jnp.inf))
            r0 = (i * L + s) * L
            rows.append(cbh[r0:r0 + L, :] * decay * dt[s:s + 1, :])
        rows += [jnp.exp(acs), jnp.exp(acs[L - 1:L, :] - acs) * dt]
    w_hi, w_lo = _split_bf16(jnp.concatenate(rows, axis=0))
    wide = _dot(jnp.concatenate([w_hi, w_lo], axis=1).astype(BF16), expand_ref[...])

    for i in range(nbatch):
        z0, w0 = i * ZX_DIM, i * per
        xbc = xbcs[i]
        xs = xbc[:, :D_INNER]
        y = dexp_ref[...] * xs
        for s in range(L):
            y = y + wide[w0 + s * L:w0 + (s + 1) * L, :] * row_bcast(i, s, 0, N_XTILES)
        ea_x = wide[w0 + L * L:w0 + L * L + L, :]
        xw = xs * wide[w0 + L * L + L:w0 + per, :]
        e_last = jnp.exp(acss[i][L - 1:L, :])
        for g in range(N_GROUPS):
            gl = slice(g * GROUP_W, (g + 1) * GROUP_W)
            bg = xbc[:, b_lo + g * D_STATE:b_lo + (g + 1) * D_STATE].astype(BF16)
            cg = xbc[:, c_lo + g * D_STATE:c_lo + (g + 1) * D_STATE].astype(BF16)
            sg = ssm0_ref[i, gl, :]
            yg = y[:, gl] + ea_x[:, gl] * _dot_nt(cg, sg.astype(BF16))
            upd = _dot_tn(xw[:, gl].astype(BF16), bg)
            for r in range(HEADS_PER_GROUP):
                h = g * HEADS_PER_GROUP + r
                rr = slice(r * HEAD_DIM, (r + 1) * HEAD_DIM)
                ssm_out_ref[i, g * GROUP_W + r * HEAD_DIM:g * GROUP_W + (r + 1) * HEAD_DIM, :] = (
                    e_last[:, h:h + 1] * sg[rr, :] + upd[rr, :])
            yg = yg * _silu(zx_ref[:, z0 + g * GROUP_W:z0 + (g + 1) * GROUP_W])
            yg = yg * lax.rsqrt(jnp.mean(yg * yg, axis=-1, keepdims=True) + RMS_EPS)
            y0 = i * D_INNER + g * GROUP_W
            y_ref[:, y0:y0 + GROUP_W] = (yg * nw_ref[:, gl]).astype(y_ref.dtype)


def _split_bf16(v):
    hi = v.astype(BF16).astype(F32)
    return hi, (v - hi).astype(BF16).astype(F32)


def _ssd_chunk_kernel(x_ref, xn_ref, sh_ref, sc_ref, win_ref, wdt_ref, conv0_ref, ssm0_ref,
                      cw_ref, cb_ref, dtb_ref, alog_ref, dexp_ref, nw_ref, expand_ref, *rest, L, first):
    y_ref, ssm_out_ref, conv_out_ref, pad, state_t, z_scr, dt_scr = rest[-7:]
    c = pl.program_id(1)
    lo = CONV_PAD - (CONV_K - 1)
    all_slots = (ssm_out_ref, conv_out_ref)
    if first:
        ssm_out_ref, conv_out_ref = ssm_out_ref.at[0], conv_out_ref.at[0]

    def projection_steps(x_rows, slot):
        h = _modulate_rows(x_rows, sh_ref[...], sc_ref[...], 1).astype(BF16)
        pw = INPROJ_COLS

        def z_step(q):
            z_scr[slot, :, q * pw:(q + 1) * pw] = _dot(h, win_ref[:, q * pw:(q + 1) * pw])

        def xbc_step(q):
            res = _dot(h, win_ref[:, D_INNER + q * pw:D_INNER + (q + 1) * pw])
            for r in range(pw // LANES):
                pad[slot, q * (pw // LANES) + r, CONV_PAD:CONV_PAD + L, :] = res[:, r * LANES:(r + 1) * LANES]

        def dt_step():
            dt_scr[slot] = _dot(h, wdt_ref[...])

        return ([functools.partial(xbc_step, q) for q in range(CONV_DIM // pw)]
                + [functools.partial(z_step, q) for q in range(D_INNER // pw)] + [dt_step])

    @pl.when(c == 0)
    def _():
        state_t[...] = ssm0_ref[...].T
        for lt in range(2 * N_XTILES):
            pad[0, lt, lo:CONV_PAD, :] = conv0_ref[0, :, lt * LANES:(lt + 1) * LANES]
        for step in projection_steps(x_ref[:L, :], 0):
            step()

    body = functools.partial(_ssd_chunk_body, z_scr=z_scr, dt_scr=dt_scr, pad=pad, state_t=state_t,
                             cw_ref=cw_ref, cb_ref=cb_ref, dtb_ref=dtb_ref, alog_ref=alog_ref,
                             dexp_ref=dexp_ref, nw_ref=nw_ref, expand_ref=expand_ref, y_ref=y_ref, L=L)
    body(0, 1, slice(0, L), projection_steps(x_ref[L:, :], 1))
    body(1, 0, slice(L, 2 * L), projection_steps(xn_ref[...], 0))
    for lt in range(2 * N_XTILES):
        conv_out_ref[0, :, lt * LANES:(lt + 1) * LANES] = pad[0, lt, lo:CONV_PAD, :]

    @pl.when(c == pl.num_programs(1) - 1)
    def _():
        if first:
            _claim_layer0(*all_slots)
        ssm_out_ref[...] = state_t[...].T


def _ssd_chunk_body(slot, next_slot, rows, side_steps, *, z_scr, dt_scr, pad, state_t, cw_ref, cb_ref,
                    dtb_ref, alog_ref, dexp_ref, nw_ref, expand_ref, y_ref, L):
    lo = CONV_PAD - (CONV_K - 1)
    assert len(side_steps) <= 2 * N_XTILES // 4 + N_GROUPS
    pending = iter(side_steps)

    def side_point():
        step = next(pending, None)
        if step is not None:
            step()

    def conv_tile(lt):
        cols = slice(lt * LANES, (lt + 1) * LANES)
        acc = cb_ref[:, cols]
        for k in range(CONV_K):
            acc = acc + cw_ref[k:k + 1, cols] * pad[slot, lt, lo + k:lo + k + L, :]
        return _silu(acc)

    tiles = []
    for lt in range(2 * N_XTILES):
        tiles.append(conv_tile(lt))
        if lt % 4 == 3:
            side_point()

    for lt in range(2 * N_XTILES):
        pad[next_slot, lt, lo:CONV_PAD, :] = pad[slot, lt, lo + L:CONV_PAD + L, :]

    dtr = dt_scr[slot] + dtb_ref[...]
    dt = jnp.maximum(dtr, 0.0) + jnp.log1p(jnp.exp(-jnp.abs(dtr)))
    a = dt * (-jnp.exp(alog_ref[...]))
    acs = _cumsum_rows(a, L)
    acs_t = acs.T
    dt_t = dt.T
    a_last = acs[L - 1:L, :]
    ea_hi, ea_lo = _split_bf16(jnp.exp(acs))
    we_hi, we_lo = _split_bf16(jnp.exp(a_last - acs) * dt)
    parts = jnp.concatenate([jnp.concatenate([ea_hi, ea_lo], axis=1),
                             jnp.concatenate([we_hi, we_lo], axis=1)], axis=0).astype(BF16)
    wide = _dot(parts, expand_ref[...])
    ea_x = wide[:L]
    we_x = wide[L:]

    causal = (lax.broadcasted_iota(jnp.int32, (L, L), 0)
              >= lax.broadcasted_iota(jnp.int32, (L, L), 1))
    lane_head = lax.broadcasted_iota(jnp.int32, (L, GROUP_W), 1) // HEAD_DIM

    for g in range(N_GROUPS):
        gl = slice(g * GROUP_W, (g + 1) * GROUP_W)
        bg = tiles[N_XTILES + g].astype(BF16)
        cg = tiles[N_XTILES + N_GROUPS + g].astype(BF16)
        cb = _dot_nt(cg, bg)
        sg = state_t[:, gl]
        y_off = _dot(cg, sg.astype(BF16))
        ms = []
        for r in range(HEADS_PER_GROUP):
            h = g * HEADS_PER_GROUP + r
            seg = acs[:, h:h + 1] - acs_t[h:h + 1, :]
            decay = jnp.exp(jnp.where(causal, seg, -jnp.inf))
            ms.append((cb * decay * dt_t[h:h + 1, :]).astype(BF16))
        xg = jnp.concatenate([tiles[2 * g], tiles[2 * g + 1]], axis=1)
        xgb = xg.astype(BF16)
        xblk = jnp.concatenate([jnp.where(lane_head == r, xgb, jnp.zeros_like(xgb))
                                for r in range(HEADS_PER_GROUP)], axis=0)
        y_diag = _dot(jnp.concatenate(ms, axis=1), xblk)
        yg = y_diag + ea_x[:, gl] * y_off + dexp_ref[:, gl] * xg
        upd = _dot_tn(bg, (xg * we_x[:, gl]).astype(BF16))
        state_t[:, gl] = ea_x[L - 1:L, gl] * sg + upd
        yg = yg * _silu(z_scr[slot, :, gl])
        yg = yg * lax.rsqrt(jnp.mean(yg * yg, axis=-1, keepdims=True) + RMS_EPS)
        y_ref[rows, gl] = (yg * nw_ref[:, gl]).astype(y_ref.dtype)
        side_point()


def _ssm_state_spec(grid_rank, j, nb):
    shape = (None, nb, N_HEADS * HEAD_DIM, D_STATE)
    if grid_rank == 2:
        return pl.BlockSpec(shape, lambda b, c: (j, b, 0, 0))
    return pl.BlockSpec(shape, lambda b: (j, b, 0, 0))


def _ssd_call(body, grid, in_specs, args, y_spec, y_shape, conv_spec, ssm_all, conv_all, ssm_prev, conv_prev,
              j, nb, scratch, sem):
    first = ssm_prev is None
    body = functools.partial(body, first=first)
    in_specs, args, aliases = list(in_specs), list(args), {}
    state_spec = _ssm_state_spec(len(grid), j, nb)
    if first:
        assert j == 0
        n_layers = ssm_all.shape[0]
        at_layer0 = lambda *g: (0, g[0], 0, 0)
        state_spec = pl.BlockSpec((n_layers, nb) + ssm_all.shape[2:], at_layer0)
        conv_spec = pl.BlockSpec((n_layers, nb or 1) + conv_all.shape[2:], at_layer0)
    else:
        in_specs += [pl.BlockSpec(memory_space=pl.ANY)] * 2
        args += [ssm_prev, conv_prev]
        aliases = {len(args) - 2: 1, len(args) - 1: 2}
    return pl.pallas_call(
        body,
        grid=grid,
        in_specs=in_specs,
        out_specs=[y_spec, state_spec, conv_spec],
        out_shape=[
            jax.ShapeDtypeStruct(y_shape, BF16),
            jax.ShapeDtypeStruct(ssm_all.shape, F32),
            jax.ShapeDtypeStruct(conv_all.shape, F32),
        ],
        scratch_shapes=scratch,
        input_output_aliases=aliases,
        compiler_params=_cparams(sem),
        name="ssd_core",
    )(*args)


def _ssd_chunked(x, sh, sc, w_in, w_dt, conv_all, ssm_all, ssm_prev, conv_prev, j, cw, cb, dtb, alog, dexp, nw):
    B, T, _ = x.shape
    L = SSD_CHUNK
    nsteps = T // (2 * L)
    last_chunk = T // L - 1
    full = lambda shape: pl.BlockSpec(shape, lambda b, c: (0,) * len(shape))
    resident = lambda shape: pl.BlockSpec(shape, lambda b, c: (0,) * len(shape), pipeline_mode=pl.Buffered(1))
    pair = lambda w: pl.BlockSpec((None, 2 * L, w), lambda b, c: (b, c, 0))
    nxt = pl.BlockSpec((None, L, D_MODEL), lambda b, c: (b, jnp.minimum(2 * c + 2, last_chunk), 0))
    per_b = pl.BlockSpec((None, 1, D_MODEL), lambda b, c: (b, 0, 0))
    conv_spec = pl.BlockSpec((None, 1, CONV_K - 1, CONV_DIM), lambda b, c: (j, b, 0, 0))
    w_in_spec = pl.BlockSpec((None, D_MODEL, ZX_DIM), lambda b, c: (j, 0, 0), pipeline_mode=pl.Buffered(1))
    sel = (jnp.arange(D_INNER)[None, :] // HEAD_DIM == jnp.arange(LANES)[:, None]).astype(BF16)
    in_specs = [
        pair(D_MODEL), nxt, per_b, per_b, w_in_spec, resident((D_MODEL, LANES)),
        conv_spec, _ssm_state_spec(2, j, None),
        full((CONV_K, CONV_DIM)), full((1, CONV_DIM)), full((1, LANES)), full((1, LANES)),
        full((1, D_INNER)), full((1, D_INNER)), resident((2 * LANES, D_INNER)),
    ]
    args = [x, x, sh, sc, w_in, w_dt, conv_all, ssm_all, cw, cb, dtb, alog, dexp, nw,
            jnp.concatenate([sel, sel], axis=0)]
    scratch = [
        pltpu.VMEM((2, 2 * N_XTILES, CONV_PAD + L, LANES), F32),
        pltpu.VMEM((D_STATE, N_HEADS * HEAD_DIM), F32),
        pltpu.VMEM((2, L, D_INNER), F32),
        pltpu.VMEM((2, L, LANES), F32),
    ]
    return _ssd_call(functools.partial(_ssd_chunk_kernel, L=L), (B, nsteps), in_specs, args,
                     pair(D_INNER), (B, T, D_INNER), conv_spec, ssm_all, conv_all, ssm_prev, conv_prev, j, None,
                     scratch, ("arbitrary", "arbitrary"))


def _ssd_step(zx, dtraw, conv_all, ssm_all, ssm_prev, conv_prev, j, cw, cb, dtb, alog, dexp, nw, B):
    T = zx.shape[0]
    nb = SSD_STEP_BATCHES
    full = lambda shape: pl.BlockSpec(shape, lambda b: (0,) * len(shape))
    col = lambda w: pl.BlockSpec((T, nb * w), lambda b: (0, b))
    conv_spec = pl.BlockSpec((None, nb, CONV_K - 1, CONV_DIM), lambda b: (j, b, 0, 0))
    in_specs = [
        col(ZX_DIM), col(LANES), conv_spec, _ssm_state_spec(1, j, nb),
        full((CONV_K, CONV_DIM)), full((1, CONV_DIM)), full((1, LANES)), full((1, LANES)),
        full((1, D_INNER)), full((1, D_INNER)),
        full((2 * N_GROUPS * D_STATE, LANES)), full((2 * LANES, D_INNER)),
    ]
    head_ids = jnp.arange(LANES)
    gsum = ((jnp.arange(N_GROUPS * D_STATE)[:, None] // D_STATE == head_ids[None, :] // HEADS_PER_GROUP)
            & (head_ids[None, :] < N_HEADS)).astype(BF16)
    sel = (jnp.arange(D_INNER)[None, :] // HEAD_DIM == head_ids[:, None]).astype(BF16)
    args = [zx, dtraw, conv_all, ssm_all, cw, cb, dtb, alog, dexp, nw,
            jnp.concatenate([gsum, gsum], axis=0), jnp.concatenate([sel, sel], axis=0)]
    scratch = [pltpu.VMEM((nb, 2 * N_XTILES, CONV_PAD + T, LANES), F32),
               pltpu.VMEM((nb, 2 * N_XTILES, T, LANES), F32)]
    return _ssd_call(functools.partial(_ssd_step_kernel, L=T, nbatch=nb), (B // nb,), in_specs, args,
                     col(D_INNER), (T, B * D_INNER), conv_spec, ssm_all, conv_all, ssm_prev, conv_prev, j, nb,
                     scratch, ("arbitrary",))


def _ffn_kernel(*refs, nb, ng, has_proj, final, relayout):
    if relayout:
        refs, relay = refs[:-1], refs[-1]
    if has_proj:
        x_ref, y_ref, ow_ref, g1_ref, sh_ref, sc_ref, g2_ref, win_ref, wo_ref, fw_ref, o_ref = refs
    else:
        x_ref, sh_ref, sc_ref, g2_ref, win_ref, wo_ref, fw_ref, o_ref = refs
    x = x_ref[...]
    tm = x.size // D_MODEL
    x = x.reshape(tm, D_MODEL)
    if has_proj:
        y = y_ref[...].reshape(tm, D_INNER)
        x = x + _gate_rows(_dot(y, ow_ref[...]), g1_ref[...], nb, ng)
    h = _modulate_rows(x, sh_ref[...], sc_ref[...], nb, ng).astype(BF16)
    tf = D_FF // FFN_SPLIT
    acc = None
    for c in range(FFN_SPLIT):
        gate = _dot(h, win_ref[:, c * tf:(c + 1) * tf])
        up = _dot(h, win_ref[:, D_FF + c * tf:D_FF + (c + 1) * tf])
        part = _dot((_silu(gate) * up).astype(BF16), wo_ref[c * tf:(c + 1) * tf, :])
        acc = part if acc is None else acc + part
    x2 = x + _gate_rows(acc, g2_ref[...], nb, ng)
    if final:
        x2 = x2 * lax.rsqrt(jnp.mean(x2 * x2, axis=-1, keepdims=True) + RMS_EPS) * fw_ref[...]
    n_lt = D_MODEL // LANES
    if relayout == "to_batch_major":
        for lt in range(n_lt):
            relay[lt] = x2[:, lt * LANES:(lt + 1) * LANES]
        for b in range(nb):
            o_ref[b] = jnp.concatenate([relay[lt, pl.ds(b, tm // nb, stride=nb), :] for lt in range(n_lt)], axis=1)
    elif relayout == "to_time_major":
        tg = tm // ng
        for lt in range(n_lt):
            for g in range(ng):
                relay[lt, pl.ds(g, tg, stride=ng), :] = x2[g * tg:(g + 1) * tg, lt * LANES:(lt + 1) * LANES]
        o_ref[...] = jnp.concatenate([relay[lt] for lt in range(n_lt)], axis=1).reshape(o_ref.shape)
    else:
        o_ref[...] = x2.reshape(o_ref.shape)


def _ffn(x3, y3, out_w, proj_layer, g1, sh, sc, g2, w_in, w_out, layer, fw, tm, final, relayout=None):
    ng, rows, _ = x3.shape
    nb = sh.shape[1]
    has_proj = y3 is not None
    resident = lambda shape, l: pl.BlockSpec((None,) + shape, lambda g, i: (l,) + (0,) * len(shape),
                                             pipeline_mode=pl.Buffered(1))
    if relayout == "to_time_major":
        assert nb == 1
        tg = tm // ng
        row_blk = lambda w: pl.BlockSpec((ng, tg, w), lambda g, i: (0, i, 0))
        per_g = pl.BlockSpec((ng, 1, D_MODEL), lambda g, i: (0, 0, 0))
        grid = (1, rows // tg)
        out_spec = pl.BlockSpec((tg, ng, D_MODEL), lambda g, i: (i, 0, 0))
        out_shape = (rows, ng, D_MODEL)
        kernel_ng = ng
    else:
        row_blk = lambda w: pl.BlockSpec((None, tm, w), lambda g, i: (g, i, 0))
        per_g = pl.BlockSpec((None, nb, D_MODEL), lambda g, i: (g, 0, 0))
        grid = (ng, rows // tm)
        kernel_ng = 1
        if relayout == "to_batch_major":
            assert ng == 1
            out_spec = pl.BlockSpec((nb, tm // nb, D_MODEL), lambda g, i: (0, i, 0))
            out_shape = (nb, rows // nb, D_MODEL)
        else:
            out_spec = row_blk(D_MODEL)
            out_shape = (ng, rows, D_MODEL)
    in_specs = [row_blk(D_MODEL)]
    args = [x3]
    if has_proj:
        in_specs += [row_blk(D_INNER), resident((D_INNER, D_MODEL), proj_layer), per_g]
        args += [y3, out_w, g1]
    in_specs += [per_g, per_g, per_g,
                 resident((D_MODEL, 2 * D_FF), layer), resident((D_FF, D_MODEL), layer),
                 pl.BlockSpec((1, D_MODEL), lambda g, i: (0, 0))]
    args += [sh, sc, g2, w_in, w_out, fw]
    return pl.pallas_call(
        functools.partial(_ffn_kernel, nb=nb, ng=kernel_ng, has_proj=has_proj, final=final, relayout=relayout),
        grid=grid,
        in_specs=in_specs,
        out_specs=out_spec,
        out_shape=jax.ShapeDtypeStruct(out_shape, F32),
        scratch_shapes=[pltpu.VMEM((D_MODEL // LANES, tm, LANES), F32)] if relayout else [],
        compiler_params=_cparams(("arbitrary", "arbitrary")),
        name="proj_ffn" if has_proj else "ffn",
    )(*args)


def _s5_disc_kernel(are_ref, aim_ref, ldt_ref, bre_ref, bim_ref, abre_ref, abim_ref, bbre_ref, bbim_ref):
    lre, lim = are_ref[...], aim_ref[...]
    dt = jnp.exp(ldt_ref[...])
    mag = jnp.exp(lre * dt)
    ab_re, ab_im = mag * jnp.cos(lim * dt), mag * jnp.sin(lim * dt)
    den = lre * lre + lim * lim
    nr, ni = ab_re - 1.0, ab_im
    q_re = (nr * lre + ni * lim) / den
    q_im = (ni * lre - nr * lim) / den
    abre_ref[...] = ab_re
    abim_ref[...] = ab_im
    br, bi = bre_ref[...], bim_ref[...]
    bbre_ref[...] = q_re[:, None, :] * br - q_im[:, None, :] * bi
    bbim_ref[...] = q_re[:, None, :] * bi + q_im[:, None, :] * br


def _s5_disc(a_re, a_im, log_dt, b_re, b_im):
    gn = jax.ShapeDtypeStruct((S5_GROUPS, S5_STATE), F32)
    gin = jax.ShapeDtypeStruct((S5_GROUPS, S5_GROUP, S5_STATE), F32)
    return pl.pallas_call(
        _s5_disc_kernel,
        out_shape=[gn, gn, gin, gin],
        name="s5_disc",
    )(a_re, a_im, log_dt.reshape(S5_GROUPS, 1), b_re, b_im)


def _s5_kernel(x_ref, sh_ref, sc_ref, g1_ref, bb_ref, cc_ref, are_ref, aim_ref, s0re_ref, s0im_ref,
               dskip_ref, glu_ref, o_ref, sre_ref, sim_ref, xs_re, xs_im, st_re, st_im, *, tb, bb):
    t_idx = pl.program_id(1)
    rows = tb * bb

    @pl.when(t_idx == 0)
    def _():
        st_re[...] = s0re_ref[...]
        st_im[...] = s0im_ref[...]

    x = x_ref[...].reshape(rows, D_MODEL)
    h = _modulate_rows(x, sh_ref[...], sc_ref[...], bb)
    hb = h.astype(BF16)
    half = S5_LANES // S5_BLOCKS
    pw = 2 * half // S5_BU_SPLIT

    def bu_pieces(k):
        out = [None] * S5_BU_SPLIT

        def piece(j):
            out[j] = _dot(hb[:, k * LANES:(k + 1) * LANES], bb_ref[k, :, j * pw:(j + 1) * pw])

        return out, [functools.partial(piece, j) for j in range(S5_BU_SPLIT)]

    def y_pieces(k):
        out = [None] * 2

        def piece(j):
            src = xs_im if j else xs_re
            out[j] = _dot(src[:, k * half:(k + 1) * half].astype(BF16), cc_ref[k, j * half:(j + 1) * half, :])

        return out, [functools.partial(piece, j) for j in range(2)]

    n_iter = (bb // SUBLANES) * tb
    bu_next, steps = bu_pieces(0)
    for step in steps:
        step()
    y_parts = [None] * S5_BLOCKS
    for k in range(S5_BLOCKS):
        ks = slice(k * half, (k + 1) * half)
        bu = jnp.concatenate(bu_next, axis=1)
        side = []
        if k + 1 < S5_BLOCKS:
            bu_next, steps = bu_pieces(k + 1)
            side += steps
        if k > 0:
            y_parts[k - 1], steps = y_pieces(k - 1)
            side += steps
        run_at = {((j + 1) * n_iter) // (len(side) + 1): step for j, step in enumerate(side)}
        ar, ai = are_ref[:, ks], aim_ref[:, ks]
        it = 0
        for rg in range(bb // SUBLANES):
            rs = slice(rg * SUBLANES, (rg + 1) * SUBLANES)
            xr, xi = st_re[rs, ks], st_im[rs, ks]
            for t in range(tb):
                sl = slice(t * bb + rg * SUBLANES, t * bb + (rg + 1) * SUBLANES)
                xr, xi = (ar * xr - ai * xi + bu[sl, :half], ar * xi + ai * xr + bu[sl, half:])
                xs_re[sl, ks] = xr
                xs_im[sl, ks] = xi
                it += 1
                if it in run_at:
                    run_at[it]()
            st_re[rs, ks] = xr
            st_im[rs, ks] = xi
    y_parts[S5_BLOCKS - 1], steps = y_pieces(S5_BLOCKS - 1)
    for step in steps:
        step()
    y = jnp.concatenate([p[0] - p[1] for p in y_parts], axis=1) + dskip_ref[...] * h
    y = 0.5 * y * (1.0 + jnp.tanh(math.sqrt(2.0 / math.pi) * (y + 0.044715 * (y * y * y))))
    yb = y.astype(BF16)
    cw = D_MODEL // S5_GLU_SPLIT
    out = jnp.concatenate(
        [_dot(yb, glu_ref[:, c * cw:(c + 1) * cw])
         * _sigmoid(_dot(yb, glu_ref[:, D_MODEL + c * cw:D_MODEL + (c + 1) * cw]))
         for c in range(S5_GLU_SPLIT)], axis=1)
    o_ref[...] = (x + _gate_rows(out, g1_ref[...], bb)).reshape(tb, bb, D_MODEL)

    @pl.when(t_idx == pl.num_programs(1) - 1)
    def _():
        sre_ref[...] = st_re[...]
        sim_ref[...] = st_im[...]


def _s5_layer(x3, sh, sc, g1, bbd, ccd, a_re, a_im, s0re, s0im, dskip, glu_w, tb, bb):
    T, B, _ = x3.shape
    per_b = pl.BlockSpec((bb, D_MODEL), lambda i, t: (i, 0))
    st = pl.BlockSpec((bb, S5_LANES), lambda i, t: (i, 0))
    full = lambda shape: pl.BlockSpec(shape, lambda i, t: (0,) * len(shape))
    return pl.pallas_call(
        functools.partial(_s5_kernel, tb=tb, bb=bb),
        grid=(B // bb, T // tb),
        in_specs=[
            pl.BlockSpec((tb, bb, D_MODEL), lambda i, t: (t, i, 0)),
            per_b, per_b, per_b,
            full((S5_BLOCKS, LANES, 2 * S5_LANES // S5_BLOCKS)),
            full((S5_BLOCKS, 2 * S5_LANES // S5_BLOCKS, LANES)),
            full((SUBLANES, S5_LANES)),
            full((SUBLANES, S5_LANES)),
            st, st,
            full((1, D_MODEL)),
            full((D_MODEL, 2 * D_MODEL)),
        ],
        out_specs=[pl.BlockSpec((tb, bb, D_MODEL), lambda i, t: (t, i, 0)), st, st],
        out_shape=[
            jax.ShapeDtypeStruct((T, B, D_MODEL), F32),
            jax.ShapeDtypeStruct((B, S5_LANES), F32),
            jax.ShapeDtypeStruct((B, S5_LANES), F32),
        ],
        scratch_shapes=[
            pltpu.VMEM((tb * bb, S5_LANES), F32),
            pltpu.VMEM((tb * bb, S5_LANES), F32),
            pltpu.VMEM((bb, S5_LANES), F32),
            pltpu.VMEM((bb, S5_LANES), F32),
        ],
        compiler_params=_cparams(("arbitrary", "arbitrary")),
        name="s5_layer",
    )(x3, sh, sc, g1, bbd, ccd, a_re, a_im, s0re, s0im, dskip, glu_w)


def _block_diag(w):
    _, r, c = w.shape
    w = w.reshape(S5_BLOCKS, 8, r, c)
    eye = jnp.eye(8, dtype=w.dtype)
    return (w[:, :, :, None, :] * eye[None, :, None, :, None]).reshape(S5_BLOCKS, 8 * r, 8 * c)


def _prep_params(p):
    w = {}
    w["ssd_in_w"] = p["ssd_in_w"].astype(BF16)
    w["ssd_dt_w"] = [jnp.pad(p["ssd_in_w"][j, :, ZX_DIM:], ((0, 0), (0, LANES - N_HEADS))).astype(BF16)
                     for j in range(2)]
    pad_h = lambda v: jnp.pad(v, (0, LANES - N_HEADS)).reshape(1, LANES)
    w["dt_bias"] = [pad_h(p["ssd_dt_bias"][j]) for j in range(2)]
    w["a_log"] = [pad_h(p["ssd_A_log"][j]) for j in range(2)]
    w["d_exp"] = [jnp.repeat(p["ssd_D"][j], HEAD_DIM).reshape(1, D_INNER) for j in range(2)]
    w["norm_w"] = [p["ssd_norm_w"][j].reshape(1, D_INNER) for j in range(2)]
    w["conv_w"] = [p["ssd_conv_w"][j] for j in range(2)]
    w["conv_b"] = [p["ssd_conv_b"][j].reshape(1, CONV_DIM) for j in range(2)]
    w["ssd_out_w"] = p["ssd_out_w"].astype(BF16)
    w["ffn_w_in"] = p["ffn_w_in"].astype(BF16)
    w["ffn_w_out"] = p["ffn_w_out"].astype(BF16)
    w["final_w"] = p["final_norm_w"].reshape(1, D_MODEL)
    w["s5"] = []
    for j in range(2):
        ab_re, ab_im, bb_re, bb_im = _s5_disc(
            p["s5_A_re"][j], p["s5_A_im"][j], p["s5_log_dt"][j],
            jnp.swapaxes(p["s5_B_re"][j], 1, 2), jnp.swapaxes(p["s5_B_im"][j], 1, 2))
        bbd = jnp.concatenate([_block_diag(bb_re), _block_diag(bb_im)], axis=2).astype(BF16)
        c_re = jnp.swapaxes(p["s5_C_re"][j], 1, 2)
        c_im = jnp.swapaxes(p["s5_C_im"][j], 1, 2)
        ccd = jnp.concatenate([_block_diag(c_re), _block_diag(c_im)], axis=1).astype(BF16)
        bro = lambda v: jnp.broadcast_to(v.reshape(1, S5_LANES), (SUBLANES, S5_LANES))
        w["s5"].append(dict(bbd=bbd, ccd=ccd, a_re=bro(ab_re), a_im=bro(ab_im),
                            dskip=p["s5_D"][j].reshape(1, D_MODEL),
                            glu_w=p["s5_glu_w"][j].astype(BF16)))
    return w


def _trunk(x, mods, ssm0, conv0, re0, im0, w, tm, s5_tb, s5_bb, ssd_batch_major):
    B, T, _ = x.shape
    assert ssd_batch_major == (T % (2 * SSD_CHUNK) == 0)
    ssm_new = conv_new = None
    re_new, im_new = [], []
    if not ssd_batch_major:
        x = jnp.swapaxes(x, 0, 1)
    for i in range(DEPTH):
        j = i // 2
        final = i == DEPTH - 1
        parts = [mods[i][:, k * D_MODEL:(k + 1) * D_MODEL] for k in range(6)]
        by_rows = [v[None] for v in parts]
        by_batch = [v[:, None] for v in parts]
        ffn_w = (w["ffn_w_in"], w["ffn_w_out"], i, w["final_w"], tm, final)
        if i % 2 == 0:
            sh1, sc1, g1, sh2, sc2, g2 = by_batch if ssd_batch_major else by_rows
            x3 = x if ssd_batch_major else x.reshape(1, T * B, D_MODEL)
            ssd_w = (w["conv_w"][j], w["conv_b"][j], w["dt_bias"][j], w["a_log"][j], w["d_exp"][j],
                     w["norm_w"][j])
            if ssd_batch_major:
                y, ssm_new, conv_new = _ssd_chunked(x3, sh1, sc1, w["ssd_in_w"], w["ssd_dt_w"][j],
                                                    conv0, ssm0, ssm_new, conv_new, j, *ssd_w)
                x = _ffn(x3, y, w["ssd_out_w"], j, g1, sh2, sc2, g2, *ffn_w, relayout="to_time_major")
            else:
                zx, dtraw = _inproj(x3, sh1, sc1, w["ssd_in_w"], j, w["ssd_dt_w"][j], tm)
                y, ssm_new, conv_new = _ssd_step(zx.reshape(T, B * ZX_DIM), dtraw[0].reshape(T, B * LANES),
                                                 conv0, ssm0, ssm_new, conv_new, j, *ssd_w, B)
                x = _ffn(x3, y.reshape(1, T * B, D_INNER), w["ssd_out_w"], j, g1, sh2, sc2, g2,
                         *ffn_w).reshape(T, B, D_MODEL)
        else:
            s5 = w["s5"][j]
            sh1, sc1, g1 = [v[0] for v in by_rows[:3]]
            xs5, sr, si = _s5_layer(x, sh1, sc1, g1, s5["bbd"], s5["ccd"], s5["a_re"], s5["a_im"],
                                    re0[j], im0[j], s5["dskip"], s5["glu_w"], s5_tb, s5_bb)
            re_new.append(sr)
            im_new.append(si)
            if ssd_batch_major:
                x = _ffn(xs5.reshape(1, T * B, D_MODEL), None, None, None, None, *by_rows[3:], *ffn_w,
                         relayout="to_batch_major")
            else:
                x = _ffn(xs5.reshape(1, T * B, D_MODEL), None, None, None, None, *by_rows[3:],
                         *ffn_w).reshape(T, B, D_MODEL)
    if not ssd_batch_major:
        x = jnp.swapaxes(x, 0, 1)
    return x, ssm_new, conv_new, jnp.stack(re_new), jnp.stack(im_new)


def kernel(x_prompt, x_sample, state_ssm, state_conv, state_s5_re, state_s5_im, c_prompt, c_sample,
           ada_w, ada_b, ssd_in_w, ssd_conv_w, ssd_conv_b, ssd_dt_bias, ssd_A_log, ssd_D, ssd_norm_w,
           ssd_out_w, s5_A_re, s5_A_im, s5_log_dt, s5_B_re, s5_B_im, s5_C_re, s5_C_im, s5_D, s5_glu_w,
           ffn_w_in, ffn_w_out, final_norm_w):
    p = dict(ssd_in_w=ssd_in_w, ssd_conv_w=ssd_conv_w, ssd_conv_b=ssd_conv_b, ssd_dt_bias=ssd_dt_bias,
             ssd_A_log=ssd_A_log, ssd_D=ssd_D, ssd_norm_w=ssd_norm_w, ssd_out_w=ssd_out_w,
             s5_A_re=s5_A_re, s5_A_im=s5_A_im, s5_log_dt=s5_log_dt, s5_B_re=s5_B_re, s5_B_im=s5_B_im,
             s5_C_re=s5_C_re, s5_C_im=s5_C_im, s5_D=s5_D, s5_glu_w=s5_glu_w,
             ffn_w_in=ffn_w_in, ffn_w_out=ffn_w_out, final_norm_w=final_norm_w)
    w = _prep_params(p)
    bp, tp, _ = x_prompt.shape
    bs, ts, _ = x_sample.shape

    mods = _ada(jnp.concatenate([c_prompt, c_sample], axis=0), ada_w, ada_b)
    mods_p = [mods[l, :bp] for l in range(DEPTH)]
    mods_s = [mods[l, bp:] for l in range(DEPTH)]

    n_ssd, n_s5 = state_ssm.shape[0], state_s5_re.shape[0]
    rows_hp = N_HEADS * HEAD_DIM
    zeros_p = (jnp.zeros((n_ssd, bp, rows_hp, D_STATE), state_ssm.dtype),
               jnp.zeros((n_ssd, bp, CONV_K - 1, CONV_DIM), state_conv.dtype),
               jnp.zeros((n_s5, bp, S5_LANES), state_s5_re.dtype),
               jnp.zeros((n_s5, bp, S5_LANES), state_s5_im.dtype))
    yp, ssm_p, conv_p, re_p, im_p = _trunk(
        x_prompt, mods_p, *zeros_p, w,
        tm=512, s5_tb=512 // bp, s5_bb=bp, ssd_batch_major=True)
    ys, ssm_s, conv_s, re_s, im_s = _trunk(
        x_sample, mods_s,
        state_ssm.reshape(n_ssd, bs, rows_hp, D_STATE), state_conv,
        state_s5_re.reshape(n_s5, bs, S5_LANES), state_s5_im.reshape(n_s5, bs, S5_LANES), w,
        tm=512, s5_tb=ts, s5_bb=512 // ts, ssd_batch_major=False)

    ssm_shape = lambda b: (n_ssd, b, N_HEADS, HEAD_DIM, D_STATE)
    s5_shape = lambda b: (n_s5, b, S5_GROUPS, S5_STATE)
    return (yp, ys,
            ssm_p.reshape(ssm_shape(bp)), conv_p, re_p.reshape(s5_shape(bp)), im_p.reshape(s5_shape(bp)),
            ssm_s.reshape(ssm_shape(bs)), conv_s, re_s.reshape(s5_shape(bs)), im_s.reshape(s5_shape(bs)))
```

```python
import functools
import math

import jax
import jax.numpy as jnp
from jax import lax
from jax.experimental import pallas as pl
from jax.experimental.pallas import tpu as pltpu

F32 = jnp.float32
BF16 = jnp.bfloat16

D_MODEL = 1024
DEPTH = 4
D_INNER = 2048
HEAD_DIM = 64
N_HEADS = 32
N_GROUPS = 8
HEADS_PER_GROUP = 4
D_STATE = 128
CONV_K = 4
BC_DIM = 2 * N_GROUPS * D_STATE
CONV_DIM = D_INNER + BC_DIM
ZX_DIM = D_INNER + CONV_DIM
SSD_CHUNK = 128
S5_GROUPS = 64
S5_GROUP = 16
S5_STATE = 64
S5_LANES = S5_GROUPS * S5_STATE
S5_BLOCKS = D_MODEL // 128
D_FF = 2816
RMS_EPS = 1e-6
LANES = 128
SUBLANES = 8
CONV_PAD = 8
N_XTILES = D_INNER // LANES
GROUP_W = HEADS_PER_GROUP * HEAD_DIM
SSD_STEP_BATCHES = 4
STATE_IN_BUFFERS = 3
FFN_SPLIT = 11
S5_GLU_SPLIT = 4
S5_BU_SPLIT = 4
INPROJ_COLS = 512
VMEM_LIMIT = 52 * 1024 * 1024


def _cparams(sem):
    return pltpu.CompilerParams(dimension_semantics=sem, vmem_limit_bytes=VMEM_LIMIT)


def _sigmoid(x):
    return 1.0 / (1.0 + jnp.exp(-x))


def _silu(x):
    hx = 0.5 * x
    return hx + hx * jnp.tanh(hx)


def _rows_by_batch(x, nb):
    tm, d = x.shape
    return x.reshape(tm // nb, nb, d)


def _modulate_rows(x, sh, sc, nb, ng=1):
    if ng > 1:
        x3 = x.reshape(ng, x.shape[0] // ng, x.shape[1])
        r = lax.rsqrt(jnp.mean(x3 * x3, axis=-1, keepdims=True) + RMS_EPS)
        return ((x3 * r) * (1.0 + sc) + sh).reshape(x.shape)
    if nb == 1:
        r = lax.rsqrt(jnp.mean(x * x, axis=-1, keepdims=True) + RMS_EPS)
        return (x * r) * (1.0 + sc) + sh
    x3 = _rows_by_batch(x, nb)
    r = lax.rsqrt(jnp.mean(x3 * x3, axis=-1, keepdims=True) + RMS_EPS)
    h = (x3 * r) * (1.0 + sc[None]) + sh[None]
    return h.reshape(x.shape)


def _gate_rows(v, g, nb, ng=1):
    if ng > 1:
        return (v.reshape(ng, v.shape[0] // ng, v.shape[1]) * g).reshape(v.shape)
    if nb == 1:
        return v * g
    return (_rows_by_batch(v, nb) * g[None]).reshape(v.shape)


def _dot(a, b):
    return jnp.dot(a, b, preferred_element_type=F32)


def _dot_nt(a, b):
    return lax.dot_general(a, b, (((1,), (1,)), ((), ())), preferred_element_type=F32)


def _dot_tn(a, b):
    return lax.dot_general(a, b, (((0,), (0,)), ((), ())), preferred_element_type=F32)


def _ada_kernel(c_ref, w_ref, b_ref, o_ref):
    s = _silu(c_ref[...]).astype(BF16)
    o_ref[0] = _dot(s, w_ref[0].astype(BF16)) + b_ref[0]


def _ada(c_all, ada_w, ada_b):
    nrow = c_all.shape[0]
    tn = 1536
    return pl.pallas_call(
        _ada_kernel,
        grid=(DEPTH, 6 * D_MODEL // tn),
        in_specs=[
            pl.BlockSpec((nrow, D_MODEL), lambda l, j: (0, 0)),
            pl.BlockSpec((1, D_MODEL, tn), lambda l, j: (l, 0, j)),
            pl.BlockSpec((1, 1, tn), lambda l, j: (l, 0, j)),
        ],
        out_specs=pl.BlockSpec((1, nrow, tn), lambda l, j: (l, 0, j)),
        out_shape=jax.ShapeDtypeStruct((DEPTH, nrow, 6 * D_MODEL), F32),
        compiler_params=_cparams(("arbitrary", "arbitrary")),
        name="ada",
    )(c_all, ada_w, ada_b.reshape(DEPTH, 1, 6 * D_MODEL))


def _inproj_kernel(x_ref, sh_ref, sc_ref, w_ref, wdt_ref, zx_ref, dt_ref, *, nb):
    h = _modulate_rows(x_ref[...], sh_ref[...], sc_ref[...], nb).astype(BF16)
    zx_ref[...] = _dot(h, w_ref[...])
    dt_ref[...] = _dot(h, wdt_ref[...])


def _inproj(x3, sh, sc, w, layer, wdt, tm):
    ng, rows, _ = x3.shape
    nb = sh.shape[1]
    tn = 2048
    per_g = pl.BlockSpec((None, nb, D_MODEL), lambda j, g, i: (g, 0, 0))
    return pl.pallas_call(
        functools.partial(_inproj_kernel, nb=nb),
        grid=(ZX_DIM // tn, ng, rows // tm),
        in_specs=[
            pl.BlockSpec((None, tm, D_MODEL), lambda j, g, i: (g, i, 0)),
            per_g, per_g,
            pl.BlockSpec((None, D_MODEL, tn), lambda j, g, i: (layer, 0, j)),
            pl.BlockSpec((D_MODEL, LANES), lambda j, g, i: (0, 0)),
        ],
        out_specs=[
            pl.BlockSpec((None, tm, tn), lambda j, g, i: (g, i, j)),
            pl.BlockSpec((None, None, tm, LANES), lambda j, g, i: (j, g, i, 0)),
        ],
        out_shape=[
            jax.ShapeDtypeStruct((ng, rows, ZX_DIM), F32),
            jax.ShapeDtypeStruct((ZX_DIM // tn, ng, rows, LANES), F32),
        ],
        compiler_params=_cparams(("arbitrary", "arbitrary", "arbitrary")),
        name="ssd_inproj",
    )(x3, sh, sc, w, wdt)


def _cumsum_rows(a, L):
    rows = lax.broadcasted_iota(jnp.int32, a.shape, 0)
    k = 1
    while k < L:
        a = a + jnp.where(rows >= k, pltpu.roll(a, k, 0), 0.0)
        k *= 2
    return a


def _transpose_rows(a, L):
    if L < LANES:
        a = jnp.concatenate([a, jnp.zeros((LANES - L, LANES), F32)], axis=0)
    return a.T[:, :L]


def _claim_layer0(ssm_out_ref, conv_out_ref):
    for ref in (ssm_out_ref, conv_out_ref):
        ref[1:] = jnp.zeros((ref.shape[0] - 1,) + ref.shape[1:], ref.dtype)
    return ssm_out_ref.at[0], conv_out_ref.at[0]


def _ssd_step_kernel(zx_ref, dt_ref, conv0_ref, ssm0_ref,
                     cw_ref, cb_ref, dtb_ref, alog_ref, dexp_ref, nw_ref, gsum_ref, expand_ref,
                     *rest, L, nbatch, first, layer, nsteps):
    y_ref, ssm_out_ref, conv_out_ref, pad, xbc_scr, state_buf, state_sem = rest[-7:]
    if first:
        ssm_out_ref, conv_out_ref = _claim_layer0(ssm_out_ref, conv_out_ref)

    step = pl.program_id(0)

    def state_copy(s, slot):
        return pltpu.make_async_copy(ssm0_ref.at[layer, pl.ds(s * nbatch, nbatch)],
                                     state_buf.at[slot], state_sem.at[slot])

    @pl.when(step == 0)
    def _():
        for s in range(min(STATE_IN_BUFFERS - 1, nsteps)):
            state_copy(s, s).start()

    ahead = step + (STATE_IN_BUFFERS - 1)

    @pl.when(ahead < nsteps)
    def _():
        state_copy(ahead, lax.rem(ahead, STATE_IN_BUFFERS)).start()

    slot = lax.rem(step, STATE_IN_BUFFERS)
    state_copy(step, slot).wait()
    lo = CONV_PAD - (CONV_K - 1)
    row_t = lax.broadcasted_iota(jnp.int32, (L, LANES), 0)
    b_lo, c_lo = D_INNER, D_INNER + N_GROUPS * D_STATE

    def row_bcast(i, s, lt0, lt1):
        return jnp.concatenate([jnp.broadcast_to(xbc_scr[i, lt, s:s + 1, :], (L, LANES))
                                for lt in range(lt0, lt1)], axis=1)

    xbcs, dts, acss, prods = [], [], [], []
    for i in range(nbatch):
        z0 = i * ZX_DIM
        conv_tiles = []
        for lt in range(2 * N_XTILES):
            cols = slice(lt * LANES, (lt + 1) * LANES)
            pad[i, lt, lo:CONV_PAD, :] = conv0_ref[i, :, cols]
            pad[i, lt, CONV_PAD:CONV_PAD + L, :] = zx_ref[:, z0 + D_INNER + lt * LANES:z0 + D_INNER + (lt + 1) * LANES]
            acc = cb_ref[:, cols]
            for k in range(CONV_K):
                acc = acc + cw_ref[k:k + 1, cols] * pad[i, lt, lo + k:lo + k + L, :]
            conv_tiles.append(_silu(acc))
            xbc_scr[i, lt] = conv_tiles[lt]
            conv_out_ref[i, :, cols] = pad[i, lt, lo + L:CONV_PAD + L, :]
        xbc = jnp.concatenate(conv_tiles, axis=1)
        dtr = dt_ref[:, i * LANES:(i + 1) * LANES] + dtb_ref[...]
        dt = jnp.maximum(dtr, 0.0) + jnp.log1p(jnp.exp(-jnp.abs(dtr)))
        xbcs.append(xbc)
        dts.append(dt)
        acss.append(_cumsum_rows(dt * (-jnp.exp(alog_ref[...])), L))
        cm = xbc[:, c_lo:]
        prods += [cm * row_bcast(i, s, b_lo // LANES, c_lo // LANES) for s in range(L)]

    p_hi, p_lo = _split_bf16(jnp.concatenate(prods, axis=0))
    cbh = _dot(jnp.concatenate([p_hi, p_lo], axis=1).astype(BF16), gsum_ref[...])

    per = L * L + 2 * L
    rows = []
    for i in range(nbatch):
        dt, acs = dts[i], acss[i]
        for s in range(L):
            decay = jnp.exp(jnp.where(row_t >= s, acs - acs[s:s + 1, :], -jnp.inf))
            r0 = (i * L + s) * L
            rows.append(cbh[r0:r0 + L, :] * decay * dt[s:s + 1, :])
        rows += [jnp.exp(acs), jnp.exp(acs[L - 1:L, :] - acs) * dt]
    w_hi, w_lo = _split_bf16(jnp.concatenate(rows, axis=0))
    wide = _dot(jnp.concatenate([w_hi, w_lo], axis=1).astype(BF16), expand_ref[...])

    for i in range(nbatch):
        z0, w0 = i * ZX_DIM, i * per
        xbc = xbcs[i]
        xs = xbc[:, :D_INNER]
        y = dexp_ref[...] * xs
        for s in range(L):
            y = y + wide[w0 + s * L:w0 + (s + 1) * L, :] * row_bcast(i, s, 0, N_XTILES)
        ea_x = wide[w0 + L * L:w0 + L * L + L, :]
        xw = xs * wide[w0 + L * L + L:w0 + per, :]
        e_last = jnp.exp(acss[i][L - 1:L, :])
        for g in range(N_GROUPS):
            gl = slice(g * GROUP_W, (g + 1) * GROUP_W)
            bg = xbc[:, b_lo + g * D_STATE:b_lo + (g + 1) * D_STATE].astype(BF16)
            cg = xbc[:, c_lo + g * D_STATE:c_lo + (g + 1) * D_STATE].astype(BF16)
            sg = state_buf[slot, i, gl, :]
            yg = y[:, gl] + ea_x[:, gl] * _dot_nt(cg, sg.astype(BF16))
            upd = _dot_tn(xw[:, gl].astype(BF16), bg)
            for r in range(HEADS_PER_GROUP):
                h = g * HEADS_PER_GROUP + r
                rr = slice(r * HEAD_DIM, (r + 1) * HEAD_DIM)
                ssm_out_ref[i, g * GROUP_W + r * HEAD_DIM:g * GROUP_W + (r + 1) * HEAD_DIM, :] = (
                    e_last[:, h:h + 1] * sg[rr, :] + upd[rr, :])
            yg = yg * _silu(zx_ref[:, z0 + g * GROUP_W:z0 + (g + 1) * GROUP_W])
            yg = yg * lax.rsqrt(jnp.mean(yg * yg, axis=-1, keepdims=True) + RMS_EPS)
            y0 = i * D_INNER + g * GROUP_W
            y_ref[:, y0:y0 + GROUP_W] = (yg * nw_ref[:, gl]).astype(y_ref.dtype)


def _split_bf16(v):
    hi = v.astype(BF16).astype(F32)
    return hi, (v - hi).astype(BF16).astype(F32)


def _ssd_chunk_kernel(x_ref, xn_ref, sh_ref, sc_ref, win_ref, wdt_ref, conv0_ref, ssm0_ref,
                      cw_ref, cb_ref, dtb_ref, alog_ref, dexp_ref, nw_ref, expand_ref, *rest, L, first):
    y_ref, ssm_out_ref, conv_out_ref, pad, state_t, z_scr, dt_scr = rest[-7:]
    c = pl.program_id(1)
    lo = CONV_PAD - (CONV_K - 1)
    all_slots = (ssm_out_ref, conv_out_ref)
    if first:
        ssm_out_ref, conv_out_ref = ssm_out_ref.at[0], conv_out_ref.at[0]

    def projection_steps(x_rows, slot):
        h = _modulate_rows(x_rows, sh_ref[...], sc_ref[...], 1).astype(BF16)
        pw = INPROJ_COLS

        def z_step(q):
            z_scr[slot, :, q * pw:(q + 1) * pw] = _dot(h, win_ref[:, q * pw:(q + 1) * pw])

        def xbc_step(q):
            res = _dot(h, win_ref[:, D_INNER + q * pw:D_INNER + (q + 1) * pw])
            for r in range(pw // LANES):
                pad[slot, q * (pw // LANES) + r, CONV_PAD:CONV_PAD + L, :] = res[:, r * LANES:(r + 1) * LANES]

        def dt_step():
            dt_scr[slot] = _dot(h, wdt_ref[...])

        return ([functools.partial(xbc_step, q) for q in range(CONV_DIM // pw)]
                + [functools.partial(z_step, q) for q in range(D_INNER // pw)] + [dt_step])

    @pl.when(c == 0)
    def _():
        state_t[...] = ssm0_ref[...].T
        for lt in range(2 * N_XTILES):
            pad[0, lt, lo:CONV_PAD, :] = conv0_ref[0, :, lt * LANES:(lt + 1) * LANES]
        for step in projection_steps(x_ref[:L, :], 0):
            step()

    body = functools.partial(_ssd_chunk_body, z_scr=z_scr, dt_scr=dt_scr, pad=pad, state_t=state_t,
                             cw_ref=cw_ref, cb_ref=cb_ref, dtb_ref=dtb_ref, alog_ref=alog_ref,
                             dexp_ref=dexp_ref, nw_ref=nw_ref, expand_ref=expand_ref, y_ref=y_ref, L=L)
    body(0, 1, slice(0, L), projection_steps(x_ref[L:, :], 1))
    body(1, 0, slice(L, 2 * L), projection_steps(xn_ref[...], 0))
    for lt in range(2 * N_XTILES):
        conv_out_ref[0, :, lt * LANES:(lt + 1) * LANES] = pad[0, lt, lo:CONV_PAD, :]

    @pl.when(c == pl.num_programs(1) - 1)
    def _():
        if first:
            _claim_layer0(*all_slots)
        ssm_out_ref[...] = state_t[...].T


def _ssd_chunk_body(slot, next_slot, rows, side_steps, *, z_scr, dt_scr, pad, state_t, cw_ref, cb_ref,
                    dtb_ref, alog_ref, dexp_ref, nw_ref, expand_ref, y_ref, L):
    lo = CONV_PAD - (CONV_K - 1)
    assert len(side_steps) <= 2 * N_XTILES // 4 + N_GROUPS
    pending = iter(side_steps)

    def side_point():
        step = next(pending, None)
        if step is not None:
            step()

    def conv_tile(lt):
        cols = slice(lt * LANES, (lt + 1) * LANES)
        acc = cb_ref[:, cols]
        for k in range(CONV_K):
            acc = acc + cw_ref[k:k + 1, cols] * pad[slot, lt, lo + k:lo + k + L, :]
        return _silu(acc)

    tiles = []
    for lt in range(2 * N_XTILES):
        tiles.append(conv_tile(lt))
        if lt % 4 == 3:
            side_point()

    for lt in range(2 * N_XTILES):
        pad[next_slot, lt, lo:CONV_PAD, :] = pad[slot, lt, lo + L:CONV_PAD + L, :]

    dtr = dt_scr[slot] + dtb_ref[...]
    dt = jnp.maximum(dtr, 0.0) + jnp.log1p(jnp.exp(-jnp.abs(dtr)))
    a = dt * (-jnp.exp(alog_ref[...]))
    acs = _cumsum_rows(a, L)
    acs_t = acs.T
    dt_t = dt.T
    a_last = acs[L - 1:L, :]
    ea_hi, ea_lo = _split_bf16(jnp.exp(acs))
    we_hi, we_lo = _split_bf16(jnp.exp(a_last - acs) * dt)
    parts = jnp.concatenate([jnp.concatenate([ea_hi, ea_lo], axis=1),
                             jnp.concatenate([we_hi, we_lo], axis=1)], axis=0).astype(BF16)
    wide = _dot(parts, expand_ref[...])
    ea_x = wide[:L]
    we_x = wide[L:]

    causal = (lax.broadcasted_iota(jnp.int32, (L, L), 0)
              >= lax.broadcasted_iota(jnp.int32, (L, L), 1))
    lane_head = lax.broadcasted_iota(jnp.int32, (L, GROUP_W), 1) // HEAD_DIM

    for g in range(N_GROUPS):
        gl = slice(g * GROUP_W, (g + 1) * GROUP_W)
        bg = tiles[N_XTILES + g].astype(BF16)
        cg = tiles[N_XTILES + N_GROUPS + g].astype(BF16)
        cb = _dot_nt(cg, bg)
        sg = state_t[:, gl]
        y_off = _dot(cg, sg.astype(BF16))
        ms = []
        for r in range(HEADS_PER_GROUP):
            h = g * HEADS_PER_GROUP + r
            seg = acs[:, h:h + 1] - acs_t[h:h + 1, :]
            decay = jnp.exp(jnp.where(causal, seg, -jnp.inf))
            ms.append((cb * decay * dt_t[h:h + 1, :]).astype(BF16))
        xg = jnp.concatenate([tiles[2 * g], tiles[2 * g + 1]], axis=1)
        xgb = xg.astype(BF16)
        xblk = jnp.concatenate([jnp.where(lane_head == r, xgb, jnp.zeros_like(xgb))
                                for r in range(HEADS_PER_GROUP)], axis=0)
        y_diag = _dot(jnp.concatenate(ms, axis=1), xblk)
        yg = y_diag + ea_x[:, gl] * y_off + dexp_ref[:, gl] * xg
        upd = _dot_tn(bg, (xg * we_x[:, gl]).astype(BF16))
        state_t[:, gl] = ea_x[L - 1:L, gl] * sg + upd
        yg = yg * _silu(z_scr[slot, :, gl])
        yg = yg * lax.rsqrt(jnp.mean(yg * yg, axis=-1, keepdims=True) + RMS_EPS)
        y_ref[rows, gl] = (yg * nw_ref[:, gl]).astype(y_ref.dtype)
        side_point()


def _ssm_state_spec(grid_rank, j, nb):
    shape = (None, nb, N_HEADS * HEAD_DIM, D_STATE)
    if grid_rank == 2:
        return pl.BlockSpec(shape, lambda b, c: (j, b, 0, 0))
    return pl.BlockSpec(shape, lambda b: (j, b, 0, 0))


def _ssd_call(body, grid, in_specs, args, y_spec, y_shape, conv_spec, ssm_all, conv_all, ssm_prev, conv_prev,
              j, nb, scratch, sem):
    first = ssm_prev is None
    body = functools.partial(body, first=first)
    in_specs, args, aliases = list(in_specs), list(args), {}
    state_spec = _ssm_state_spec(len(grid), j, nb)
    if first:
        assert j == 0
        n_layers = ssm_all.shape[0]
        at_layer0 = lambda *g: (0, g[0], 0, 0)
        state_spec = pl.BlockSpec((n_layers, nb) + ssm_all.shape[2:], at_layer0)
        conv_spec = pl.BlockSpec((n_layers, nb or 1) + conv_all.shape[2:], at_layer0)
    else:
        in_specs += [pl.BlockSpec(memory_space=pl.ANY)] * 2
        args += [ssm_prev, conv_prev]
        aliases = {len(args) - 2: 1, len(args) - 1: 2}
    return pl.pallas_call(
        body,
        grid=grid,
        in_specs=in_specs,
        out_specs=[y_spec, state_spec, conv_spec],
        out_shape=[
            jax.ShapeDtypeStruct(y_shape, BF16),
            jax.ShapeDtypeStruct(ssm_all.shape, F32),
            jax.ShapeDtypeStruct(conv_all.shape, F32),
        ],
        scratch_shapes=scratch,
        input_output_aliases=aliases,
        compiler_params=_cparams(sem),
        name="ssd_core",
    )(*args)


def _ssd_chunked(x, sh, sc, w_in, w_dt, conv_all, ssm_all, ssm_prev, conv_prev, j, cw, cb, dtb, alog, dexp, nw):
    B, T, _ = x.shape
    L = SSD_CHUNK
    nsteps = T // (2 * L)
    last_chunk = T // L - 1
    full = lambda shape: pl.BlockSpec(shape, lambda b, c: (0,) * len(shape))
    resident = lambda shape: pl.BlockSpec(shape, lambda b, c: (0,) * len(shape), pipeline_mode=pl.Buffered(1))
    pair = lambda w: pl.BlockSpec((None, 2 * L, w), lambda b, c: (b, c, 0))
    nxt = pl.BlockSpec((None, L, D_MODEL), lambda b, c: (b, jnp.minimum(2 * c + 2, last_chunk), 0))
    per_b = pl.BlockSpec((None, 1, D_MODEL), lambda b, c: (b, 0, 0))
    conv_spec = pl.BlockSpec((None, 1, CONV_K - 1, CONV_DIM), lambda b, c: (j, b, 0, 0))
    w_in_spec = pl.BlockSpec((None, D_MODEL, ZX_DIM), lambda b, c: (j, 0, 0), pipeline_mode=pl.Buffered(1))
    sel = (jnp.arange(D_INNER)[None, :] // HEAD_DIM == jnp.arange(LANES)[:, None]).astype(BF16)
    in_specs = [
        pair(D_MODEL), nxt, per_b, per_b, w_in_spec, resident((D_MODEL, LANES)),
        conv_spec, _ssm_state_spec(2, j, None),
        full((CONV_K, CONV_DIM)), full((1, CONV_DIM)), full((1, LANES)), full((1, LANES)),
        full((1, D_INNER)), full((1, D_INNER)), resident((2 * LANES, D_INNER)),
    ]
    args = [x, x, sh, sc, w_in, w_dt, conv_all, ssm_all, cw, cb, dtb, alog, dexp, nw,
            jnp.concatenate([sel, sel], axis=0)]
    scratch = [
        pltpu.VMEM((2, 2 * N_XTILES, CONV_PAD + L, LANES), F32),
        pltpu.VMEM((D_STATE, N_HEADS * HEAD_DIM), F32),
        pltpu.VMEM((2, L, D_INNER), F32),
        pltpu.VMEM((2, L, LANES), F32),
    ]
    return _ssd_call(functools.partial(_ssd_chunk_kernel, L=L), (B, nsteps), in_specs, args,
                     pair(D_INNER), (B, T, D_INNER), conv_spec, ssm_all, conv_all, ssm_prev, conv_prev, j, None,
                     scratch, ("arbitrary", "arbitrary"))


def _ssd_step(zx, dtraw, conv_all, ssm_all, ssm_prev, conv_prev, j, cw, cb, dtb, alog, dexp, nw, B):
    T = zx.shape[0]
    nb = SSD_STEP_BATCHES
    full = lambda shape: pl.BlockSpec(shape, lambda b: (0,) * len(shape))
    col = lambda w: pl.BlockSpec((T, nb * w), lambda b: (0, b))
    conv_spec = pl.BlockSpec((None, nb, CONV_K - 1, CONV_DIM), lambda b: (j, b, 0, 0))
    in_specs = [
        col(ZX_DIM), col(LANES), conv_spec, pl.BlockSpec(memory_space=pl.ANY),
        full((CONV_K, CONV_DIM)), full((1, CONV_DIM)), full((1, LANES)), full((1, LANES)),
        full((1, D_INNER)), full((1, D_INNER)),
        full((2 * N_GROUPS * D_STATE, LANES)), full((2 * LANES, D_INNER)),
    ]
    head_ids = jnp.arange(LANES)
    gsum = ((jnp.arange(N_GROUPS * D_STATE)[:, None] // D_STATE == head_ids[None, :] // HEADS_PER_GROUP)
            & (head_ids[None, :] < N_HEADS)).astype(BF16)
    sel = (jnp.arange(D_INNER)[None, :] // HEAD_DIM == head_ids[:, None]).astype(BF16)
    args = [zx, dtraw, conv_all, ssm_all, cw, cb, dtb, alog, dexp, nw,
            jnp.concatenate([gsum, gsum], axis=0), jnp.concatenate([sel, sel], axis=0)]
    scratch = [pltpu.VMEM((nb, 2 * N_XTILES, CONV_PAD + T, LANES), F32),
               pltpu.VMEM((nb, 2 * N_XTILES, T, LANES), F32),
               pltpu.VMEM((STATE_IN_BUFFERS, nb, N_HEADS * HEAD_DIM, D_STATE), F32),
               pltpu.SemaphoreType.DMA((STATE_IN_BUFFERS,))]
    body = functools.partial(_ssd_step_kernel, L=T, nbatch=nb, layer=j, nsteps=B // nb)
    return _ssd_call(body, (B // nb,), in_specs, args,
                     col(D_INNER), (T, B * D_INNER), conv_spec, ssm_all, conv_all, ssm_prev, conv_prev, j, nb,
                     scratch, ("arbitrary",))


def _ffn_kernel(*refs, nb, ng, has_proj, final, relayout):
    if relayout:
        refs, relay = refs[:-1], refs[-1]
    if has_proj:
        x_ref, y_ref, ow_ref, g1_ref, sh_ref, sc_ref, g2_ref, win_ref, wo_ref, fw_ref, o_ref = refs
    else:
        x_ref, sh_ref, sc_ref, g2_ref, win_ref, wo_ref, fw_ref, o_ref = refs
    x = x_ref[...]
    tm = x.size // D_MODEL
    x = x.reshape(tm, D_MODEL)
    if has_proj:
        y = y_ref[...].reshape(tm, D_INNER)
        x = x + _gate_rows(_dot(y, ow_ref[...]), g1_ref[...], nb, ng)
    h = _modulate_rows(x, sh_ref[...], sc_ref[...], nb, ng).astype(BF16)
    tf = D_FF // FFN_SPLIT
    acc = None
    for c in range(FFN_SPLIT):
        gate = _dot(h, win_ref[:, c * tf:(c + 1) * tf])
        up = _dot(h, win_ref[:, D_FF + c * tf:D_FF + (c + 1) * tf])
        part = _dot((_silu(gate) * up).astype(BF16), wo_ref[c * tf:(c + 1) * tf, :])
        acc = part if acc is None else acc + part
    x2 = x + _gate_rows(acc, g2_ref[...], nb, ng)
    if final:
        x2 = x2 * lax.rsqrt(jnp.mean(x2 * x2, axis=-1, keepdims=True) + RMS_EPS) * fw_ref[...]
    n_lt = D_MODEL // LANES
    if relayout == "to_batch_major":
        for lt in range(n_lt):
            relay[lt] = x2[:, lt * LANES:(lt + 1) * LANES]
        for b in range(nb):
            o_ref[b] = jnp.concatenate([relay[lt, pl.ds(b, tm // nb, stride=nb), :] for lt in range(n_lt)], axis=1)
    elif relayout == "to_time_major":
        tg = tm // ng
        for lt in range(n_lt):
            for g in range(ng):
                relay[lt, pl.ds(g, tg, stride=ng), :] = x2[g * tg:(g + 1) * tg, lt * LANES:(lt + 1) * LANES]
        o_ref[...] = jnp.concatenate([relay[lt] for lt in range(n_lt)], axis=1).reshape(o_ref.shape)
    else:
        o_ref[...] = x2.reshape(o_ref.shape)


def _ffn(x3, y3, out_w, proj_layer, g1, sh, sc, g2, w_in, w_out, layer, fw, tm, final, relayout=None):
    ng, rows, _ = x3.shape
    nb = sh.shape[1]
    has_proj = y3 is not None
    resident = lambda shape, l: pl.BlockSpec((None,) + shape, lambda g, i: (l,) + (0,) * len(shape),
                                             pipeline_mode=pl.Buffered(1))
    if relayout == "to_time_major":
        assert nb == 1
        tg = tm // ng
        row_blk = lambda w: pl.BlockSpec((ng, tg, w), lambda g, i: (0, i, 0))
        per_g = pl.BlockSpec((ng, 1, D_MODEL), lambda g, i: (0, 0, 0))
        grid = (1, rows // tg)
        out_spec = pl.BlockSpec((tg, ng, D_MODEL), lambda g, i: (i, 0, 0))
        out_shape = (rows, ng, D_MODEL)
        kernel_ng = ng
    else:
        row_blk = lambda w: pl.BlockSpec((None, tm, w), lambda g, i: (g, i, 0))
        per_g = pl.BlockSpec((None, nb, D_MODEL), lambda g, i: (g, 0, 0))
        grid = (ng, rows // tm)
        kernel_ng = 1
        if relayout == "to_batch_major":
            assert ng == 1
            out_spec = pl.BlockSpec((nb, tm // nb, D_MODEL), lambda g, i: (0, i, 0))
            out_shape = (nb, rows // nb, D_MODEL)
        else:
            out_spec = row_blk(D_MODEL)
            out_shape = (ng, rows, D_MODEL)
    in_specs = [row_blk(D_MODEL)]
    args = [x3]
    if has_proj:
        in_specs += [row_blk(D_INNER), resident((D_INNER, D_MODEL), proj_layer), per_g]
        args += [y3, out_w, g1]
    in_specs += [per_g, per_g, per_g,
                 resident((D_MODEL, 2 * D_FF), layer), resident((D_FF, D_MODEL), layer),
                 pl.BlockSpec((1, D_MODEL), lambda g, i: (0, 0))]
    args += [sh, sc, g2, w_in, w_out, fw]
    return pl.pallas_call(
        functools.partial(_ffn_kernel, nb=nb, ng=kernel_ng, has_proj=has_proj, final=final, relayout=relayout),
        grid=grid,
        in_specs=in_specs,
        out_specs=out_spec,
        out_shape=jax.ShapeDtypeStruct(out_shape, F32),
        scratch_shapes=[pltpu.VMEM((D_MODEL // LANES, tm, LANES), F32)] if relayout else [],
        compiler_params=_cparams(("arbitrary", "arbitrary")),
        name="proj_ffn" if has_proj else "ffn",
    )(*args)


def _s5_disc_kernel(are_ref, aim_ref, ldt_ref, bre_ref, bim_ref, abre_ref, abim_ref, bbre_ref, bbim_ref):
    lre, lim = are_ref[...], aim_ref[...]
    dt = jnp.exp(ldt_ref[...])
    mag = jnp.exp(lre * dt)
    ab_re, ab_im = mag * jnp.cos(lim * dt), mag * jnp.sin(lim * dt)
    den = lre * lre + lim * lim
    nr, ni = ab_re - 1.0, ab_im
    q_re = (nr * lre + ni * lim) / den
    q_im = (ni * lre - nr * lim) / den
    abre_ref[...] = ab_re
    abim_ref[...] = ab_im
    br, bi = bre_ref[...], bim_ref[...]
    bbre_ref[...] = q_re[:, None, :] * br - q_im[:, None, :] * bi
    bbim_ref[...] = q_re[:, None, :] * bi + q_im[:, None, :] * br


def _s5_disc(a_re, a_im, log_dt, b_re, b_im):
    gn = jax.ShapeDtypeStruct((S5_GROUPS, S5_STATE), F32)
    gin = jax.ShapeDtypeStruct((S5_GROUPS, S5_GROUP, S5_STATE), F32)
    return pl.pallas_call(
        _s5_disc_kernel,
        out_shape=[gn, gn, gin, gin],
        name="s5_disc",
    )(a_re, a_im, log_dt.reshape(S5_GROUPS, 1), b_re, b_im)


def _s5_kernel(x_ref, sh_ref, sc_ref, g1_ref, bb_ref, cc_ref, are_ref, aim_ref, s0re_ref, s0im_ref,
               dskip_ref, glu_ref, o_ref, sre_ref, sim_ref, xs_re, xs_im, st_re, st_im, *, tb, bb):
    t_idx = pl.program_id(1)
    rows = tb * bb

    @pl.when(t_idx == 0)
    def _():
        st_re[...] = s0re_ref[...]
        st_im[...] = s0im_ref[...]

    x = x_ref[...].reshape(rows, D_MODEL)
    h = _modulate_rows(x, sh_ref[...], sc_ref[...], bb)
    hb = h.astype(BF16)
    half = S5_LANES // S5_BLOCKS
    pw = 2 * half // S5_BU_SPLIT

    def bu_pieces(k):
        out = [None] * S5_BU_SPLIT

        def piece(j):
            out[j] = _dot(hb[:, k * LANES:(k + 1) * LANES], bb_ref[k, :, j * pw:(j + 1) * pw])

        return out, [functools.partial(piece, j) for j in range(S5_BU_SPLIT)]

    def y_pieces(k):
        out = [None] * 2

        def piece(j):
            src = xs_im if j else xs_re
            out[j] = _dot(src[:, k * half:(k + 1) * half].astype(BF16), cc_ref[k, j * half:(j + 1) * half, :])

        return out, [functools.partial(piece, j) for j in range(2)]

    n_iter = (bb // SUBLANES) * tb
    bu_next, steps = bu_pieces(0)
    for step in steps:
        step()
    y_parts = [None] * S5_BLOCKS
    for k in range(S5_BLOCKS):
        ks = slice(k * half, (k + 1) * half)
        bu = jnp.concatenate(bu_next, axis=1)
        side = []
        if k + 1 < S5_BLOCKS:
            bu_next, steps = bu_pieces(k + 1)
            side += steps
        if k > 0:
            y_parts[k - 1], steps = y_pieces(k - 1)
            side += steps
        run_at = {((j + 1) * n_iter) // (len(side) + 1): step for j, step in enumerate(side)}
        ar, ai = are_ref[:, ks], aim_ref[:, ks]
        it = 0
        for rg in range(bb // SUBLANES):
            rs = slice(rg * SUBLANES, (rg + 1) * SUBLANES)
            xr, xi = st_re[rs, ks], st_im[rs, ks]
            for t in range(tb):
                sl = slice(t * bb + rg * SUBLANES, t * bb + (rg + 1) * SUBLANES)
                xr, xi = (ar * xr - ai * xi + bu[sl, :half], ar * xi + ai * xr + bu[sl, half:])
                xs_re[sl, ks] = xr
                xs_im[sl, ks] = xi
                it += 1
                if it in run_at:
                    run_at[it]()
            st_re[rs, ks] = xr
            st_im[rs, ks] = xi
    y_parts[S5_BLOCKS - 1], steps = y_pieces(S5_BLOCKS - 1)
    for step in steps:
        step()
    y = jnp.concatenate([p[0] - p[1] for p in y_parts], axis=1) + dskip_ref[...] * h
    y = 0.5 * y * (1.0 + jnp.tanh(math.sqrt(2.0 / math.pi) * (y + 0.044715 * (y * y * y))))
    yb = y.astype(BF16)
    cw = D_MODEL // S5_GLU_SPLIT
    out = jnp.concatenate(
        [_dot(yb, glu_ref[:, c * cw:(c + 1) * cw])
         * _sigmoid(_dot(yb, glu_ref[:, D_MODEL + c * cw:D_MODEL + (c + 1) * cw]))
         for c in range(S5_GLU_SPLIT)], axis=1)
    o_ref[...] = (x + _gate_rows(out, g1_ref[...], bb)).reshape(tb, bb, D_MODEL)

    @pl.when(t_idx == pl.num_programs(1) - 1)
    def _():
        sre_ref[...] = st_re[...]
        sim_ref[...] = st_im[...]


def _s5_layer(x3, sh, sc, g1, bbd, ccd, a_re, a_im, s0re, s0im, dskip, glu_w, tb, bb):
    T, B, _ = x3.shape
    per_b = pl.BlockSpec((bb, D_MODEL), lambda i, t: (i, 0))
    st = pl.BlockSpec((bb, S5_LANES), lambda i, t: (i, 0))
    full = lambda shape: pl.BlockSpec(shape, lambda i, t: (0,) * len(shape))
    return pl.pallas_call(
        functools.partial(_s5_kernel, tb=tb, bb=bb),
        grid=(B // bb, T // tb),
        in_specs=[
            pl.BlockSpec((tb, bb, D_MODEL), lambda i, t: (t, i, 0)),
            per_b, per_b, per_b,
            full((S5_BLOCKS, LANES, 2 * S5_LANES // S5_BLOCKS)),
            full((S5_BLOCKS, 2 * S5_LANES // S5_BLOCKS, LANES)),
            full((SUBLANES, S5_LANES)),
            full((SUBLANES, S5_LANES)),
            st, st,
            full((1, D_MODEL)),
            full((D_MODEL, 2 * D_MODEL)),
        ],
        out_specs=[pl.BlockSpec((tb, bb, D_MODEL), lambda i, t: (t, i, 0)), st, st],
        out_shape=[
            jax.ShapeDtypeStruct((T, B, D_MODEL), F32),
            jax.ShapeDtypeStruct((B, S5_LANES), F32),
            jax.ShapeDtypeStruct((B, S5_LANES), F32),
        ],
        scratch_shapes=[
            pltpu.VMEM((tb * bb, S5_LANES), F32),
            pltpu.VMEM((tb * bb, S5_LANES), F32),
            pltpu.VMEM((bb, S5_LANES), F32),
            pltpu.VMEM((bb, S5_LANES), F32),
        ],
        compiler_params=_cparams(("arbitrary", "arbitrary")),
        name="s5_layer",
    )(x3, sh, sc, g1, bbd, ccd, a_re, a_im, s0re, s0im, dskip, glu_w)


def _block_diag(w):
    _, r, c = w.shape
    w = w.reshape(S5_BLOCKS, 8, r, c)
    eye = jnp.eye(8, dtype=w.dtype)
    return (w[:, :, :, None, :] * eye[None, :, None, :, None]).reshape(S5_BLOCKS, 8 * r, 8 * c)


def _prep_params(p):
    w = {}
    w["ssd_in_w"] = p["ssd_in_w"].astype(BF16)
    w["ssd_dt_w"] = [jnp.pad(p["ssd_in_w"][j, :, ZX_DIM:], ((0, 0), (0, LANES - N_HEADS))).astype(BF16)
                     for j in range(2)]
    pad_h = lambda v: jnp.pad(v, (0, LANES - N_HEADS)).reshape(1, LANES)
    w["dt_bias"] = [pad_h(p["ssd_dt_bias"][j]) for j in range(2)]
    w["a_log"] = [pad_h(p["ssd_A_log"][j]) for j in range(2)]
    w["d_exp"] = [jnp.repeat(p["ssd_D"][j], HEAD_DIM).reshape(1, D_INNER) for j in range(2)]
    w["norm_w"] = [p["ssd_norm_w"][j].reshape(1, D_INNER) for j in range(2)]
    w["conv_w"] = [p["ssd_conv_w"][j] for j in range(2)]
    w["conv_b"] = [p["ssd_conv_b"][j].reshape(1, CONV_DIM) for j in range(2)]
    w["ssd_out_w"] = p["ssd_out_w"].astype(BF16)
    w["ffn_w_in"] = p["ffn_w_in"].astype(BF16)
    w["ffn_w_out"] = p["ffn_w_out"].astype(BF16)
    w["final_w"] = p["final_norm_w"].reshape(1, D_MODEL)
    w["s5"] = []
    for j in range(2):
        ab_re, ab_im, bb_re, bb_im = _s5_disc(
            p["s5_A_re"][j], p["s5_A_im"][j], p["s5_log_dt"][j],
            jnp.swapaxes(p["s5_B_re"][j], 1, 2), jnp.swapaxes(p["s5_B_im"][j], 1, 2))
        bbd = jnp.concatenate([_block_diag(bb_re), _block_diag(bb_im)], axis=2).astype(BF16)
        c_re = jnp.swapaxes(p["s5_C_re"][j], 1, 2)
        c_im = jnp.swapaxes(p["s5_C_im"][j], 1, 2)
        ccd = jnp.concatenate([_block_diag(c_re), _block_diag(c_im)], axis=1).astype(BF16)
        bro = lambda v: jnp.broadcast_to(v.reshape(1, S5_LANES), (SUBLANES, S5_LANES))
        w["s5"].append(dict(bbd=bbd, ccd=ccd, a_re=bro(ab_re), a_im=bro(ab_im),
                            dskip=p["s5_D"][j].reshape(1, D_MODEL),
                            glu_w=p["s5_glu_w"][j].astype(BF16)))
    return w


def _trunk(x, mods, ssm0, conv0, re0, im0, w, tm, s5_tb, s5_bb, ssd_batch_major):
    B, T, _ = x.shape
    assert ssd_batch_major == (T % (2 * SSD_CHUNK) == 0)
    ssm_new = conv_new = None
    re_new, im_new = [], []
    if not ssd_batch_major:
        x = jnp.swapaxes(x, 0, 1)
    for i in range(DEPTH):
        j = i // 2
        final = i == DEPTH - 1
        parts = [mods[i][:, k * D_MODEL:(k + 1) * D_MODEL] for k in range(6)]
        by_rows = [v[None] for v in parts]
        by_batch = [v[:, None] for v in parts]
        ffn_w = (w["ffn_w_in"], w["ffn_w_out"], i, w["final_w"], tm, final)
        if i % 2 == 0:
            sh1, sc1, g1, sh2, sc2, g2 = by_batch if ssd_batch_major else by_rows
            x3 = x if ssd_batch_major else x.reshape(1, T * B, D_MODEL)
            ssd_w = (w["conv_w"][j], w["conv_b"][j], w["dt_bias"][j], w["a_log"][j], w["d_exp"][j],
                     w["norm_w"][j])
            if ssd_batch_major:
                y, ssm_new, conv_new = _ssd_chunked(x3, sh1, sc1, w["ssd_in_w"], w["ssd_dt_w"][j],
                                                    conv0, ssm0, ssm_new, conv_new, j, *ssd_w)
                x = _ffn(x3, y, w["ssd_out_w"], j, g1, sh2, sc2, g2, *ffn_w, relayout="to_time_major")
            else:
                zx, dtraw = _inproj(x3, sh1, sc1, w["ssd_in_w"], j, w["ssd_dt_w"][j], tm)
                y, ssm_new, conv_new = _ssd_step(zx.reshape(T, B * ZX_DIM), dtraw[0].reshape(T, B * LANES),
                                                 conv0, ssm0, ssm_new, conv_new, j, *ssd_w, B)
                x = _ffn(x3, y.reshape(1, T * B, D_INNER), w["ssd_out_w"], j, g1, sh2, sc2, g2,
                         *ffn_w).reshape(T, B, D_MODEL)
        else:
            s5 = w["s5"][j]
            sh1, sc1, g1 = [v[0] for v in by_rows[:3]]
            xs5, sr, si = _s5_layer(x, sh1, sc1, g1, s5["bbd"], s5["ccd"], s5["a_re"], s5["a_im"],
                                    re0[j], im0[j], s5["dskip"], s5["glu_w"], s5_tb, s5_bb)
            re_new.append(sr)
            im_new.append(si)
            if ssd_batch_major:
                x = _ffn(xs5.reshape(1, T * B, D_MODEL), None, None, None, None, *by_rows[3:], *ffn_w,
                         relayout="to_batch_major")
            else:
                x = _ffn(xs5.reshape(1, T * B, D_MODEL), None, None, None, None, *by_rows[3:],
                         *ffn_w).reshape(T, B, D_MODEL)
    if not ssd_batch_major:
        x = jnp.swapaxes(x, 0, 1)
    return x, ssm_new, conv_new, jnp.stack(re_new), jnp.stack(im_new)


def kernel(x_prompt, x_sample, state_ssm, state_conv, state_s5_re, state_s5_im, c_prompt, c_sample,
           ada_w, ada_b, ssd_in_w, ssd_conv_w, ssd_conv_b, ssd_dt_bias, ssd_A_log, ssd_D, ssd_norm_w,
           ssd_out_w, s5_A_re, s5_A_im, s5_log_dt, s5_B_re, s5_B_im, s5_C_re, s5_C_im, s5_D, s5_glu_w,
           ffn_w_in, ffn_w_out, final_norm_w):
    p = dict(ssd_in_w=ssd_in_w, ssd_conv_w=ssd_conv_w, ssd_conv_b=ssd_conv_b, ssd_dt_bias=ssd_dt_bias,
             ssd_A_log=ssd_A_log, ssd_D=ssd_D, ssd_norm_w=ssd_norm_w, ssd_out_w=ssd_out_w,
             s5_A_re=s5_A_re, s5_A_im=s5_A_im, s5_log_dt=s5_log_dt, s5_B_re=s5_B_re, s5_B_im=s5_B_im,
             s5_C_re=s5_C_re, s5_C_im=s5_C_im, s5_D=s5_D, s5_glu_w=s5_glu_w,
             ffn_w_in=ffn_w_in, ffn_w_out=ffn_w_out, final_norm_w=final_norm_w)
    w = _prep_params(p)
    bp, tp, _ = x_prompt.shape
    bs, ts, _ = x_sample.shape

    mods = _ada(jnp.concatenate([c_prompt, c_sample], axis=0), ada_w, ada_b)
    mods_p = [mods[l, :bp] for l in range(DEPTH)]
    mods_s = [mods[l, bp:] for l in range(DEPTH)]

    n_ssd, n_s5 = state_ssm.shape[0], state_s5_re.shape[0]
    rows_hp = N_HEADS * HEAD_DIM
    zeros_p = (jnp.zeros((n_ssd, bp, rows_hp, D_STATE), state_ssm.dtype),
               jnp.zeros((n_ssd, bp, CONV_K - 1, CONV_DIM), state_conv.dtype),
               jnp.zeros((n_s5, bp, S5_LANES), state_s5_re.dtype),
               jnp.zeros((n_s5, bp, S5_LANES), state_s5_im.dtype))
    yp, ssm_p, conv_p, re_p, im_p = _trunk(
        x_prompt, mods_p, *zeros_p, w,
        tm=512, s5_tb=512 // bp, s5_bb=bp, ssd_batch_major=True)
    ys, ssm_s, conv_s, re_s, im_s = _trunk(
        x_sample, mods_s,
        state_ssm.reshape(n_ssd, bs, rows_hp, D_STATE), state_conv,
        state_s5_re.reshape(n_s5, bs, S5_LANES), state_s5_im.reshape(n_s5, bs, S5_LANES), w,
        tm=512, s5_tb=ts, s5_bb=512 // ts, ssd_batch_major=False)

    ssm_shape = lambda b: (n_ssd, b, N_HEADS, HEAD_DIM, D_STATE)
    s5_shape = lambda b: (n_s5, b, S5_GROUPS, S5_STATE)
    return (yp, ys,
            ssm_p.reshape(ssm_shape(bp)), conv_p, re_p.reshape(s5_shape(bp)), im_p.reshape(s5_shape(bp)),
            ssm_s.reshape(ssm_shape(bs)), conv_s, re_s.reshape(s5_shape(bs)), im_s.reshape(s5_shape(bs)))
```
